```python
import math
import jax, jax.numpy as jnp
from jax import lax
import numpy as np

D_MODEL = 1024
BATCH = 16
SEQ = 4096
DEPTH = 4

CTX_LEN = 256
GRID_W = 64

MLA_HEADS = 8
MLA_Q_RANK = 384
MLA_KV_RANK = 256
MLA_NOPE = 64
MLA_ROPE = 32
MLA_V = 64
MLA_QK = MLA_NOPE + MLA_ROPE
MLA_SCALE = MLA_QK ** -0.5
ATTN_BLOCK = 128
ROPE_BASE = 10000.0

HG_HEADS = 4
HG_DK = 64
HG_DV = 64
HG_CHUNK = 64

SSM_HEADS = 4
SSM_HEADDIM = 64
SSM_GROUPS = 2
SSM_STATE = 64
SSM_CONV = 5
SSM_CHUNK = 128
SSM_INNER = SSM_HEADS * SSM_HEADDIM
SSM_XBC = SSM_INNER + 2 * SSM_GROUPS * SSM_STATE

FFN_HIDDEN = -(-8 * D_MODEL // (3 * 256)) * 256

MLA_COLS = MLA_Q_RANK + MLA_KV_RANK + MLA_ROPE
HG_W = HG_HEADS * HG_DK
HG_COLS = 3 * HG_W + 2 * HG_HEADS * HG_DV
SSM_COLS = SSM_INNER + SSM_XBC + 2 * SSM_HEADS
IN_COLS = MLA_COLS + HG_COLS + SSM_COLS
MIX_WIDTH = MLA_HEADS * MLA_V + HG_HEADS * HG_DV + SSM_INNER

kernel_name = "hybrid_mla_hgrn2_mamba2_diffusion_block"


def rms_norm(x, g, eps=1e-6):
    xf = x.astype(jnp.float32)
    y = xf * lax.rsqrt(jnp.mean(xf * xf, axis=-1, keepdims=True) + eps)
    return (y * g.astype(jnp.float32)).astype(x.dtype)


def modulate(h, shift, scale):
    return h * (1 + scale) + shift


def swiglu(h, w_in, w_out):
    a, b = jnp.split(h @ w_in, 2, axis=-1)
    return (jax.nn.silu(a) * b) @ w_out


def axial_rope_tables(n):
    rows = n // GRID_W
    row = jnp.repeat(jnp.arange(rows, dtype=jnp.float32), GRID_W)
    col = jnp.tile(jnp.arange(GRID_W, dtype=jnp.float32), rows)
    n_freq = MLA_ROPE // 4
    inv = ROPE_BASE ** (-jnp.arange(n_freq, dtype=jnp.float32) / n_freq)
    ang = jnp.stack([row[:, None] * inv, col[:, None] * inv], axis=1)
    return jnp.cos(ang), jnp.sin(ang)


def apply_axial_rope(x, cos, sin):
    xr = x.astype(jnp.float32).reshape(x.shape[:-1] + (2, 2, MLA_ROPE // 4))
    x1, x2 = xr[..., 0, :], xr[..., 1, :]
    out = jnp.stack([x1 * cos - x2 * sin, x2 * cos + x1 * sin], axis=-2)
    return out.reshape(x.shape).astype(x.dtype)


def mla_project(p, qa_g, wqb, kva_g, wkvb):
    bsz, n, _ = p.shape
    q_lat = rms_norm(p[..., :MLA_Q_RANK], qa_g)
    kv_lat = rms_norm(p[..., MLA_Q_RANK:MLA_Q_RANK + MLA_KV_RANK], kva_g)
    k_rope = p[..., MLA_Q_RANK + MLA_KV_RANK:MLA_COLS]
    q = (q_lat @ wqb).reshape(bsz, n, MLA_HEADS, MLA_QK)
    kv = (kv_lat @ wkvb).reshape(bsz, n, MLA_HEADS, MLA_NOPE + MLA_V)
    return q[..., :MLA_NOPE], q[..., MLA_NOPE:], kv[..., :MLA_NOPE], kv[..., MLA_NOPE:], k_rope


def mla_probs(qn, qr, kn, kr):
    s = jnp.einsum('bhqd,bhkd->bhqk', qn, kn) + jnp.einsum('bhqr,bkr->bhqk', qr, kr)
    return jax.nn.softmax(s.astype(jnp.float32) * MLA_SCALE, axis=-1)


def mla_mixer(p_ctx, p_lat, qa_g, wqb, kva_g, wkvb, cos, sin, with_ctx_out):
    bsz, n, _ = p_lat.shape
    n_ctx = p_ctx.shape[1]
    qn_c, qr_c, kn_c, v_c, kr_c = mla_project(p_ctx, qa_g, wqb, kva_g, wkvb)
    qn, qr, kn, v, kr = mla_project(p_lat, qa_g, wqb, kva_g, wkvb)
    qr = apply_axial_rope(qr, cos[:, None], sin[:, None])
    kr = apply_axial_rope(kr, cos, sin)
    heads = lambda a: a.transpose(0, 2, 1, 3)
    kn_all = jnp.concatenate([heads(kn_c), heads(kn)], axis=2)
    v_all = jnp.concatenate([heads(v_c), heads(v)], axis=2)
    kr_all = jnp.concatenate([kr_c, kr], axis=1)
    nb = n // ATTN_BLOCK
    to_blocks = lambda a: a.reshape(bsz, nb, ATTN_BLOCK, MLA_HEADS, a.shape[-1]).transpose(1, 0, 3, 2, 4)

    def block(qs):
        qn_b, qr_b = qs
        pr = mla_probs(qn_b, qr_b, kn_all, kr_all)
        return jnp.einsum('bhqk,bhkd->bhqd', pr.astype(v_all.dtype), v_all)

    o = lax.map(block, (to_blocks(qn), to_blocks(qr)))
    o_lat = o.transpose(1, 0, 3, 2, 4).reshape(bsz, n, MLA_HEADS * MLA_V)
    o_ctx = None
    if with_ctx_out:
        pr = mla_probs(heads(qn_c), heads(qr_c), heads(kn_c), kr_c)
        o_ctx = jnp.einsum('bhqk,bhkd->bqhd', pr.astype(v_c.dtype), heads(v_c)).reshape(bsz, n_ctx, MLA_HEADS * MLA_V)
    return o_lat, o_ctx


def gla_chunk_scan(q, k, logf, v, s0):
    bsz, nh, n, dk = q.shape
    dv = v.shape[-1]
    nc = n // HG_CHUNK
    to_chunks = lambda a: a.reshape(bsz, nh, nc, HG_CHUNK, a.shape[-1]).transpose(2, 0, 1, 3, 4)
    tri = jnp.tril(jnp.ones((HG_CHUNK, HG_CHUNK), bool))[:, :, None]

    def step(S, blk):
        qc, kc, gc, vc = blk
        b = jnp.cumsum(gc, axis=2)
        diff = b[:, :, :, None, :] - b[:, :, None, :, :]
        dec = jnp.where(tri, jnp.exp(jnp.where(tri, diff, 0.0)), 0.0)
        att = jnp.einsum('bhtk,bhsk,bhtsk->bhts', qc, kc, dec)
        o = jnp.einsum('bhts,bhsv->bhtv', att, vc) + jnp.einsum('bhtk,bhkv->bhtv', qc * jnp.exp(b), S)
        b_last = b[:, :, -1:, :]
        S = jnp.exp(b_last[:, :, 0, :, None]) * S + jnp.einsum('bhsk,bhsv->bhkv', kc * jnp.exp(b_last - b), vc)
        return S, o

    S, o = lax.scan(step, s0, (to_chunks(q), to_chunks(k), to_chunks(logf), to_chunks(v)))
    return o.transpose(1, 2, 0, 3, 4).reshape(bsz, nh, n, dv), S


def hgrn_gates(ff, lb):
    lb = lb.astype(jnp.float32).reshape(HG_HEADS, 1, HG_DK)
    f = lb + (1 - lb) * jax.nn.sigmoid(ff)
    k = (1 - lb) * jax.nn.sigmoid(-ff)
    return k, jnp.log(f)


def hgrn_split(p):
    bsz, n, _ = p.shape
    heads = lambda a: a.reshape(bsz, n, HG_HEADS, -1).transpose(0, 2, 1, 3).astype(jnp.float32)
    q = heads(jax.nn.silu(p[..., :HG_W]))
    ff_f = heads(p[..., HG_W:2 * HG_W])
    ff_b = heads(p[..., 2 * HG_W:3 * HG_W])
    iv = heads(p[..., 3 * HG_W:3 * HG_W + HG_HEADS * HG_DV])
    og = p[..., 3 * HG_W + HG_HEADS * HG_DV:]
    return q, ff_f, ff_b, iv, og


def hgrn_bidir(q, ff_f, ff_b, iv, lb_f, lb_b, s0_f, s0_b):
    k_f, g_f = hgrn_gates(ff_f, lb_f)
    o_f, s_f = gla_chunk_scan(q, k_f, g_f, iv, s0_f)
    k_b, g_b = hgrn_gates(ff_b, lb_b)
    flip = lambda a: jnp.flip(a, axis=2)
    o_b, s_b = gla_chunk_scan(flip(q), flip(k_b), flip(g_b), flip(iv), s0_b)
    return o_f + flip(o_b), s_f, s_b


def hgrn_readout(o, og, norm_g):
    bsz, _, n, _ = o.shape
    o = rms_norm(o, norm_g.reshape(HG_HEADS, 1, HG_DV))
    o = o.transpose(0, 2, 1, 3).reshape(bsz, n, HG_HEADS * HG_DV)
    return (o * jax.nn.silu(og.astype(jnp.float32))).astype(og.dtype)


def hgrn_mixer(p_ctx, p_lat, lb_f, lb_b, norm_g, with_ctx_out):
    bsz = p_lat.shape[0]
    zeros = jnp.zeros((bsz, HG_HEADS, HG_DK, HG_DV), jnp.float32)
    qc, ffc_f, ffc_b, ivc, ogc = hgrn_split(p_ctx)
    oc, sc_f, sc_b = hgrn_bidir(qc, ffc_f, ffc_b, ivc, lb_f, lb_b, zeros, zeros)
    q, ff_f, ff_b, iv, og = hgrn_split(p_lat)
    o, _, _ = hgrn_bidir(q, ff_f, ff_b, iv, lb_f, lb_b, sc_f, sc_b)
    o_ctx = hgrn_readout(oc, ogc, norm_g) if with_ctx_out else None
    return hgrn_readout(o, og, norm_g), o_ctx


def seg_decay(a):
    T = a.shape[-1]
    strict = jnp.tril(jnp.ones((T, T), bool), -1)
    lower = jnp.tril(jnp.ones((T, T), bool))
    ae = jnp.where(strict, jnp.broadcast_to(a[..., None], a.shape + (T,)), 0.0)
    cs = jnp.cumsum(ae, axis=-2)
    return jnp.where(lower, jnp.exp(jnp.where(lower, cs, 0.0)), 0.0)


def ssd_chunked(x, a, Bm, Cm, s0):
    bsz, n, nh, hp = x.shape
    nc = n // SSM_CHUNK
    X = x.reshape(bsz, nc, SSM_CHUNK, nh, hp)
    Bc = Bm.reshape(bsz, nc, SSM_CHUNK, nh, -1)
    Cc = Cm.reshape(bsz, nc, SSM_CHUNK, nh, -1)
    A = a.reshape(bsz, nc, SSM_CHUNK, nh).transpose(0, 3, 1, 2)
    A_cs = jnp.cumsum(A, axis=-1)
    L = seg_decay(A)
    scores = jnp.einsum('bclhn,bcshn->bhcls', Cc, Bc) * L
    y_diag = jnp.einsum('bhcls,bcshp->bclhp', scores, X)
    decay_states = jnp.exp(A_cs[..., -1:] - A_cs)
    states = jnp.einsum('bclhn,bhcl,bclhp->bchpn', Bc, decay_states, X)
    states = jnp.concatenate([s0[:, None], states], axis=1)
    decay_chunk = seg_decay(jnp.pad(A_cs[..., -1], ((0, 0), (0, 0), (1, 0))))
    new_states = jnp.einsum('bhzc,bchpn->bzhpn', decay_chunk, states)
    y_off = jnp.einsum('bclhn,bchpn,bhcl->bclhp', Cc, new_states[:, :-1], jnp.exp(A_cs))
    return (y_diag + y_off).reshape(bsz, n, nh, hp), new_states[:, -1]


def depthwise_conv(u, w, b):
    ch = u.shape[-1]
    y = lax.conv_general_dilated(u, w[:, None, :].astype(u.dtype), window_strides=(1,),
                                 padding=[(SSM_CONV // 2, SSM_CONV // 2)],
                                 dimension_numbers=('NWC', 'WIO', 'NWC'), feature_group_count=ch)
    return y + b


def ssm_split(p, conv_w, conv_b):
    bsz, n, _ = p.shape
    z = p[..., :SSM_INNER]
    xbc = jax.nn.silu(depthwise_conv(p[..., SSM_INNER:SSM_INNER + SSM_XBC], conv_w, conv_b)).astype(jnp.float32)
    xs = xbc[..., :SSM_INNER].reshape(bsz, n, SSM_HEADS, SSM_HEADDIM)
    rep = SSM_HEADS // SSM_GROUPS
    grp = lambda a: jnp.repeat(a.reshape(bsz, n, SSM_GROUPS, SSM_STATE), rep, axis=2)
    Bm = grp(xbc[..., SSM_INNER:SSM_INNER + SSM_GROUPS * SSM_STATE])
    Cm = grp(xbc[..., SSM_INNER + SSM_GROUPS * SSM_STATE:])
    dt_raw = p[..., SSM_INNER + SSM_XBC:].astype(jnp.float32).reshape(bsz, n, 2, SSM_HEADS)
    return z, xs, Bm, Cm, dt_raw


def ssm_bidir(xs, Bm, Cm, dt_raw, dt_bias, a_log, s0_f, s0_b):
    dt = jax.nn.softplus(dt_raw + dt_bias.astype(jnp.float32))
    A = -jnp.exp(a_log.astype(jnp.float32))
    y_f, s_f = ssd_chunked(xs * dt[..., 0, :, None], A[0] * dt[..., 0, :], Bm, Cm, s0_f)
    flip = lambda a: jnp.flip(a, axis=1)
    y_b, s_b = ssd_chunked(flip(xs * dt[..., 1, :, None]), flip(A[1] * dt[..., 1, :]), flip(Bm), flip(Cm), s0_b)
    return y_f + flip(y_b), s_f, s_b


def ssm_readout(y, xs, z, d_skip, norm_g):
    bsz, n = z.shape[:2]
    y = (y + d_skip.astype(jnp.float32)[:, None] * xs).reshape(bsz, n, SSM_INNER)
    yz = (y * jax.nn.silu(z.astype(jnp.float32))).reshape(bsz, n, SSM_GROUPS, SSM_INNER // SSM_GROUPS)
    out = rms_norm(yz, norm_g.reshape(SSM_GROUPS, -1))
    return out.reshape(bsz, n, SSM_INNER).astype(z.dtype)


def ssm_mixer(p_ctx, p_lat, conv_w, conv_b, dt_bias, a_log, d_skip, norm_g, with_ctx_out):
    bsz = p_lat.shape[0]
    zeros = jnp.zeros((bsz, SSM_HEADS, SSM_HEADDIM, SSM_STATE), jnp.float32)
    zc, xc, Bc, Cc, dtc = ssm_split(p_ctx, conv_w, conv_b)
    yc, sc_f, sc_b = ssm_bidir(xc, Bc, Cc, dtc, dt_bias, a_log, zeros, zeros)
    z, xs, Bm, Cm, dtr = ssm_split(p_lat, conv_w, conv_b)
    y, _, _ = ssm_bidir(xs, Bm, Cm, dtr, dt_bias, a_log, sc_f, sc_b)
    o_ctx = ssm_readout(yc, xc, zc, d_skip, norm_g) if with_ctx_out else None
    return ssm_readout(y, xs, z, d_skip, norm_g), o_ctx


def setup_inputs(seed: int = 0) -> dict:
    key = jax.random.key(seed)
    ks = jax.random.split(key, 25)
    D = D_MODEL
    nrm = lambda k, shape, scale: jax.random.normal(k, shape, jnp.float32) * scale
    gain = lambda k, shape: 1.0 + 0.02 * jax.random.normal(k, shape, jnp.float32)
    dt0 = jnp.exp(jax.random.uniform(ks[17], (DEPTH, 2, SSM_HEADS), jnp.float32, math.log(1e-3), math.log(1e-1)))
    dt_bias = dt0 + jnp.log(-jnp.expm1(-dt0))
    a_log = jnp.log(jax.random.uniform(ks[18], (DEPTH, 2, SSM_HEADS), jnp.float32, 1.0, 16.0))
    return {
        "x": nrm(ks[0], (BATCH, SEQ, D), 1.0),
        "c": nrm(ks[1], (BATCH, D), 1.0),
        "ctx": nrm(ks[2], (BATCH, CTX_LEN, D), 1.0),
        "c_ctx": nrm(ks[3], (D,), 1.0),
        "w_ada": nrm(ks[4], (DEPTH, D, 6 * D), 0.5 * D ** -0.5),
        "b_ada": nrm(ks[5], (DEPTH, 6 * D), 0.01),
        "norm1_g": gain(ks[6], (DEPTH, D)),
        "norm2_g": gain(ks[7], (DEPTH, D)),
        "w_in": nrm(ks[8], (DEPTH, D, IN_COLS), D ** -0.5),
        "mla_qa_g": gain(ks[9], (DEPTH, MLA_Q_RANK)),
        "mla_wqb": nrm(ks[10], (DEPTH, MLA_Q_RANK, MLA_HEADS * MLA_QK), MLA_Q_RANK ** -0.5),
        "mla_kva_g": gain(ks[11], (DEPTH, MLA_KV_RANK)),
        "mla_wkvb": nrm(ks[12], (DEPTH, MLA_KV_RANK, MLA_HEADS * (MLA_NOPE + MLA_V)), MLA_KV_RANK ** -0.5),
        "hg_lb_logits": nrm(ks[13], (DEPTH, 2, HG_W), 0.1),
        "hg_norm_g": gain(ks[14], (DEPTH, HG_HEADS * HG_DV)),
        "ssm_conv_w": nrm(ks[15], (DEPTH, SSM_CONV, SSM_XBC), SSM_CONV ** -0.5),
        "ssm_conv_b": nrm(ks[16], (DEPTH, SSM_XBC), 0.01),
        "ssm_dt_bias": dt_bias,
        "ssm_a_log": a_log,
        "ssm_d": gain(ks[19], (DEPTH, SSM_HEADS)),
        "ssm_norm_g": gain(ks[20], (DEPTH, SSM_INNER)),
        "w_out": nrm(ks[21], (DEPTH, MIX_WIDTH, D), MIX_WIDTH ** -0.5),
        "w_ffn_in": nrm(ks[22], (DEPTH, D, 2 * FFN_HIDDEN), D ** -0.5),
        "w_ffn_out": nrm(ks[23], (DEPTH, FFN_HIDDEN, D), FFN_HIDDEN ** -0.5),
        "final_g": gain(ks[24], (D,)),
    }


def reference(x, c, ctx, c_ctx, w_ada, b_ada, norm1_g, norm2_g, w_in, mla_qa_g, mla_wqb, mla_kva_g,
              mla_wkvb, hg_lb_logits, hg_norm_g, ssm_conv_w, ssm_conv_b, ssm_dt_bias, ssm_a_log, ssm_d,
              ssm_norm_g, w_out, w_ffn_in, w_ffn_out, final_g):
    n = x.shape[1]
    cos, sin = axial_rope_tables(n)
    lb_soft = jax.nn.softmax(hg_lb_logits.astype(jnp.float32), axis=0)
    lb_all = jnp.cumsum(lb_soft, axis=0) - lb_soft[0]
    silu_c = jax.nn.silu(c)
    silu_cc = jax.nn.silu(c_ctx)
    o1 = MLA_COLS
    o2 = MLA_COLS + HG_COLS
    h_ctx = ctx
    for l in range(DEPTH):
        ctx_out = l < DEPTH - 1
        mod = (silu_c @ w_ada[l] + b_ada[l])[:, None, :]
        modc = silu_cc @ w_ada[l] + b_ada[l]
        sh1, sc1, g1, sh2, sc2, g2 = jnp.split(mod, 6, axis=-1)
        csh1, csc1, cg1, csh2, csc2, cg2 = jnp.split(modc, 6, axis=-1)
        p = modulate(rms_norm(x, norm1_g[l]), sh1, sc1) @ w_in[l]
        pc = modulate(rms_norm(h_ctx, norm1_g[l]), csh1, csc1) @ w_in[l]
        a_lat, a_ctx = mla_mixer(pc[..., :o1], p[..., :o1], mla_qa_g[l], mla_wqb[l], mla_kva_g[l],
                                 mla_wkvb[l], cos, sin, ctx_out)
        r_lat, r_ctx = hgrn_mixer(pc[..., o1:o2], p[..., o1:o2], lb_all[l, 0], lb_all[l, 1], hg_norm_g[l], ctx_out)
        s_lat, s_ctx = ssm_mixer(pc[..., o2:], p[..., o2:], ssm_conv_w[l], ssm_conv_b[l], ssm_dt_bias[l],
                                 ssm_a_log[l], ssm_d[l], ssm_norm_g[l], ctx_out)
        x = x + g1 * (jnp.concatenate([a_lat, r_lat, s_lat], axis=-1) @ w_out[l])
        x = x + g2 * swiglu(modulate(rms_norm(x, norm2_g[l]), sh2, sc2), w_ffn_in[l], w_ffn_out[l])
        if ctx_out:
            h_ctx = h_ctx + cg1 * (jnp.concatenate([a_ctx, r_ctx, s_ctx], axis=-1) @ w_out[l])
            h_ctx = h_ctx + cg2 * swiglu(modulate(rms_norm(h_ctx, norm2_g[l]), csh2, csc2), w_ffn_in[l], w_ffn_out[l])
    return rms_norm(x, final_g)
```

```python
import functools

import numpy as np
import jax
import jax.numpy as jnp
from jax import lax
from jax.experimental import pallas as pl
from jax.experimental.pallas import tpu as pltpu

F32 = jnp.float32
BF16 = jnp.bfloat16
EPS = 1e-6

LANES = 128
SUBLANES = 8
VMEM_BYTES = 64 * 1024 * 1024

GRID_W = 64
MLA_HEADS = 8
MLA_Q_RANK = 384
MLA_KV_RANK = 256
MLA_NOPE = 64
MLA_ROPE = 32
MLA_V = 64
MLA_QK = MLA_NOPE + MLA_ROPE
ROPE_BASE = 10000.0
HG_HEADS = 4
HG_DK = 64
HG_DV = 64
HG_W = HG_HEADS * HG_DK
SSM_HEADS = 4
SSM_HEADDIM = 64
SSM_GROUPS = 2
SSM_STATE = 64
SSM_CONV = 5
SSM_INNER = SSM_HEADS * SSM_HEADDIM
SSM_XBC = SSM_INNER + 2 * SSM_GROUPS * SSM_STATE

MLA_COLS = MLA_Q_RANK + MLA_KV_RANK + MLA_ROPE
HG_COLS = 3 * HG_W + 2 * HG_HEADS * HG_DV
SSM_COLS = SSM_INNER + SSM_XBC + 2 * SSM_HEADS

P_QLAT = 0
P_KVLAT = MLA_Q_RANK
P_KR = P_KVLAT + MLA_KV_RANK
P_HG = P_KR + LANES
P_SSM = P_HG + HG_COLS
SSM_BLOCK = SSM_INNER + 2 * 2 * LANES + 2 * LANES
P_TOTAL = P_SSM + SSM_BLOCK

TOKEN_TILE = 256
HG_CHUNK = 64
SSM_CHUNK = 128

NT_DIMS = (((1,), (1,)), ((), ()))
TN_DIMS = (((0,), (0,)), ((), ()))


def _vmem_limit(nbytes):
    return int(min(VMEM_BYTES - 8 * 1024 * 1024, max(nbytes, 16 * 1024 * 1024)))


def _sigmoid_pair(x):
    e = jnp.exp(-jnp.abs(x))
    d = 1.0 / (1.0 + e)
    ed = e * d
    pos = x >= 0
    return jnp.where(pos, d, ed), jnp.where(pos, ed, d)


def _silu(x):
    return x * _sigmoid_pair(x)[0]


def _rms(x, g):
    ms = jnp.mean(x * x, axis=-1, keepdims=True)
    return x * lax.rsqrt(ms + EPS) * g


def _dot(a, b):
    return jnp.dot(a, b, preferred_element_type=F32)


def _dot_nt(a, b):
    return lax.dot_general(a, b, NT_DIMS, preferred_element_type=F32)


def _dot_tn(a, b):
    return lax.dot_general(a, b, TN_DIMS, preferred_element_type=F32)


def _split3(x):
    x1 = x.astype(BF16)
    r1 = x - x1.astype(F32)
    x2 = r1.astype(BF16)
    x3 = (r1 - x2.astype(F32)).astype(BF16)
    return x1, x2, x3


def _cumsum_mm(tri, x):
    x1, x2, x3 = _split3(x)
    return _dot(tri, x1) + _dot(tri, x2) + _dot(tri, x3)


def _mod_kernel(c_ref, w_ref, b_ref, o_ref):
    s = _silu(c_ref[...]).astype(BF16)
    o_ref[0] = _dot(s, w_ref[0]) + b_ref[0]


def _mod_call(cc, w_ada, b_ada):
    depth, d, six_d = w_ada.shape
    rows = cc.shape[0]
    tn = 1536
    return pl.pallas_call(
        _mod_kernel,
        grid=(depth, six_d // tn),
        in_specs=[
            pl.BlockSpec((rows, d), lambda l, j: (0, 0)),
            pl.BlockSpec((1, d, tn), lambda l, j: (l, 0, j)),
            pl.BlockSpec((1, 1, tn), lambda l, j: (l, 0, j)),
        ],
        out_specs=pl.BlockSpec((1, rows, tn), lambda l, j: (l, 0, j)),
        out_shape=jax.ShapeDtypeStruct((depth, rows, six_d), F32),
        name="adaln_mod",
    )(cc, w_ada, b_ada)


def _rope(xh, c, s, half0):
    sw = jnp.where(half0, pltpu.roll(xh, LANES - 8, 1), pltpu.roll(xh, 8, 1))
    return xh * c + sw * s


def _inproj_kernel(x_ref, mod_ref, g1_ref, win_ref, qag_ref, kvag_ref, wqb_ref, wkvb_ref,
                   cq_ref, sq_ref, ck_ref, sk_ref,
                   q_ref, k_ref, v_ref, hg_ref, ssm_ref):
    d = x_ref.shape[2]
    tm = x_ref.shape[1]
    x = x_ref[0]
    sh = mod_ref[0, 0, :, 0:d]
    sc = mod_ref[0, 0, :, d:2 * d]
    h = _rms(x, g1_ref[...]) * (1.0 + sc) + sh
    p = _dot(h.astype(BF16), win_ref[...])
    hg_ref[0] = p[:, P_HG:P_SSM]
    ssm_ref[0] = p[:, P_SSM:P_TOTAL]
    q_lat = _rms(p[:, P_QLAT:P_KVLAT], qag_ref[...])
    kv_lat = _rms(p[:, P_KVLAT:P_KR], kvag_ref[...])
    q = _dot(q_lat.astype(BF16), wqb_ref[...])
    kv = _dot(kv_lat.astype(BF16), wkvb_ref[...])
    lane = lax.broadcasted_iota(jnp.int32, (tm, LANES), 1)
    half0 = jnp.bitwise_and(lane, 15) < 8
    cq, sq, ck, sk = cq_ref[...], sq_ref[...], ck_ref[...], sk_ref[...]
    kr = _rope(p[:, P_KR:P_HG], ck, sk, half0)
    nope = lane < MLA_NOPE
    for hh in range(MLA_HEADS):
        sl = slice(hh * LANES, (hh + 1) * LANES)
        q_ref[0, hh] = _rope(q[:, sl], cq, sq, half0).astype(q_ref.dtype)
        kvh = kv[:, sl]
        k_ref[0, hh] = jnp.where(nope, kvh, kr).astype(k_ref.dtype)
        v_ref[0, hh] = kvh.astype(v_ref.dtype)


def _inproj_call(x_all, modsel, g1, w_in, qa_g, kva_g, wqb, wkvb, tabs):
    b, n_all, d = x_all.shape
    tm = TOKEN_TILE
    nt = n_all // tm
    full = lambda shape: pl.BlockSpec(shape, lambda bb, i: (0,) * len(shape))
    tab = pl.BlockSpec((tm, LANES), lambda bb, i: (i, 0))
    head_out = pl.BlockSpec((1, MLA_HEADS, tm, LANES), lambda bb, i: (bb, 0, i, 0))
    est = 2 * (tm * d * 4 + w_in.size * 2 + wqb.size * 2 + wkvb.size * 2 + 3 * MLA_HEADS * tm * LANES * 2
               + tm * (HG_COLS + SSM_BLOCK) * 4) + 6 * tm * P_TOTAL * 4
    return pl.pallas_call(
        _inproj_kernel,
        grid=(b, nt),
        in_specs=[
            pl.BlockSpec((1, tm, d), lambda bb, i: (bb, i, 0)),
            pl.BlockSpec((1, 1, 1, modsel.shape[3]), lambda bb, i: (bb, jnp.minimum(i, 1), 0, 0)),
            full((1, d)),
            full(w_in.shape),
            full((1, MLA_Q_RANK)),
            full((1, MLA_KV_RANK)),
            full(wqb.shape),
            full(wkvb.shape),
            tab, tab, tab, tab,
        ],
        out_specs=[
            head_out, head_out, head_out,
            pl.BlockSpec((1, tm, HG_COLS), lambda bb, i: (bb, i, 0)),
            pl.BlockSpec((1, tm, SSM_BLOCK), lambda bb, i: (bb, i, 0)),
        ],
        out_shape=[
            jax.ShapeDtypeStruct((b, MLA_HEADS, n_all, LANES), BF16),
            jax.ShapeDtypeStruct((b, MLA_HEADS, n_all, LANES), BF16),
            jax.ShapeDtypeStruct((b, MLA_HEADS, n_all, LANES), BF16),
            jax.ShapeDtypeStruct((b, n_all, HG_COLS), F32),
            jax.ShapeDtypeStruct((b, n_all, SSM_BLOCK), F32),
        ],
        compiler_params=pltpu.CompilerParams(vmem_limit_bytes=_vmem_limit(est)),
        name="inproj_mla",
    )(x_all, modsel, g1, w_in, qa_g, kva_g, wqb, wkvb, *tabs)


def _attn_kernel(q_ref, k_ref, v_ref, o_ref, *, n_ctx):
    n_all = k_ref.shape[2]
    tq = q_ref.shape[2]

    def run(nk):
        outs = []
        for j in range(2):
            q = q_ref[0, j]
            k = k_ref[0, j, 0:nk, :]
            v = v_ref[0, j, 0:nk, :]
            s = _dot_nt(q, k)
            m = jnp.max(s, axis=-1, keepdims=True)
            p = jnp.exp(s - m)
            l = jnp.sum(p, axis=-1, keepdims=True)
            o = _dot(p.astype(BF16), v)
            outs.append(o * (1.0 / l))
        lane = lax.broadcasted_iota(jnp.int32, (tq, LANES), 1)
        o_ref[0] = jnp.where(lane < MLA_V, pltpu.roll(outs[0], MLA_V, 1), outs[1]).astype(o_ref.dtype)

    is_ctx = pl.program_id(2) == 0

    @pl.when(is_ctx)
    def _():
        run(n_ctx)

    @pl.when(jnp.logical_not(is_ctx))
    def _():
        run(n_all)


def _attn_call(q, k, v, n_ctx):
    b, h, n_all, _ = q.shape
    tq = TOKEN_TILE
    kv_spec = pl.BlockSpec((1, 2, n_all, LANES), lambda bb, hp, i: (bb, hp, 0, 0))
    est = 2 * (2 * tq * LANES * 2 + 2 * 2 * n_all * LANES * 2 + tq * LANES * 2) + 6 * tq * n_all * 4
    return pl.pallas_call(
        functools.partial(_attn_kernel, n_ctx=n_ctx),
        grid=(b, h // 2, n_all // tq),
        in_specs=[pl.BlockSpec((1, 2, tq, LANES), lambda bb, hp, i: (bb, hp, i, 0)), kv_spec, kv_spec],
        out_specs=pl.BlockSpec((1, tq, LANES), lambda bb, hp, i: (bb, i, hp)),
        out_shape=jax.ShapeDtypeStruct((b, n_all, h * MLA_V), BF16),
        compiler_params=pltpu.CompilerParams(vmem_limit_bytes=_vmem_limit(est)),
        name="mla_attention",
    )(q, k, v)


def _hg_consts(c):
    ri = lax.broadcasted_iota(jnp.int32, (c, c), 0)
    ci = lax.broadcasted_iota(jnp.int32, (c, c), 1)
    cst = {
        "tril": jnp.where(ci <= ri, 1.0, 0.0).astype(BF16),
        "triu": jnp.where(ci >= ri, 1.0, 0.0).astype(BF16),
        "lvl": {},
    }
    h = SUBLANES
    while 2 * h <= c:
        same = jnp.right_shift(ri, int(np.log2(2 * h))) == jnp.right_shift(ci, int(np.log2(2 * h)))
        r_lo = jnp.bitwise_and(ri, 2 * h - 1) < h
        c_lo = jnp.bitwise_and(ci, 2 * h - 1) < h
        fwd = jnp.where(same, jnp.where(r_lo, 0.0, jnp.where(c_lo, 1.0, 0.0)), 0.0)
        bwd = jnp.where(same, jnp.where(r_lo, jnp.where(c_lo, 0.0, 1.0), 0.0), 0.0)
        cst["lvl"][(h, False)] = fwd
        cst["lvl"][(h, True)] = bwd
        h *= 2
    r2 = lax.broadcasted_iota(jnp.int32, (LANES, LANES), 0)
    c2 = lax.broadcasted_iota(jnp.int32, (LANES, LANES), 1)
    bd = (r2 < HG_DK) == (c2 < HG_DK)
    cst["bd"] = bd
    cst["bo"] = jnp.where(bd, 1.0, 0.0).astype(BF16)
    lane = lax.broadcasted_iota(jnp.int32, (c, LANES), 1)
    cst["hm"] = [lane < HG_DK, lane >= HG_DK]
    cst["rit"] = jnp.bitwise_and(lax.broadcasted_iota(jnp.int32, (c, LANES), 0), SUBLANES - 1)
    return cst


def _hg_chunk(q, k, f, g, v, st, *, rev, cst):
    c = q.shape[0]
    tri = cst["triu"] if rev else cst["tril"]
    b = _cumsum_mm(tri, g)
    tot = b[0:1, :] if rev else b[c - 1:c, :]
    o = _dot_nt((q * jnp.exp(b)).astype(BF16), st.astype(BF16))
    att = [jnp.zeros((c, c), F32), jnp.zeros((c, c), F32)]
    h = SUBLANES
    while 2 * h <= c:
        pieces = []
        for blk in range(c // (2 * h)):
            row = blk * 2 * h + (h if rev else h - 1)
            pieces.append(jnp.broadcast_to(b[row:row + 1, :], (2 * h, LANES)))
        ref = pieces[0] if len(pieces) == 1 else jnp.concatenate(pieces, axis=0)
        e = jnp.exp(-jnp.abs(b - ref))
        qh = q * e
        kh = (k * e).astype(BF16)
        msk = cst["lvl"][(h, rev)]
        for j in range(2):
            aj = _dot_nt(jnp.where(cst["hm"][j], qh, 0.0).astype(BF16), kh)
            att[j] = att[j] + aj * msk
        h *= 2
    for j in range(2):
        vj = jnp.where(cst["hm"][j], v, 0.0).astype(BF16)
        o = o + _dot(att[j].astype(BF16), vj)

    def shift(x, jj):
        if jj == 0:
            return x
        return pltpu.roll(x, (c - jj) if rev else jj, 0)

    e = None
    for dd in range(SUBLANES):
        if dd == 0:
            w = q * k
        else:
            fd = shift(f, dd - 1)
            e = fd if e is None else e * fd
            valid = (cst["rit"] <= SUBLANES - 1 - dd) if rev else (cst["rit"] >= dd)
            w = jnp.where(valid, q * shift(k, dd) * e, 0.0)
        a = _dot(w.astype(BF16), cst["bo"])
        o = o + a * shift(v, dd)
    ke = (k * jnp.exp(tot - b)).astype(BF16)
    upd = _dot_tn(v.astype(BF16), ke)
    st_new = st * jnp.exp(tot) + jnp.where(cst["bd"], upd, 0.0)
    return o, st_new


def _hgrn_kernel(q_ref, fff_ref, ffb_ref, iv_ref, og_ref, lb_ref, ng_ref, r_ref, of_ref, *, n_ctx):
    c = HG_CHUNK
    n_all = q_ref.shape[1]
    nc = n_all // c
    ncc = n_ctx // c
    cst = _hg_consts(c)
    lbf = lb_ref[0:1, :]
    lbb = lb_ref[1:2, :]
    ng = ng_ref[...]

    def load(ci, ff_ref, lb):
        r0 = pl.multiple_of(ci * c, c)
        rows = pl.ds(r0, c)
        q = _silu(q_ref[0, rows, :])
        sig, sigm = _sigmoid_pair(ff_ref[0, rows, :])
        f = lb + (1.0 - lb) * sig
        k = (1.0 - lb) * sigm
        return rows, q, k, f, jnp.log(f), iv_ref[0, rows, :]

    def fwd_body(ci, st):
        rows, q, k, f, g, v = load(ci, fff_ref, lbf)
        o, st = _hg_chunk(q, k, f, g, v, st, rev=False, cst=cst)
        of_ref[rows, :] = o
        return st

    def bwd_body(ci, st):
        rows, q, k, f, g, v = load(ci, ffb_ref, lbb)
        o, st = _hg_chunk(q, k, f, g, v, st, rev=True, cst=cst)
        o = o + of_ref[rows, :]
        ms = _dot((o * o).astype(BF16), cst["bo"]) * (1.0 / HG_DV)
        r = o * lax.rsqrt(ms + EPS) * ng * _silu(og_ref[0, rows, :])
        r_ref[0, rows, :] = r.astype(r_ref.dtype)
        return st

    zero = jnp.zeros((LANES, LANES), F32)
    lax.fori_loop(0, nc, fwd_body, zero)
    st = lax.fori_loop(0, ncc, lambda i, s: bwd_body(ncc - 1 - i, s), zero)
    lax.fori_loop(0, nc - ncc, lambda i, s: bwd_body(nc - 1 - i, s), st)


def _hgrn_call(p_hg, lb, ng, n_ctx):
    b, n_all, _ = p_hg.shape
    col = lambda j: pl.BlockSpec((1, n_all, LANES), lambda bb, pr: (bb, 0, 2 * j + pr))
    est = 2 * 6 * n_all * LANES * 4 + n_all * LANES * 4 + 8 * 1024 * 1024
    return pl.pallas_call(
        functools.partial(_hgrn_kernel, n_ctx=n_ctx),
        grid=(b, 2),
        in_specs=[col(0), col(1), col(2), col(3), col(4),
                  pl.BlockSpec((2, LANES), lambda bb, pr: (0, pr)),
                  pl.BlockSpec((1, LANES), lambda bb, pr: (0, pr))],
        out_specs=pl.BlockSpec((1, n_all, LANES), lambda bb, pr: (bb, 0, pr)),
        out_shape=jax.ShapeDtypeStruct((b, n_all, HG_HEADS * HG_DV), BF16),
        scratch_shapes=[pltpu.VMEM((n_all, LANES), F32)],
        compiler_params=pltpu.CompilerParams(vmem_limit_bytes=_vmem_limit(est)),
        name="hgrn2_bidir",
    )(p_hg, p_hg, p_hg, p_hg, p_hg, lb, ng)


def _ssd_chunk(xa, dtraw, st, *, rev, cst, dtb, a_neg):
    cs_rows = xa.shape[0]
    lane = cst["lane"]
    lo = lane < SSM_STATE
    xs = xa[:, 0:LANES]
    bc = xa[:, LANES:2 * LANES]
    bm = jnp.where(lo, bc, 0.0).astype(BF16)
    cm = jnp.where(lo, pltpu.roll(bc, SSM_STATE, 1), 0.0).astype(BF16)
    xdt = dtraw + dtb
    dt = jnp.maximum(xdt, 0.0) + jnp.log(1.0 + jnp.exp(-jnp.abs(xdt)))
    a = dt * a_neg
    cs = _cumsum_mm(cst["triu"] if rev else cst["tril"], a)
    cs_t = cs.T
    gmat = _dot_nt(cm, bm)
    base = 2 if rev else 0
    trow = 0 if rev else cs_rows - 1
    y = None
    cols, dts, tots = [], [], []
    for j in range(2):
        li = base + j
        col = cs[:, li:li + 1]
        cols.append(col)
        dts.append(jnp.broadcast_to(dt[:, li:li + 1], (cs_rows, LANES)))
        tots.append(cs[trow:trow + 1, li:li + 1])
        dm = jnp.broadcast_to(col, (cs_rows, cs_rows)) - jnp.broadcast_to(cs_t[li:li + 1, :], (cs_rows, cs_rows))
        lmat = jnp.where(cst["tri_mask_b" if rev else "tri_mask_f"], jnp.exp(jnp.minimum(dm, 0.0)), 0.0)
        xj = jnp.where(lo if j == 0 else jnp.logical_not(lo), xs * dts[j], 0.0).astype(BF16)
        yj = _dot((gmat * lmat).astype(BF16), xj)
        y = yj if y is None else y + yj
    dec_out = jnp.where(lo, jnp.exp(jnp.broadcast_to(cols[0], (cs_rows, LANES))),
                        jnp.exp(jnp.broadcast_to(cols[1], (cs_rows, LANES))))
    y = y + _dot_nt(cm, st.astype(BF16)) * dec_out
    dec_in = jnp.where(lo, jnp.exp(jnp.broadcast_to(tots[0] - cols[0], (cs_rows, LANES))),
                       jnp.exp(jnp.broadcast_to(tots[1] - cols[1], (cs_rows, LANES))))
    xin = (xs * jnp.where(lo, dts[0], dts[1]) * dec_in).astype(BF16)
    upd = _dot_tn(xin, bm)
    rowdec = jnp.where(cst["sub128"] < SSM_HEADDIM,
                       jnp.exp(jnp.broadcast_to(tots[0], (LANES, LANES))),
                       jnp.exp(jnp.broadcast_to(tots[1], (LANES, LANES))))
    return y, st * rowdec + upd


def _ssd_kernel(z_ref, xbc_ref, dt_ref, cw_ref, cb_ref, dtb_ref, alog_ref, dsk_ref, ng_ref,
                y_ref, xa_ref, yf_ref, *, n_ctx):
    c = SSM_CHUNK
    n_all = z_ref.shape[1]
    nc = n_all // c
    ncc = n_ctx // c
    ri = lax.broadcasted_iota(jnp.int32, (c, c), 0)
    ci_ = lax.broadcasted_iota(jnp.int32, (c, c), 1)
    cst = {
        "tril": jnp.where(ci_ <= ri, 1.0, 0.0).astype(BF16),
        "triu": jnp.where(ci_ >= ri, 1.0, 0.0).astype(BF16),
        "tri_mask_f": ci_ <= ri,
        "tri_mask_b": ci_ >= ri,
        "lane": lax.broadcasted_iota(jnp.int32, (c, LANES), 1),
        "sub128": lax.broadcasted_iota(jnp.int32, (LANES, LANES), 0),
    }
    dtb = dtb_ref[...]
    a_neg = -jnp.exp(alog_ref[...])
    cb = cb_ref[...]
    halo = SUBLANES

    def conv_body(ci, carry):
        r0 = pl.multiple_of(ci * c, c)
        cur = xbc_ref[0, pl.ds(r0, c), :]
        first = jnp.logical_or(ci == 0, ci == ncc)
        last = jnp.logical_or(ci == ncc - 1, ci == nc - 1)
        rp = pl.multiple_of(jnp.maximum(r0 - halo, 0), halo)
        rn = pl.multiple_of(jnp.minimum(r0 + c, n_all - halo), halo)
        prev = xbc_ref[0, pl.ds(rp, halo), :] * jnp.where(first, 0.0, 1.0)
        nxt = xbc_ref[0, pl.ds(rn, halo), :] * jnp.where(last, 0.0, 1.0)
        ext = jnp.concatenate([prev, cur, nxt], axis=0)
        acc = jnp.broadcast_to(cb, (c, 2 * LANES))
        for j in range(SSM_CONV):
            s = (SSM_CONV // 2 - j) % (c + 2 * halo)
            sh = ext if s == 0 else pltpu.roll(ext, s, 0)
            acc = acc + cw_ref[j:j + 1, :] * sh[halo:halo + c, :]
        xa_ref[pl.ds(r0, c), :] = _silu(acc)
        return carry

    lax.fori_loop(0, nc, conv_body, 0)

    def fwd_body(ci, st):
        rows = pl.ds(pl.multiple_of(ci * c, c), c)
        y, st = _ssd_chunk(xa_ref[rows, :], dt_ref[0, rows, :], st, rev=False, cst=cst, dtb=dtb, a_neg=a_neg)
        yf_ref[rows, :] = y
        return st

    dsk = dsk_ref[...]
    ng = ng_ref[...]

    def bwd_body(ci, st):
        rows = pl.ds(pl.multiple_of(ci * c, c), c)
        xa = xa_ref[rows, :]
        y, st = _ssd_chunk(xa, dt_ref[0, rows, :], st, rev=True, cst=cst, dtb=dtb, a_neg=a_neg)
        y = y + yf_ref[rows, :] + dsk * xa[:, 0:LANES]
        yz = y * _silu(z_ref[0, rows, :])
        y_ref[0, rows, :] = _rms(yz, ng).astype(y_ref.dtype)
        return st

    zst = jnp.zeros((LANES, LANES), F32)
    lax.fori_loop(0, nc, fwd_body, zst)
    st = lax.fori_loop(0, ncc, lambda i, s: bwd_body(ncc - 1 - i, s), zst)
    lax.fori_loop(0, nc - ncc, lambda i, s: bwd_body(nc - 1 - i, s), st)


def _ssd_call(p_ssm, conv_w, conv_b, dtb, alog, dsk, ng, n_ctx):
    b, n_all, _ = p_ssm.shape
    vec = pl.BlockSpec((1, LANES), lambda bb, g: (0, g))
    est = 2 * (n_all * 4 * LANES * 4 + n_all * LANES * 2) + n_all * 3 * LANES * 4 + 8 * 1024 * 1024
    return pl.pallas_call(
        functools.partial(_ssd_kernel, n_ctx=n_ctx),
        grid=(b, SSM_GROUPS),
        in_specs=[
            pl.BlockSpec((1, n_all, LANES), lambda bb, g: (bb, 0, g)),
            pl.BlockSpec((1, n_all, 2 * LANES), lambda bb, g: (bb, 0, 1 + g)),
            pl.BlockSpec((1, n_all, LANES), lambda bb, g: (bb, 0, 6 + g)),
            pl.BlockSpec((SSM_CONV, 2 * LANES), lambda bb, g: (0, g)),
            pl.BlockSpec((1, 2 * LANES), lambda bb, g: (0, g)),
            vec, vec, vec, vec,
        ],
        out_specs=pl.BlockSpec((1, n_all, LANES), lambda bb, g: (bb, 0, g)),
        out_shape=jax.ShapeDtypeStruct((b, n_all, SSM_INNER), BF16),
        scratch_shapes=[pltpu.VMEM((n_all, 2 * LANES), F32), pltpu.VMEM((n_all, LANES), F32)],
        compiler_params=pltpu.CompilerParams(vmem_limit_bytes=_vmem_limit(est)),
        name="ssd_bidir",
    )(p_ssm, p_ssm, p_ssm, conv_w, conv_b, dtb, alog, dsk, ng)


def _ffn_kernel(x_ref, a_ref, r_ref, s_ref, mod_ref, g2_ref, wa_ref, wr_ref, ws_ref, w1_ref, w2_ref,
                o_ref, *, hid_chunk):
    d = x_ref.shape[2]
    hidden = w2_ref.shape[0]
    x = x_ref[0]
    mod = lambda j: mod_ref[0, 0, :, j * d:(j + 1) * d]
    mix = _dot(a_ref[0], wa_ref[...]) + _dot(r_ref[0], wr_ref[...]) + _dot(s_ref[0], ws_ref[...])
    x1 = x + mod(2) * mix
    h2 = (_rms(x1, g2_ref[...]) * (1.0 + mod(4)) + mod(3)).astype(BF16)
    y = None
    for c0 in range(0, hidden, hid_chunk):
        ha = _dot(h2, w1_ref[:, c0:c0 + hid_chunk])
        hb = _dot(h2, w1_ref[:, hidden + c0:hidden + c0 + hid_chunk])
        act = (_silu(ha) * hb).astype(BF16)
        yc = _dot(act, w2_ref[c0:c0 + hid_chunk, :])
        y = yc if y is None else y + yc
    o_ref[0] = x1 + mod(5) * y


def _ffn_call(x_all, a, r, s, modsel, g2, wa, wr, ws, w1, w2):
    b, n_all, d = x_all.shape
    tm = TOKEN_TILE
    hidden = w2.shape[0]
    hid_chunk = hidden // 2
    full = lambda arr: pl.BlockSpec(arr.shape, lambda bb, i: (0,) * arr.ndim, pipeline_mode=pl.Buffered(1))
    tok = lambda w: pl.BlockSpec((1, tm, w), lambda bb, i: (bb, i, 0))
    est = (wa.size + wr.size + ws.size + w1.size + w2.size) * 2 + 2 * (2 * tm * d * 4 + tm * 1024 * 2) \
        + 6 * tm * hid_chunk * 4 + 6 * tm * d * 4
    return pl.pallas_call(
        functools.partial(_ffn_kernel, hid_chunk=hid_chunk),
        grid=(b, n_all // tm),
        in_specs=[
            tok(d), tok(a.shape[2]), tok(r.shape[2]), tok(s.shape[2]),
            pl.BlockSpec((1, 1, 1, modsel.shape[3]), lambda bb, i: (bb, jnp.minimum(i, 1), 0, 0)),
            pl.BlockSpec((1, d), lambda bb, i: (0, 0)),
            full(wa), full(wr), full(ws), full(w1), full(w2),
        ],
        out_specs=tok(d),
        out_shape=jax.ShapeDtypeStruct((b, n_all, d), F32),
        compiler_params=pltpu.CompilerParams(vmem_limit_bytes=_vmem_limit(est)),
        name="outproj_ffn",
    )(x_all, a, r, s, modsel, g2, wa, wr, ws, w1, w2)


def _final_kernel(x_ref, g_ref, o_ref):
    o_ref[0] = _rms(x_ref[0], g_ref[...])


def _final_call(x_all, g, n_ctx):
    b, n_all, d = x_all.shape
    tm = TOKEN_TILE
    skip = n_ctx // tm
    return pl.pallas_call(
        _final_kernel,
        grid=(b, (n_all - n_ctx) // tm),
        in_specs=[pl.BlockSpec((1, tm, d), lambda bb, i: (bb, i + skip, 0)),
                  pl.BlockSpec((1, d), lambda bb, i: (0, 0))],
        out_specs=pl.BlockSpec((1, tm, d), lambda bb, i: (bb, i, 0)),
        out_shape=jax.ShapeDtypeStruct((b, n_all - n_ctx, d), F32),
        name="final_norm",
    )(x_all, g)


def _win_perm():
    perm = np.full((P_TOTAL,), -1, np.int64)
    perm[P_QLAT:P_QLAT + MLA_Q_RANK + MLA_KV_RANK] = np.arange(MLA_Q_RANK + MLA_KV_RANK)
    perm[P_KR + MLA_NOPE:P_KR + MLA_QK] = MLA_Q_RANK + MLA_KV_RANK + np.arange(MLA_ROPE)
    perm[P_HG:P_HG + HG_COLS] = MLA_COLS + np.arange(HG_COLS)
    o2 = MLA_COLS + HG_COLS
    perm[P_SSM:P_SSM + SSM_INNER] = o2 + np.arange(SSM_INNER)
    xo = o2 + SSM_INNER
    bo = xo + SSM_INNER
    co = bo + SSM_GROUPS * SSM_STATE
    dto = o2 + SSM_INNER + SSM_XBC
    for g in range(SSM_GROUPS):
        base = P_SSM + SSM_INNER + g * 2 * LANES
        perm[base:base + LANES] = xo + g * LANES + np.arange(LANES)
        perm[base + LANES:base + LANES + SSM_STATE] = bo + g * SSM_STATE + np.arange(SSM_STATE)
        perm[base + LANES + SSM_STATE:base + 2 * LANES] = co + g * SSM_STATE + np.arange(SSM_STATE)
        dbase = P_SSM + SSM_INNER + 2 * 2 * LANES + g * LANES
        perm[dbase:dbase + 4] = [dto + 2 * g, dto + 2 * g + 1, dto + SSM_HEADS + 2 * g, dto + SSM_HEADS + 2 * g + 1]
    return perm


def _conv_perm():
    perm = np.zeros((SSM_XBC,), np.int64)
    for g in range(SSM_GROUPS):
        base = g * 2 * LANES
        perm[base:base + LANES] = g * LANES + np.arange(LANES)
        perm[base + LANES:base + LANES + SSM_STATE] = SSM_INNER + g * SSM_STATE + np.arange(SSM_STATE)
        perm[base + LANES + SSM_STATE:base + 2 * LANES] = SSM_INNER + SSM_GROUPS * SSM_STATE + g * SSM_STATE + np.arange(SSM_STATE)
    return perm


def _gather_cols(w, perm):
    idx = jnp.asarray(np.maximum(perm, 0), jnp.int32)
    out = jnp.take(w, idx, axis=-1)
    return jnp.where(jnp.asarray(perm >= 0), out, 0.0)


def _head_vec(v):
    depth = v.shape[0]
    out = jnp.zeros((depth, SSM_GROUPS, LANES), F32)
    for g in range(SSM_GROUPS):
        vals = jnp.stack([v[:, 0, 2 * g], v[:, 0, 2 * g + 1], v[:, 1, 2 * g], v[:, 1, 2 * g + 1]], axis=-1)
        out = out.at[:, g, 0:4].set(vals)
    return out.reshape(depth, 1, SSM_GROUPS * LANES)


def _rope_tables(n_ctx, n_lat, scale):
    rows = n_lat // GRID_W
    row = jnp.repeat(jnp.arange(rows, dtype=F32), GRID_W)
    col = jnp.tile(jnp.arange(GRID_W, dtype=F32), rows)
    n_freq = MLA_ROPE // 4
    inv = ROPE_BASE ** (-jnp.arange(n_freq, dtype=F32) / n_freq)
    ang = jnp.stack([row[:, None] * inv, col[:, None] * inv], axis=1)
    cos, sin = jnp.cos(ang), jnp.sin(ang)
    c_r = jnp.concatenate([cos[:, 0], cos[:, 0], cos[:, 1], cos[:, 1]], axis=-1)
    s_r = jnp.concatenate([-sin[:, 0], sin[:, 0], -sin[:, 1], sin[:, 1]], axis=-1)
    c_lat = jnp.concatenate([jnp.ones((n_lat, MLA_NOPE), F32), c_r, jnp.ones((n_lat, LANES - MLA_QK), F32)], axis=-1)
    s_lat = jnp.concatenate([jnp.zeros((n_lat, MLA_NOPE), F32), s_r, jnp.zeros((n_lat, LANES - MLA_QK), F32)], axis=-1)
    c_all = jnp.concatenate([jnp.ones((n_ctx, LANES), F32), c_lat], axis=0)
    s_all = jnp.concatenate([jnp.zeros((n_ctx, LANES), F32), s_lat], axis=0)
    return c_all * scale, s_all * scale, c_all, s_all


def kernel(x, c, ctx, c_ctx, w_ada, b_ada, norm1_g, norm2_g, w_in, mla_qa_g, mla_wqb, mla_kva_g, mla_wkvb, hg_lb_logits, hg_norm_g, ssm_conv_w, ssm_conv_b, ssm_dt_bias, ssm_a_log, ssm_d, ssm_norm_g, w_out, w_ffn_in, w_ffn_out, final_g):
    bsz, n_lat, d = x.shape
    n_ctx = ctx.shape[1]
    depth = w_ada.shape[0]
    assert n_ctx % TOKEN_TILE == 0 and n_lat % TOKEN_TILE == 0 and n_lat % GRID_W == 0

    rows = -(-(bsz + 1) // SUBLANES) * SUBLANES
    cc = jnp.zeros((rows, d), F32).at[:bsz].set(c).at[bsz].set(c_ctx)
    mods = _mod_call(cc, w_ada.astype(BF16), b_ada.reshape(depth, 1, 6 * d))
    modsel = jnp.stack([jnp.broadcast_to(mods[:, bsz:bsz + 1], (depth, bsz, 6 * d)), mods[:, :bsz]], axis=2)[:, :, :, None, :]

    w_in_p = _gather_cols(w_in, _win_perm()).astype(BF16)
    wqb4 = mla_wqb.reshape(depth, MLA_Q_RANK, MLA_HEADS, MLA_QK)
    wqb_p = jnp.pad(wqb4, ((0, 0), (0, 0), (0, 0), (0, LANES - MLA_QK))).reshape(depth, MLA_Q_RANK, MLA_HEADS * LANES).astype(BF16)
    wkvb_b = mla_wkvb.astype(BF16)
    cperm = _conv_perm()
    conv_w_p = jnp.take(ssm_conv_w, jnp.asarray(cperm, jnp.int32), axis=-1)
    conv_b_p = jnp.take(ssm_conv_b, jnp.asarray(cperm, jnp.int32), axis=-1).reshape(depth, 1, SSM_XBC)
    dtb_p = _head_vec(ssm_dt_bias)
    alog_p = _head_vec(ssm_a_log)
    dsk_p = jnp.repeat(ssm_d, SSM_HEADDIM, axis=-1).reshape(depth, 1, SSM_INNER)
    lb_soft = jax.nn.softmax(hg_lb_logits.astype(F32), axis=0)
    lb_all = jnp.cumsum(lb_soft, axis=0) - lb_soft[0]
    w_out_b = w_out.astype(BF16)
    a_w = MLA_HEADS * MLA_V
    r_w = HG_HEADS * HG_DV
    w1_b = w_ffn_in.astype(BF16)
    w2_b = w_ffn_out.astype(BF16)
    tabs = _rope_tables(n_ctx, n_lat, MLA_QK ** -0.5)

    x_all = jnp.concatenate([ctx, x], axis=1)
    for l in range(depth):
        q, k, v, p_hg, p_ssm = _inproj_call(
            x_all, modsel[l], norm1_g[l].reshape(1, d), w_in_p[l], mla_qa_g[l].reshape(1, -1),
            mla_kva_g[l].reshape(1, -1), wqb_p[l], wkvb_b[l], tabs)
        a = _attn_call(q, k, v, n_ctx)
        r = _hgrn_call(p_hg, lb_all[l], hg_norm_g[l].reshape(1, -1), n_ctx)
        s = _ssd_call(p_ssm, conv_w_p[l], conv_b_p[l], dtb_p[l], alog_p[l], dsk_p[l],
                      ssm_norm_g[l].reshape(1, -1), n_ctx)
        x_all = _ffn_call(x_all, a, r, s, modsel[l], norm2_g[l].reshape(1, d),
                          w_out_b[l, :a_w], w_out_b[l, a_w:a_w + r_w], w_out_b[l, a_w + r_w:], w1_b[l], w2_b[l])
    return _final_call(x_all, final_g.reshape(1, d), n_ctx)
```

```python
import functools

import numpy as np
import jax
import jax.numpy as jnp
from jax import lax
from jax.experimental import pallas as pl
from jax.experimental.pallas import tpu as pltpu

F32 = jnp.float32
BF16 = jnp.bfloat16
EPS = 1e-6

LANES = 128
SUBLANES = 8
VMEM_BYTES = 64 * 1024 * 1024

GRID_W = 64
MLA_HEADS = 8
MLA_Q_RANK = 384
MLA_KV_RANK = 256
MLA_NOPE = 64
MLA_ROPE = 32
MLA_V = 64
MLA_QK = MLA_NOPE + MLA_ROPE
ROPE_BASE = 10000.0
HG_HEADS = 4
HG_DK = 64
HG_DV = 64
HG_W = HG_HEADS * HG_DK
SSM_HEADS = 4
SSM_HEADDIM = 64
SSM_GROUPS = 2
SSM_STATE = 64
SSM_CONV = 5
SSM_INNER = SSM_HEADS * SSM_HEADDIM
SSM_XBC = SSM_INNER + 2 * SSM_GROUPS * SSM_STATE

MLA_COLS = MLA_Q_RANK + MLA_KV_RANK + MLA_ROPE
HG_COLS = 3 * HG_W + 2 * HG_HEADS * HG_DV
SSM_COLS = SSM_INNER + SSM_XBC + 2 * SSM_HEADS

P_QLAT = 0
P_KVLAT = MLA_Q_RANK
P_KR = P_KVLAT + MLA_KV_RANK
P_HG = P_KR + LANES
P_SSM = P_HG + HG_COLS
SSM_BLOCK = SSM_INNER + 2 * 2 * LANES + 2 * LANES
P_TOTAL = P_SSM + SSM_BLOCK

TOKEN_TILE = 256
HG_CHUNK = 64
SSM_CHUNK = 128

NT_DIMS = (((1,), (1,)), ((), ()))
TN_DIMS = (((0,), (0,)), ((), ()))


def _vmem_limit(nbytes):
    return int(min(VMEM_BYTES - 8 * 1024 * 1024, max(nbytes, 16 * 1024 * 1024)))


def _sigmoid_pair(x):
    e = jnp.exp(-jnp.abs(x))
    d = 1.0 / (1.0 + e)
    ed = e * d
    pos = x >= 0
    return jnp.where(pos, d, ed), jnp.where(pos, ed, d)


def _silu(x):
    return x * _sigmoid_pair(x)[0]


def _rms(x, g):
    ms = jnp.mean(x * x, axis=-1, keepdims=True)
    return x * lax.rsqrt(ms + EPS) * g


def _dot(a, b):
    return jnp.dot(a, b, preferred_element_type=F32)


def _dot_nt(a, b):
    return lax.dot_general(a, b, NT_DIMS, preferred_element_type=F32)


def _dot_tn(a, b):
    return lax.dot_general(a, b, TN_DIMS, preferred_element_type=F32)


def _split3(x):
    x1 = x.astype(BF16)
    r1 = x - x1.astype(F32)
    x2 = r1.astype(BF16)
    x3 = (r1 - x2.astype(F32)).astype(BF16)
    return x1, x2, x3


def _cumsum_mm(tri, x):
    x1, x2, x3 = _split3(x)
    return _dot(tri, x1) + _dot(tri, x2) + _dot(tri, x3)


def _mod_kernel(c_ref, w_ref, b_ref, o_ref):
    s = _silu(c_ref[...]).astype(BF16)
    o_ref[0] = _dot(s, w_ref[0]) + b_ref[0]


def _mod_call(cc, w_ada, b_ada):
    depth, d, six_d = w_ada.shape
    rows = cc.shape[0]
    tn = 1536
    return pl.pallas_call(
        _mod_kernel,
        grid=(depth, six_d // tn),
        in_specs=[
            pl.BlockSpec((rows, d), lambda l, j: (0, 0)),
            pl.BlockSpec((1, d, tn), lambda l, j: (l, 0, j)),
            pl.BlockSpec((1, 1, tn), lambda l, j: (l, 0, j)),
        ],
        out_specs=pl.BlockSpec((1, rows, tn), lambda l, j: (l, 0, j)),
        out_shape=jax.ShapeDtypeStruct((depth, rows, six_d), F32),
        name="adaln_mod",
    )(cc, w_ada, b_ada)


def _rope(xh, c, s, half0):
    sw = jnp.where(half0, pltpu.roll(xh, LANES - 8, 1), pltpu.roll(xh, 8, 1))
    return xh * c + sw * s


def _inproj_kernel(x_ref, mod_ref, g1_ref, win_ref, qag_ref, kvag_ref, wqb_ref, wkvb_ref,
                   cq_ref, sq_ref, ck_ref, sk_ref,
                   q_ref, k_ref, v_ref, hg_ref, ssm_ref):
    d = x_ref.shape[2]
    tm = x_ref.shape[1]
    x = x_ref[0]
    sh = mod_ref[0, 0, :, 0:d]
    sc = mod_ref[0, 0, :, d:2 * d]
    h = _rms(x, g1_ref[...]) * (1.0 + sc) + sh
    p = _dot(h.astype(BF16), win_ref[...])
    hg_ref[0] = p[:, P_HG:P_SSM]
    ssm_ref[0] = p[:, P_SSM:P_TOTAL]
    q_lat = _rms(p[:, P_QLAT:P_KVLAT], qag_ref[...])
    kv_lat = _rms(p[:, P_KVLAT:P_KR], kvag_ref[...])
    q = _dot(q_lat.astype(BF16), wqb_ref[...])
    kv = _dot(kv_lat.astype(BF16), wkvb_ref[...])
    lane = lax.broadcasted_iota(jnp.int32, (tm, LANES), 1)
    half0 = jnp.bitwise_and(lane, 15) < 8
    cq, sq, ck, sk = cq_ref[...], sq_ref[...], ck_ref[...], sk_ref[...]
    kr = _rope(p[:, P_KR:P_HG], ck, sk, half0)
    nope = lane < MLA_NOPE
    for hh in range(MLA_HEADS):
        sl = slice(hh * LANES, (hh + 1) * LANES)
        q_ref[0, hh] = _rope(q[:, sl], cq, sq, half0).astype(q_ref.dtype)
        kvh = kv[:, sl]
        k_ref[0, hh] = jnp.where(nope, kvh, kr).astype(k_ref.dtype)
        v_ref[0, hh] = jnp.where(nope, 1.0, kvh).astype(v_ref.dtype)


def _inproj_call(x_all, modsel, g1, w_in, qa_g, kva_g, wqb, wkvb, tabs):
    b, n_all, d = x_all.shape
    tm = TOKEN_TILE
    nt = n_all // tm
    full = lambda shape: pl.BlockSpec(shape, lambda bb, i: (0,) * len(shape))
    tab = pl.BlockSpec((tm, LANES), lambda bb, i: (i, 0))
    head_out = pl.BlockSpec((1, MLA_HEADS, tm, LANES), lambda bb, i: (bb, 0, i, 0))
    est = 2 * (tm * d * 4 + w_in.size * 2 + wqb.size * 2 + wkvb.size * 2 + 3 * MLA_HEADS * tm * LANES * 2
               + tm * (HG_COLS + SSM_BLOCK) * 4) + 6 * tm * P_TOTAL * 4
    return pl.pallas_call(
        _inproj_kernel,
        grid=(b, nt),
        in_specs=[
            pl.BlockSpec((1, tm, d), lambda bb, i: (bb, i, 0)),
            pl.BlockSpec((1, 1, 1, modsel.shape[3]), lambda bb, i: (bb, jnp.minimum(i, 1), 0, 0)),
            full((1, d)),
            full(w_in.shape),
            full((1, MLA_Q_RANK)),
            full((1, MLA_KV_RANK)),
            full(wqb.shape),
            full(wkvb.shape),
            tab, tab, tab, tab,
        ],
        out_specs=[
            head_out, head_out, head_out,
            pl.BlockSpec((1, tm, HG_COLS), lambda bb, i: (bb, i, 0)),
            pl.BlockSpec((1, tm, SSM_BLOCK), lambda bb, i: (bb, i, 0)),
        ],
        out_shape=[
            jax.ShapeDtypeStruct((b, MLA_HEADS, n_all, LANES), BF16),
            jax.ShapeDtypeStruct((b, MLA_HEADS, n_all, LANES), BF16),
            jax.ShapeDtypeStruct((b, MLA_HEADS, n_all, LANES), BF16),
            jax.ShapeDtypeStruct((b, n_all, HG_COLS), F32),
            jax.ShapeDtypeStruct((b, n_all, SSM_BLOCK), F32),
        ],
        compiler_params=pltpu.CompilerParams(vmem_limit_bytes=_vmem_limit(est)),
        name="inproj_mla",
    )(x_all, modsel, g1, w_in, qa_g, kva_g, wqb, wkvb, *tabs)


def _attn_kernel(q_ref, k_ref, v_ref, o_ref, *, n_ctx):
    n_all = k_ref.shape[2]
    tq = q_ref.shape[2]

    def run(nk):
        ss = [_dot_nt(q_ref[0, j], k_ref[0, j, 0:nk, :]) for j in range(2)]
        ps = [jnp.exp2(s - jnp.max(s, axis=-1, keepdims=True)).astype(BF16) for s in ss]
        outs = []
        for j in range(2):
            o = _dot(ps[j], v_ref[0, j, 0:nk, :])
            outs.append(o / pltpu.roll(o, MLA_V, 1))
        lane = lax.broadcasted_iota(jnp.int32, (tq, LANES), 1)
        o_ref[0] = jnp.where(lane < MLA_V, pltpu.roll(outs[0], MLA_V, 1), outs[1]).astype(o_ref.dtype)

    is_ctx = pl.program_id(2) == 0

    @pl.when(is_ctx)
    def _():
        run(n_ctx)

    @pl.when(jnp.logical_not(is_ctx))
    def _():
        run(n_all)


def _attn_call(q, k, v, n_ctx):
    b, h, n_all, _ = q.shape
    tq = TOKEN_TILE
    kv_spec = pl.BlockSpec((1, 2, n_all, LANES), lambda bb, hp, i: (bb, hp, 0, 0))
    est = 2 * (2 * tq * LANES * 2 + 2 * 2 * n_all * LANES * 2 + tq * LANES * 2) + 6 * tq * n_all * 4
    return pl.pallas_call(
        functools.partial(_attn_kernel, n_ctx=n_ctx),
        grid=(b, h // 2, n_all // tq),
        in_specs=[pl.BlockSpec((1, 2, tq, LANES), lambda bb, hp, i: (bb, hp, i, 0)), kv_spec, kv_spec],
        out_specs=pl.BlockSpec((1, tq, LANES), lambda bb, hp, i: (bb, i, hp)),
        out_shape=jax.ShapeDtypeStruct((b, n_all, h * MLA_V), BF16),
        compiler_params=pltpu.CompilerParams(vmem_limit_bytes=_vmem_limit(est)),
        name="mla_attention",
    )(q, k, v)


def _hg_consts(c):
    ri = lax.broadcasted_iota(jnp.int32, (c, c), 0)
    ci = lax.broadcasted_iota(jnp.int32, (c, c), 1)
    cst = {
        "tril": jnp.where(ci <= ri, 1.0, 0.0).astype(BF16),
        "triu": jnp.where(ci >= ri, 1.0, 0.0).astype(BF16),
        "lvl": {},
    }
    h = SUBLANES
    while 2 * h <= c:
        same = jnp.right_shift(ri, int(np.log2(2 * h))) == jnp.right_shift(ci, int(np.log2(2 * h)))
        r_lo = jnp.bitwise_and(ri, 2 * h - 1) < h
        c_lo = jnp.bitwise_and(ci, 2 * h - 1) < h
        fwd = jnp.where(same, jnp.where(r_lo, 0.0, jnp.where(c_lo, 1.0, 0.0)), 0.0)
        bwd = jnp.where(same, jnp.where(r_lo, jnp.where(c_lo, 0.0, 1.0), 0.0), 0.0)
        cst["lvl"][(h, False)] = fwd
        cst["lvl"][(h, True)] = bwd
        h *= 2
    r2 = lax.broadcasted_iota(jnp.int32, (LANES, LANES), 0)
    c2 = lax.broadcasted_iota(jnp.int32, (LANES, LANES), 1)
    bd = (r2 < HG_DK) == (c2 < HG_DK)
    cst["bd"] = bd
    cst["bo"] = jnp.where(bd, 1.0, 0.0).astype(BF16)
    lane = lax.broadcasted_iota(jnp.int32, (c, LANES), 1)
    cst["hm"] = [lane < HG_DK, lane >= HG_DK]
    cst["rit"] = jnp.bitwise_and(lax.broadcasted_iota(jnp.int32, (c, LANES), 0), SUBLANES - 1)
    return cst


def _hg_chunk(q, k, f, g, v, st, *, rev, cst):
    c = q.shape[0]
    tri = cst["triu"] if rev else cst["tril"]
    b = _cumsum_mm(tri, g)
    tot = b[0:1, :] if rev else b[c - 1:c, :]
    o = _dot_nt((q * jnp.exp(b)).astype(BF16), st.astype(BF16))
    att = [jnp.zeros((c, c), F32), jnp.zeros((c, c), F32)]
    h = SUBLANES
    while 2 * h <= c:
        pieces = []
        for blk in range(c // (2 * h)):
            row = blk * 2 * h + (h if rev else h - 1)
            pieces.append(jnp.broadcast_to(b[row:row + 1, :], (2 * h, LANES)))
        ref = pieces[0] if len(pieces) == 1 else jnp.concatenate(pieces, axis=0)
        e = jnp.exp(-jnp.abs(b - ref))
        qh = q * e
        kh = (k * e).astype(BF16)
        msk = cst["lvl"][(h, rev)]
        for j in range(2):
            aj = _dot_nt(jnp.where(cst["hm"][j], qh, 0.0).astype(BF16), kh)
            att[j] = att[j] + aj * msk
        h *= 2
    for j in range(2):
        vj = jnp.where(cst["hm"][j], v, 0.0).astype(BF16)
        o = o + _dot(att[j].astype(BF16), vj)

    def shift(x, jj):
        if jj == 0:
            return x
        x3 = x.reshape(c // SUBLANES, SUBLANES, LANES)
        return pltpu.roll(x3, (SUBLANES - jj) if rev else jj, 1).reshape(c, LANES)

    e = None
    for dd in range(SUBLANES):
        if dd == 0:
            w = q * k
        else:
            fd = shift(f, dd - 1)
            e = fd if e is None else e * fd
            valid = (cst["rit"] <= SUBLANES - 1 - dd) if rev else (cst["rit"] >= dd)
            w = jnp.where(valid, q * shift(k, dd) * e, 0.0)
        a = _dot(w.astype(BF16), cst["bo"])
        o = o + a * shift(v, dd)
    ke = (k * jnp.exp(tot - b)).astype(BF16)
    upd = _dot_tn(v.astype(BF16), ke)
    st_new = st * jnp.exp(tot) + jnp.where(cst["bd"], upd, 0.0)
    return o, st_new


def _hgrn_kernel(q_ref, fff_ref, ffb_ref, iv_ref, og_ref, lb_ref, ng_ref, r_ref, of_ref, ob_ref, *, n_ctx):
    c = HG_CHUNK
    n_all = q_ref.shape[1]
    nc = n_all // c
    ncc = n_ctx // c
    cst = _hg_consts(c)
    lbf = lb_ref[0:1, :]
    lbb = lb_ref[1:2, :]
    ng = ng_ref[...]

    def load(ci, ff_ref, lb):
        r0 = pl.multiple_of(ci * c, c)
        rows = pl.ds(r0, c)
        q = _silu(q_ref[0, rows, :])
        sig, sigm = _sigmoid_pair(ff_ref[0, rows, :])
        f = lb + (1.0 - lb) * sig
        k = (1.0 - lb) * sigm
        return rows, q, k, f, jnp.log(f), iv_ref[0, rows, :]

    def body(i, carry):
        st_f, st_b = carry
        rows, q, k, f, g, v = load(i, fff_ref, lbf)
        o, st_f = _hg_chunk(q, k, f, g, v, st_f, rev=False, cst=cst)
        of_ref[rows, :] = o
        ib = jnp.where(i < ncc, ncc - 1 - i, nc + ncc - 1 - i)
        rows, q, k, f, g, v = load(ib, ffb_ref, lbb)
        o, st_b = _hg_chunk(q, k, f, g, v, st_b, rev=True, cst=cst)
        ob_ref[rows, :] = o
        return st_f, st_b

    zero = jnp.zeros((LANES, LANES), F32)
    lax.fori_loop(0, nc, body, (zero, zero), unroll=2)

    tr = TOKEN_TILE

    def readout(i, carry):
        rows = pl.ds(pl.multiple_of(i * tr, tr), tr)
        o = of_ref[rows, :] + ob_ref[rows, :]
        ms = _dot((o * o).astype(BF16), cst["bo"]) * (1.0 / HG_DV)
        r = o * lax.rsqrt(ms + EPS) * ng * _silu(og_ref[0, rows, :])
        r_ref[0, rows, :] = r.astype(r_ref.dtype)
        return carry

    lax.fori_loop(0, n_all // tr, readout, 0)


def _hgrn_call(p_hg, lb, ng, n_ctx):
    b, n_all, _ = p_hg.shape
    col = lambda j: pl.BlockSpec((1, n_all, LANES), lambda bb, pr: (bb, 0, 2 * j + pr))
    est = 2 * 6 * n_all * LANES * 4 + 2 * n_all * LANES * 4 + 8 * 1024 * 1024
    return pl.pallas_call(
        functools.partial(_hgrn_kernel, n_ctx=n_ctx),
        grid=(b, 2),
        in_specs=[col(0), col(1), col(2), col(3), col(4),
                  pl.BlockSpec((2, LANES), lambda bb, pr: (0, pr)),
                  pl.BlockSpec((1, LANES), lambda bb, pr: (0, pr))],
        out_specs=pl.BlockSpec((1, n_all, LANES), lambda bb, pr: (bb, 0, pr)),
        out_shape=jax.ShapeDtypeStruct((b, n_all, HG_HEADS * HG_DV), BF16),
        scratch_shapes=[pltpu.VMEM((n_all, LANES), F32), pltpu.VMEM((n_all, LANES), F32)],
        compiler_params=pltpu.CompilerParams(vmem_limit_bytes=_vmem_limit(est)),
        name="hgrn2_bidir",
    )(p_hg, p_hg, p_hg, p_hg, p_hg, lb, ng)


def _ssd_chunk(xa, dtraw, st, *, rev, cst, dtb, a_neg):
    cs_rows = xa.shape[0]
    lane = cst["lane"]
    lo = lane < SSM_STATE
    xs = xa[:, 0:LANES]
    bc = xa[:, LANES:2 * LANES]
    bm = jnp.where(lo, bc, 0.0).astype(BF16)
    cm = jnp.where(lo, pltpu.roll(bc, SSM_STATE, 1), 0.0).astype(BF16)
    xdt = dtraw + dtb
    dt = jnp.maximum(xdt, 0.0) + jnp.log(1.0 + jnp.exp(-jnp.abs(xdt)))
    a = dt * a_neg
    cs = _cumsum_mm(cst["triu"] if rev else cst["tril"], a)
    cs_t = cs.T
    gmat = _dot_nt(cm, bm)
    base = 2 if rev else 0
    trow = 0 if rev else cs_rows - 1
    y = None
    cols, dts, tots = [], [], []
    for j in range(2):
        li = base + j
        col = cs[:, li:li + 1]
        cols.append(col)
        dts.append(jnp.broadcast_to(dt[:, li:li + 1], (cs_rows, LANES)))
        tots.append(cs[trow:trow + 1, li:li + 1])
        dm = jnp.broadcast_to(col, (cs_rows, cs_rows)) - jnp.broadcast_to(cs_t[li:li + 1, :], (cs_rows, cs_rows))
        lmat = jnp.where(cst["tri_mask_b" if rev else "tri_mask_f"], jnp.exp(jnp.minimum(dm, 0.0)), 0.0)
        xj = jnp.where(lo if j == 0 else jnp.logical_not(lo), xs * dts[j], 0.0).astype(BF16)
        yj = _dot((gmat * lmat).astype(BF16), xj)
        y = yj if y is None else y + yj
    dec_out = jnp.where(lo, jnp.exp(jnp.broadcast_to(cols[0], (cs_rows, LANES))),
                        jnp.exp(jnp.broadcast_to(cols[1], (cs_rows, LANES))))
    y = y + _dot_nt(cm, st.astype(BF16)) * dec_out
    dec_in = jnp.where(lo, jnp.exp(jnp.broadcast_to(tots[0] - cols[0], (cs_rows, LANES))),
                       jnp.exp(jnp.broadcast_to(tots[1] - cols[1], (cs_rows, LANES))))
    xin = (xs * jnp.where(lo, dts[0], dts[1]) * dec_in).astype(BF16)
    upd = _dot_tn(xin, bm)
    rowdec = jnp.where(cst["sub128"] < SSM_HEADDIM,
                       jnp.exp(jnp.broadcast_to(tots[0], (LANES, LANES))),
                       jnp.exp(jnp.broadcast_to(tots[1], (LANES, LANES))))
    return y, st * rowdec + upd


def _ssd_kernel(z_ref, xbc_ref, dt_ref, cw_ref, cb_ref, dtb_ref, alog_ref, dsk_ref, ng_ref,
                y_ref, xa_ref, yf_ref, yb_ref, *, n_ctx):
    c = SSM_CHUNK
    n_all = z_ref.shape[1]
    nc = n_all // c
    ncc = n_ctx // c
    ri = lax.broadcasted_iota(jnp.int32, (c, c), 0)
    ci_ = lax.broadcasted_iota(jnp.int32, (c, c), 1)
    cst = {
        "tril": jnp.where(ci_ <= ri, 1.0, 0.0).astype(BF16),
        "triu": jnp.where(ci_ >= ri, 1.0, 0.0).astype(BF16),
        "tri_mask_f": ci_ <= ri,
        "tri_mask_b": ci_ >= ri,
        "lane": lax.broadcasted_iota(jnp.int32, (c, LANES), 1),
        "sub128": lax.broadcasted_iota(jnp.int32, (LANES, LANES), 0),
    }
    dtb = dtb_ref[...]
    a_neg = -jnp.exp(alog_ref[...])
    cb = cb_ref[...]
    halo = SUBLANES

    def conv_body(ci, carry):
        r0 = pl.multiple_of(ci * c, c)
        cur = xbc_ref[0, pl.ds(r0, c), :]
        first = jnp.logical_or(ci == 0, ci == ncc)
        last = jnp.logical_or(ci == ncc - 1, ci == nc - 1)
        rp = pl.multiple_of(jnp.maximum(r0 - halo, 0), halo)
        rn = pl.multiple_of(jnp.minimum(r0 + c, n_all - halo), halo)
        prev = xbc_ref[0, pl.ds(rp, halo), :] * jnp.where(first, 0.0, 1.0)
        nxt = xbc_ref[0, pl.ds(rn, halo), :] * jnp.where(last, 0.0, 1.0)
        ext = jnp.concatenate([prev, cur, nxt], axis=0)
        acc = jnp.broadcast_to(cb, (c, 2 * LANES))
        for j in range(SSM_CONV):
            s = (SSM_CONV // 2 - j) % (c + 2 * halo)
            sh = ext if s == 0 else pltpu.roll(ext, s, 0)
            acc = acc + cw_ref[j:j + 1, :] * sh[halo:halo + c, :]
        xa_ref[pl.ds(r0, c), :] = _silu(acc)
        return carry

    lax.fori_loop(0, nc, conv_body, 0)

    def body(i, carry):
        st_f, st_b = carry
        rows = pl.ds(pl.multiple_of(i * c, c), c)
        y, st_f = _ssd_chunk(xa_ref[rows, :], dt_ref[0, rows, :], st_f, rev=False, cst=cst, dtb=dtb, a_neg=a_neg)
        yf_ref[rows, :] = y
        ib = jnp.where(i < ncc, ncc - 1 - i, nc + ncc - 1 - i)
        rows = pl.ds(pl.multiple_of(ib * c, c), c)
        y, st_b = _ssd_chunk(xa_ref[rows, :], dt_ref[0, rows, :], st_b, rev=True, cst=cst, dtb=dtb, a_neg=a_neg)
        yb_ref[rows, :] = y
        return st_f, st_b

    zst = jnp.zeros((LANES, LANES), F32)
    lax.fori_loop(0, nc, body, (zst, zst), unroll=2)

    dsk = dsk_ref[...]
    ng = ng_ref[...]
    tr = TOKEN_TILE

    def readout(i, carry):
        rows = pl.ds(pl.multiple_of(i * tr, tr), tr)
        y = yf_ref[rows, :] + yb_ref[rows, :] + dsk * xa_ref[rows, 0:LANES]
        yz = y * _silu(z_ref[0, rows, :])
        y_ref[0, rows, :] = _rms(yz, ng).astype(y_ref.dtype)
        return carry

    lax.fori_loop(0, n_all // tr, readout, 0)


def _ssd_call(p_ssm, conv_w, conv_b, dtb, alog, dsk, ng, n_ctx):
    b, n_all, _ = p_ssm.shape
    vec = pl.BlockSpec((1, LANES), lambda bb, g: (0, g))
    est = 2 * (n_all * 4 * LANES * 4 + n_all * LANES * 2) + n_all * 4 * LANES * 4 + 8 * 1024 * 1024
    return pl.pallas_call(
        functools.partial(_ssd_kernel, n_ctx=n_ctx),
        grid=(b, SSM_GROUPS),
        in_specs=[
            pl.BlockSpec((1, n_all, LANES), lambda bb, g: (bb, 0, g)),
            pl.BlockSpec((1, n_all, 2 * LANES), lambda bb, g: (bb, 0, 1 + g)),
            pl.BlockSpec((1, n_all, LANES), lambda bb, g: (bb, 0, 6 + g)),
            pl.BlockSpec((SSM_CONV, 2 * LANES), lambda bb, g: (0, g)),
            pl.BlockSpec((1, 2 * LANES), lambda bb, g: (0, g)),
            vec, vec, vec, vec,
        ],
        out_specs=pl.BlockSpec((1, n_all, LANES), lambda bb, g: (bb, 0, g)),
        out_shape=jax.ShapeDtypeStruct((b, n_all, SSM_INNER), BF16),
        scratch_shapes=[pltpu.VMEM((n_all, 2 * LANES), F32), pltpu.VMEM((n_all, LANES), F32),
                        pltpu.VMEM((n_all, LANES), F32)],
        compiler_params=pltpu.CompilerParams(vmem_limit_bytes=_vmem_limit(est)),
        name="ssd_bidir",
    )(p_ssm, p_ssm, p_ssm, conv_w, conv_b, dtb, alog, dsk, ng)


def _ffn_kernel(x_ref, a_ref, r_ref, s_ref, mod_ref, g2_ref, wa_ref, wr_ref, ws_ref, w1_ref, w2_ref,
                o_ref, *, hid_chunk):
    d = x_ref.shape[2]
    hidden = w2_ref.shape[0]
    x = x_ref[0]
    mod = lambda j: mod_ref[0, 0, :, j * d:(j + 1) * d]
    mix = _dot(a_ref[0], wa_ref[...]) + _dot(r_ref[0], wr_ref[...]) + _dot(s_ref[0], ws_ref[...])
    x1 = x + mod(2) * mix
    h2 = (_rms(x1, g2_ref[...]) * (1.0 + mod(4)) + mod(3)).astype(BF16)
    y = None
    for c0 in range(0, hidden, hid_chunk):
        ha = _dot(h2, w1_ref[:, c0:c0 + hid_chunk])
        hb = _dot(h2, w1_ref[:, hidden + c0:hidden + c0 + hid_chunk])
        act = (_silu(ha) * hb).astype(BF16)
        yc = _dot(act, w2_ref[c0:c0 + hid_chunk, :])
        y = yc if y is None else y + yc
    o_ref[0] = x1 + mod(5) * y


def _ffn_call(x_all, a, r, s, modsel, g2, wa, wr, ws, w1, w2):
    b, n_all, d = x_all.shape
    tm = TOKEN_TILE
    hidden = w2.shape[0]
    hid_chunk = hidden // 2
    full = lambda arr: pl.BlockSpec(arr.shape, lambda bb, i: (0,) * arr.ndim, pipeline_mode=pl.Buffered(1))
    tok = lambda w: pl.BlockSpec((1, tm, w), lambda bb, i: (bb, i, 0))
    est = (wa.size + wr.size + ws.size + w1.size + w2.size) * 2 + 2 * (2 * tm * d * 4 + tm * 1024 * 2) \
        + 6 * tm * hid_chunk * 4 + 6 * tm * d * 4
    return pl.pallas_call(
        functools.partial(_ffn_kernel, hid_chunk=hid_chunk),
        grid=(b, n_all // tm),
        in_specs=[
            tok(d), tok(a.shape[2]), tok(r.shape[2]), tok(s.shape[2]),
            pl.BlockSpec((1, 1, 1, modsel.shape[3]), lambda bb, i: (bb, jnp.minimum(i, 1), 0, 0)),
            pl.BlockSpec((1, d), lambda bb, i: (0, 0)),
            full(wa), full(wr), full(ws), full(w1), full(w2),
        ],
        out_specs=tok(d),
        out_shape=jax.ShapeDtypeStruct((b, n_all, d), F32),
        compiler_params=pltpu.CompilerParams(vmem_limit_bytes=_vmem_limit(est)),
        name="outproj_ffn",
    )(x_all, a, r, s, modsel, g2, wa, wr, ws, w1, w2)


def _final_kernel(x_ref, g_ref, o_ref):
    o_ref[0] = _rms(x_ref[0], g_ref[...])


def _final_call(x_all, g, n_ctx):
    b, n_all, d = x_all.shape
    tm = TOKEN_TILE
    skip = n_ctx // tm
    return pl.pallas_call(
        _final_kernel,
        grid=(b, (n_all - n_ctx) // tm),
        in_specs=[pl.BlockSpec((1, tm, d), lambda bb, i: (bb, i + skip, 0)),
                  pl.BlockSpec((1, d), lambda bb, i: (0, 0))],
        out_specs=pl.BlockSpec((1, tm, d), lambda bb, i: (bb, i, 0)),
        out_shape=jax.ShapeDtypeStruct((b, n_all - n_ctx, d), F32),
        name="final_norm",
    )(x_all, g)


def _win_perm():
    perm = np.full((P_TOTAL,), -1, np.int64)
    perm[P_QLAT:P_QLAT + MLA_Q_RANK + MLA_KV_RANK] = np.arange(MLA_Q_RANK + MLA_KV_RANK)
    perm[P_KR + MLA_NOPE:P_KR + MLA_QK] = MLA_Q_RANK + MLA_KV_RANK + np.arange(MLA_ROPE)
    perm[P_HG:P_HG + HG_COLS] = MLA_COLS + np.arange(HG_COLS)
    o2 = MLA_COLS + HG_COLS
    perm[P_SSM:P_SSM + SSM_INNER] = o2 + np.arange(SSM_INNER)
    xo = o2 + SSM_INNER
    bo = xo + SSM_INNER
    co = bo + SSM_GROUPS * SSM_STATE
    dto = o2 + SSM_INNER + SSM_XBC
    for g in range(SSM_GROUPS):
        base = P_SSM + SSM_INNER + g * 2 * LANES
        perm[base:base + LANES] = xo + g * LANES + np.arange(LANES)
        perm[base + LANES:base + LANES + SSM_STATE] = bo + g * SSM_STATE + np.arange(SSM_STATE)
        perm[base + LANES + SSM_STATE:base + 2 * LANES] = co + g * SSM_STATE + np.arange(SSM_STATE)
        dbase = P_SSM + SSM_INNER + 2 * 2 * LANES + g * LANES
        perm[dbase:dbase + 4] = [dto + 2 * g, dto + 2 * g + 1, dto + SSM_HEADS + 2 * g, dto + SSM_HEADS + 2 * g + 1]
    return perm


def _conv_perm():
    perm = np.zeros((SSM_XBC,), np.int64)
    for g in range(SSM_GROUPS):
        base = g * 2 * LANES
        perm[base:base + LANES] = g * LANES + np.arange(LANES)
        perm[base + LANES:base + LANES + SSM_STATE] = SSM_INNER + g * SSM_STATE + np.arange(SSM_STATE)
        perm[base + LANES + SSM_STATE:base + 2 * LANES] = SSM_INNER + SSM_GROUPS * SSM_STATE + g * SSM_STATE + np.arange(SSM_STATE)
    return perm


def _gather_cols(w, perm):
    idx = jnp.asarray(np.maximum(perm, 0), jnp.int32)
    out = jnp.take(w, idx, axis=-1)
    return jnp.where(jnp.asarray(perm >= 0), out, 0.0)


def _head_vec(v):
    depth = v.shape[0]
    out = jnp.zeros((depth, SSM_GROUPS, LANES), F32)
    for g in range(SSM_GROUPS):
        vals = jnp.stack([v[:, 0, 2 * g], v[:, 0, 2 * g + 1], v[:, 1, 2 * g], v[:, 1, 2 * g + 1]], axis=-1)
        out = out.at[:, g, 0:4].set(vals)
    return out.reshape(depth, 1, SSM_GROUPS * LANES)


def _rope_tables(n_ctx, n_lat, scale):
    rows = n_lat // GRID_W
    row = jnp.repeat(jnp.arange(rows, dtype=F32), GRID_W)
    col = jnp.tile(jnp.arange(GRID_W, dtype=F32), rows)
    n_freq = MLA_ROPE // 4
    inv = ROPE_BASE ** (-jnp.arange(n_freq, dtype=F32) / n_freq)
    ang = jnp.stack([row[:, None] * inv, col[:, None] * inv], axis=1)
    cos, sin = jnp.cos(ang), jnp.sin(ang)
    c_r = jnp.concatenate([cos[:, 0], cos[:, 0], cos[:, 1], cos[:, 1]], axis=-1)
    s_r = jnp.concatenate([-sin[:, 0], sin[:, 0], -sin[:, 1], sin[:, 1]], axis=-1)
    c_lat = jnp.concatenate([jnp.ones((n_lat, MLA_NOPE), F32), c_r, jnp.ones((n_lat, LANES - MLA_QK), F32)], axis=-1)
    s_lat = jnp.concatenate([jnp.zeros((n_lat, MLA_NOPE), F32), s_r, jnp.zeros((n_lat, LANES - MLA_QK), F32)], axis=-1)
    c_all = jnp.concatenate([jnp.ones((n_ctx, LANES), F32), c_lat], axis=0)
    s_all = jnp.concatenate([jnp.zeros((n_ctx, LANES), F32), s_lat], axis=0)
    return c_all * scale, s_all * scale, c_all, s_all


def kernel(x, c, ctx, c_ctx, w_ada, b_ada, norm1_g, norm2_g, w_in, mla_qa_g, mla_wqb, mla_kva_g, mla_wkvb, hg_lb_logits, hg_norm_g, ssm_conv_w, ssm_conv_b, ssm_dt_bias, ssm_a_log, ssm_d, ssm_norm_g, w_out, w_ffn_in, w_ffn_out, final_g):
    bsz, n_lat, d = x.shape
    n_ctx = ctx.shape[1]
    depth = w_ada.shape[0]
    assert n_ctx % TOKEN_TILE == 0 and n_lat % TOKEN_TILE == 0 and n_lat % GRID_W == 0

    rows = -(-(bsz + 1) // SUBLANES) * SUBLANES
    cc = jnp.zeros((rows, d), F32).at[:bsz].set(c).at[bsz].set(c_ctx)
    mods = _mod_call(cc, w_ada.astype(BF16), b_ada.reshape(depth, 1, 6 * d))
    modsel = jnp.stack([jnp.broadcast_to(mods[:, bsz:bsz + 1], (depth, bsz, 6 * d)), mods[:, :bsz]], axis=2)[:, :, :, None, :]

    w_in_p = _gather_cols(w_in, _win_perm()).astype(BF16)
    wqb4 = mla_wqb.reshape(depth, MLA_Q_RANK, MLA_HEADS, MLA_QK)
    wqb_p = jnp.pad(wqb4, ((0, 0), (0, 0), (0, 0), (0, LANES - MLA_QK))).reshape(depth, MLA_Q_RANK, MLA_HEADS * LANES).astype(BF16)
    wkvb_b = mla_wkvb.astype(BF16)
    cperm = _conv_perm()
    conv_w_p = jnp.take(ssm_conv_w, jnp.asarray(cperm, jnp.int32), axis=-1)
    conv_b_p = jnp.take(ssm_conv_b, jnp.asarray(cperm, jnp.int32), axis=-1).reshape(depth, 1, SSM_XBC)
    dtb_p = _head_vec(ssm_dt_bias)
    alog_p = _head_vec(ssm_a_log)
    dsk_p = jnp.repeat(ssm_d, SSM_HEADDIM, axis=-1).reshape(depth, 1, SSM_INNER)
    lb_soft = jax.nn.softmax(hg_lb_logits.astype(F32), axis=0)
    lb_all = jnp.cumsum(lb_soft, axis=0) - lb_soft[0]
    w_out_b = w_out.astype(BF16)
    a_w = MLA_HEADS * MLA_V
    r_w = HG_HEADS * HG_DV
    w1_b = w_ffn_in.astype(BF16)
    w2_b = w_ffn_out.astype(BF16)
    tabs = _rope_tables(n_ctx, n_lat, MLA_QK ** -0.5 * float(np.log2(np.e)))

    x_all = jnp.concatenate([ctx, x], axis=1)
    for l in range(depth):
        q, k, v, p_hg, p_ssm = _inproj_call(
            x_all, modsel[l], norm1_g[l].reshape(1, d), w_in_p[l], mla_qa_g[l].reshape(1, -1),
            mla_kva_g[l].reshape(1, -1), wqb_p[l], wkvb_b[l], tabs)
        a = _attn_call(q, k, v, n_ctx)
        r = _hgrn_call(p_hg, lb_all[l], hg_norm_g[l].reshape(1, -1), n_ctx)
        s = _ssd_call(p_ssm, conv_w_p[l], conv_b_p[l], dtb_p[l], alog_p[l], dsk_p[l],
                      ssm_norm_g[l].reshape(1, -1), n_ctx)
        x_all = _ffn_call(x_all, a, r, s, modsel[l], norm2_g[l].reshape(1, d),
                          w_out_b[l, :a_w], w_out_b[l, a_w:a_w + r_w], w_out_b[l, a_w + r_w:], w1_b[l], w2_b[l])
    return _final_call(x_all, final_g.reshape(1, d), n_ctx)
```

```python
import functools

import numpy as np
import jax
import jax.numpy as jnp
from jax import lax
from jax.experimental import pallas as pl
from jax.experimental.pallas import tpu as pltpu

F32 = jnp.float32
BF16 = jnp.bfloat16
EPS = 1e-6

LANES = 128
SUBLANES = 8
VMEM_BYTES = 64 * 1024 * 1024

GRID_W = 64
MLA_HEADS = 8
MLA_Q_RANK = 384
MLA_KV_RANK = 256
MLA_NOPE = 64
MLA_ROPE = 32
MLA_V = 64
MLA_QK = MLA_NOPE + MLA_ROPE
ROPE_BASE = 10000.0
HG_HEADS = 4
HG_DK = 64
HG_DV = 64
HG_W = HG_HEADS * HG_DK
SSM_HEADS = 4
SSM_HEADDIM = 64
SSM_GROUPS = 2
SSM_STATE = 64
SSM_CONV = 5
SSM_INNER = SSM_HEADS * SSM_HEADDIM
SSM_XBC = SSM_INNER + 2 * SSM_GROUPS * SSM_STATE

MLA_COLS = MLA_Q_RANK + MLA_KV_RANK + MLA_ROPE
HG_COLS = 3 * HG_W + 2 * HG_HEADS * HG_DV
SSM_COLS = SSM_INNER + SSM_XBC + 2 * SSM_HEADS

P_QLAT = 0
P_KVLAT = MLA_Q_RANK
P_KR = P_KVLAT + MLA_KV_RANK
P_HG = P_KR + LANES
P_SSM = P_HG + HG_COLS
SSM_BLOCK = SSM_INNER + 2 * 2 * LANES + 2 * 2 * LANES
P_TOTAL = P_SSM + SSM_BLOCK

TOKEN_TILE = 256
HG_CHUNK = 128
HG_BATCH = 2
SSM_CHUNK = 128
SSM_BATCH = 2

NT_DIMS = (((1,), (1,)), ((), ()))
TN_DIMS = (((0,), (0,)), ((), ()))


def _vmem_limit(nbytes):
    return int(min(VMEM_BYTES - 8 * 1024 * 1024, max(nbytes, 16 * 1024 * 1024)))


def _sigmoid_pair(x):
    e = jnp.exp(-jnp.abs(x))
    d = 1.0 / (1.0 + e)
    ed = e * d
    pos = x >= 0
    return jnp.where(pos, d, ed), jnp.where(pos, ed, d)


def _silu(x):
    return x * _sigmoid_pair(x)[0]


def _rms(x, g):
    ms = jnp.mean(x * x, axis=-1, keepdims=True)
    return x * lax.rsqrt(ms + EPS) * g


def _dot(a, b):
    return jnp.dot(a, b, preferred_element_type=F32)


def _dot_nt(a, b):
    return lax.dot_general(a, b, NT_DIMS, preferred_element_type=F32)


def _dot_tn(a, b):
    return lax.dot_general(a, b, TN_DIMS, preferred_element_type=F32)


def _split3(x):
    x1 = x.astype(BF16)
    r1 = x - x1.astype(F32)
    x2 = r1.astype(BF16)
    x3 = (r1 - x2.astype(F32)).astype(BF16)
    return x1, x2, x3


def _cumsum_mm(tri, x):
    x1, x2, x3 = _split3(x)
    return _dot(tri, x1) + _dot(tri, x2) + _dot(tri, x3)


def _mod_kernel(c_ref, w_ref, b_ref, o_ref):
    s = _silu(c_ref[...]).astype(BF16)
    o_ref[0] = _dot(s, w_ref[0]) + b_ref[0]


def _mod_call(cc, w_ada, b_ada):
    depth, d, six_d = w_ada.shape
    rows = cc.shape[0]
    tn = 1536
    return pl.pallas_call(
        _mod_kernel,
        grid=(depth, six_d // tn),
        in_specs=[
            pl.BlockSpec((rows, d), lambda l, j: (0, 0)),
            pl.BlockSpec((1, d, tn), lambda l, j: (l, 0, j)),
            pl.BlockSpec((1, 1, tn), lambda l, j: (l, 0, j)),
        ],
        out_specs=pl.BlockSpec((1, rows, tn), lambda l, j: (l, 0, j)),
        out_shape=jax.ShapeDtypeStruct((depth, rows, six_d), F32),
        name="adaln_mod",
    )(cc, w_ada, b_ada)


def _rope(xh, c, s, half0):
    sw = jnp.where(half0, pltpu.roll(xh, LANES - 8, 1), pltpu.roll(xh, 8, 1))
    return xh * c + sw * s


def _inproj_kernel(x_ref, mod_ref, g1_ref, win_ref, qag_ref, kvag_ref, wqb_ref, wkvb_ref,
                   cq_ref, sq_ref, ck_ref, sk_ref,
                   q_ref, k_ref, v_ref, hg_ref, ssm_ref):
    d = x_ref.shape[2]
    tm = x_ref.shape[1]
    x = x_ref[0]
    sh = mod_ref[0, 0, :, 0:d]
    sc = mod_ref[0, 0, :, d:2 * d]
    h = _rms(x, g1_ref[...]) * (1.0 + sc) + sh
    p = _dot(h.astype(BF16), win_ref[...])
    hg_ref[0] = p[:, P_HG:P_SSM]
    ssm_ref[0] = p[:, P_SSM:P_TOTAL]
    q_lat = _rms(p[:, P_QLAT:P_KVLAT], qag_ref[...])
    kv_lat = _rms(p[:, P_KVLAT:P_KR], kvag_ref[...])
    q = _dot(q_lat.astype(BF16), wqb_ref[...])
    kv = _dot(kv_lat.astype(BF16), wkvb_ref[...])
    lane = lax.broadcasted_iota(jnp.int32, (tm, LANES), 1)
    half0 = jnp.bitwise_and(lane, 15) < 8
    cq, sq, ck, sk = cq_ref[...], sq_ref[...], ck_ref[...], sk_ref[...]
    kr = _rope(p[:, P_KR:P_HG], ck, sk, half0)
    nope = lane < MLA_NOPE
    for hh in range(MLA_HEADS):
        sl = slice(hh * LANES, (hh + 1) * LANES)
        q_ref[0, hh] = _rope(q[:, sl], cq, sq, half0).astype(q_ref.dtype)
        kvh = kv[:, sl]
        k_ref[0, hh] = jnp.where(nope, kvh, kr).astype(k_ref.dtype)
        v_ref[0, hh] = jnp.where(nope, 1.0, kvh).astype(v_ref.dtype)


def _inproj_call(x_all, modsel, g1, w_in, qa_g, kva_g, wqb, wkvb, tabs):
    b, n_all, d = x_all.shape
    tm = TOKEN_TILE
    nt = n_all // tm
    full = lambda shape: pl.BlockSpec(shape, lambda bb, i: (0,) * len(shape))
    tab = pl.BlockSpec((tm, LANES), lambda bb, i: (i, 0))
    head_out = pl.BlockSpec((1, MLA_HEADS, tm, LANES), lambda bb, i: (bb, 0, i, 0))
    est = 2 * (tm * d * 4 + w_in.size * 2 + wqb.size * 2 + wkvb.size * 2 + 3 * MLA_HEADS * tm * LANES * 2
               + tm * (HG_COLS + SSM_BLOCK) * 4) + 6 * tm * P_TOTAL * 4
    return pl.pallas_call(
        _inproj_kernel,
        grid=(b, nt),
        in_specs=[
            pl.BlockSpec((1, tm, d), lambda bb, i: (bb, i, 0)),
            pl.BlockSpec((1, 1, 1, modsel.shape[3]), lambda bb, i: (bb, jnp.minimum(i, 1), 0, 0)),
            full((1, d)),
            full(w_in.shape),
            full((1, MLA_Q_RANK)),
            full((1, MLA_KV_RANK)),
            full(wqb.shape),
            full(wkvb.shape),
            tab, tab, tab, tab,
        ],
        out_specs=[
            head_out, head_out, head_out,
            pl.BlockSpec((1, tm, HG_COLS), lambda bb, i: (bb, i, 0)),
            pl.BlockSpec((1, tm, SSM_BLOCK), lambda bb, i: (bb, i, 0)),
        ],
        out_shape=[
            jax.ShapeDtypeStruct((b, MLA_HEADS, n_all, LANES), BF16),
            jax.ShapeDtypeStruct((b, MLA_HEADS, n_all, LANES), BF16),
            jax.ShapeDtypeStruct((b, MLA_HEADS, n_all, LANES), BF16),
            jax.ShapeDtypeStruct((b, n_all, HG_COLS), F32),
            jax.ShapeDtypeStruct((b, n_all, SSM_BLOCK), F32),
        ],
        compiler_params=pltpu.CompilerParams(vmem_limit_bytes=_vmem_limit(est)),
        name="inproj_mla",
    )(x_all, modsel, g1, w_in, qa_g, kva_g, wqb, wkvb, *tabs)


def _attn_kernel(q_ref, k_ref, v_ref, o_ref, *, n_ctx):
    n_all = k_ref.shape[2]
    tq = q_ref.shape[2]

    def run(nk):
        ss = [_dot_nt(q_ref[0, j], k_ref[0, j, 0:nk, :]) for j in range(2)]
        ps = [jnp.exp2(s - jnp.max(s, axis=-1, keepdims=True)).astype(BF16) for s in ss]
        outs = []
        for j in range(2):
            o = _dot(ps[j], v_ref[0, j, 0:nk, :])
            outs.append(o / pltpu.roll(o, MLA_V, 1))
        lane = lax.broadcasted_iota(jnp.int32, (tq, LANES), 1)
        o_ref[0] = jnp.where(lane < MLA_V, pltpu.roll(outs[0], MLA_V, 1), outs[1]).astype(o_ref.dtype)

    is_ctx = pl.program_id(2) == 0

    @pl.when(is_ctx)
    def _():
        run(n_ctx)

    @pl.when(jnp.logical_not(is_ctx))
    def _():
        run(n_all)


def _attn_call(q, k, v, n_ctx):
    b, h, n_all, _ = q.shape
    tq = TOKEN_TILE
    kv_spec = pl.BlockSpec((1, 2, n_all, LANES), lambda bb, hp, i: (bb, hp, 0, 0))
    est = 2 * (2 * tq * LANES * 2 + 2 * 2 * n_all * LANES * 2 + tq * LANES * 2) + 6 * tq * n_all * 4
    return pl.pallas_call(
        functools.partial(_attn_kernel, n_ctx=n_ctx),
        grid=(b, h // 2, n_all // tq),
        in_specs=[pl.BlockSpec((1, 2, tq, LANES), lambda bb, hp, i: (bb, hp, i, 0)), kv_spec, kv_spec],
        out_specs=pl.BlockSpec((1, tq, LANES), lambda bb, hp, i: (bb, i, hp)),
        out_shape=jax.ShapeDtypeStruct((b, n_all, h * MLA_V), BF16),
        compiler_params=pltpu.CompilerParams(vmem_limit_bytes=_vmem_limit(est)),
        name="mla_attention",
    )(q, k, v)


def _hg_consts(c):
    ri = lax.broadcasted_iota(jnp.int32, (c, c), 0)
    ci = lax.broadcasted_iota(jnp.int32, (c, c), 1)
    cst = {
        "tril": jnp.where(ci <= ri, 1.0, 0.0).astype(BF16),
        "triu": jnp.where(ci >= ri, 1.0, 0.0).astype(BF16),
        "lvl": {},
    }
    h = SUBLANES
    while 2 * h <= c:
        same = jnp.right_shift(ri, int(np.log2(2 * h))) == jnp.right_shift(ci, int(np.log2(2 * h)))
        r_lo = jnp.bitwise_and(ri, 2 * h - 1) < h
        c_lo = jnp.bitwise_and(ci, 2 * h - 1) < h
        fwd = jnp.where(same, jnp.where(r_lo, 0.0, jnp.where(c_lo, 1.0, 0.0)), 0.0)
        bwd = jnp.where(same, jnp.where(r_lo, jnp.where(c_lo, 0.0, 1.0), 0.0), 0.0)
        cst["lvl"][(h, False)] = fwd
        cst["lvl"][(h, True)] = bwd
        h *= 2
    r2 = lax.broadcasted_iota(jnp.int32, (LANES, LANES), 0)
    c2 = lax.broadcasted_iota(jnp.int32, (LANES, LANES), 1)
    bd = (r2 < HG_DK) == (c2 < HG_DK)
    cst["bd"] = bd
    cst["bo"] = jnp.where(bd, 1.0, 0.0).astype(BF16)
    lane = lax.broadcasted_iota(jnp.int32, (c, LANES), 1)
    cst["hm"] = [lane < HG_DK, lane >= HG_DK]
    cst["rit"] = jnp.bitwise_and(lax.broadcasted_iota(jnp.int32, (c, LANES), 0), SUBLANES - 1)
    return cst


def _hg_intra(streams, *, cst):
    c = streams[0][0].shape[0]
    ns = len(streams)
    bs = [_cumsum_mm(cst["triu"] if s[5] else cst["tril"], s[3]) for s in streams]
    tots = [b[0:1, :] if s[5] else b[c - 1:c, :] for b, s in zip(bs, streams)]
    qes = [(s[0] * jnp.exp2(b)).astype(BF16) for b, s in zip(bs, streams)]
    atts = [[None, None] for _ in range(ns)]
    h = SUBLANES
    while 2 * h <= c:
        ops = []
        for b, (q, k, f, g, v, rev) in zip(bs, streams):
            pieces = []
            for blk in range(c // (2 * h)):
                row = blk * 2 * h + (h if rev else h - 1)
                pieces.append(jnp.broadcast_to(b[row:row + 1, :], (2 * h, LANES)))
            ref = pieces[0] if len(pieces) == 1 else jnp.concatenate(pieces, axis=0)
            e = jnp.exp2(-jnp.abs(b - ref))
            qh = q * e
            ops.append(([jnp.where(cst["hm"][j], qh, 0.0).astype(BF16) for j in range(2)], (k * e).astype(BF16)))
        prods = [[_dot_nt(qj, kh) for qj in qjs] for qjs, kh in ops]
        for si in range(ns):
            msk = cst["lvl"][(h, streams[si][5])]
            for j in range(2):
                t = prods[si][j] * msk
                atts[si][j] = t if atts[si][j] is None else atts[si][j] + t
        h *= 2
    outs = []
    for si, (q, k, f, g, v, rev) in enumerate(streams):
        o = None
        for j in range(2):
            vj = jnp.where(cst["hm"][j], v, 0.0).astype(BF16)
            oj = _dot(atts[si][j].astype(BF16), vj)
            o = oj if o is None else o + oj
        outs.append(o)

    def shift(x, jj, rev):
        if jj == 0:
            return x
        x3 = x.reshape(c // SUBLANES, SUBLANES, LANES)
        return pltpu.roll(x3, (SUBLANES - jj) if rev else jj, 1).reshape(c, LANES)

    es = [None] * ns
    for dd in range(SUBLANES):
        ws = []
        for si, (q, k, f, g, v, rev) in enumerate(streams):
            if dd == 0:
                w = q * k
            else:
                fd = shift(f, dd - 1, rev)
                es[si] = fd if es[si] is None else es[si] * fd
                valid = (cst["rit"] <= SUBLANES - 1 - dd) if rev else (cst["rit"] >= dd)
                w = jnp.where(valid, q * shift(k, dd, rev) * es[si], 0.0)
            ws.append(w.astype(BF16))
        sums = [_dot(w, cst["bo"]) for w in ws]
        for si, (q, k, f, g, v, rev) in enumerate(streams):
            outs[si] = outs[si] + sums[si] * shift(v, dd, rev)
    res = []
    for si, (q, k, f, g, v, rev) in enumerate(streams):
        ke = (k * jnp.exp2(tots[si] - bs[si])).astype(BF16)
        upd = jnp.where(cst["bd"], _dot_tn(v.astype(BF16), ke), 0.0)
        res.append((outs[si], qes[si], upd, jnp.exp2(tots[si])))
    return res


def _hgrn_kernel(q_ref, fff_ref, ffb_ref, iv_ref, og_ref, lb_ref, ng_ref, r_ref,
                 o_ref, qef_ref, qeb_ref, updf_ref, updb_ref, decf_ref, decb_ref, *, n_ctx):
    c = HG_CHUNK
    n_all = q_ref.shape[1]
    nc = n_all // c
    ncc = n_ctx // c
    cst = _hg_consts(c)
    lbf = lb_ref[0:1, :]
    lbb = lb_ref[1:2, :]
    ng = ng_ref[...]
    dirs = ((False, fff_ref, lbf, qef_ref, updf_ref, decf_ref), (True, ffb_ref, lbb, qeb_ref, updb_ref, decb_ref))

    nb = HG_BATCH
    assert nc % nb == 0

    def intra(i, carry):
        streams, where = [], []
        for u in range(nb):
            ci = i * nb + u
            rows = pl.ds(pl.multiple_of(ci * c, c), c)
            q = _silu(q_ref[0, rows, :])
            v = iv_ref[0, rows, :]
            for rev, ff_ref, lb, qe_ref, upd_ref, dec_ref in dirs:
                sig, sigm = _sigmoid_pair(ff_ref[0, rows, :])
                f = lb + (1.0 - lb) * sig
                streams.append((q, (1.0 - lb) * sigm, f, jnp.log2(f), v, rev))
                where.append((ci, rows, qe_ref, upd_ref, dec_ref))
        res = _hg_intra(streams, cst=cst)
        for u in range(nb):
            o = res[2 * u][0] + res[2 * u + 1][0]
            o_ref[where[2 * u][1], :] = o
        for (od, qe, upd, dec), (ci, rows, qe_ref, upd_ref, dec_ref) in zip(res, where):
            qe_ref[rows, :] = qe
            upd_ref[ci] = upd
            dec_ref[ci] = jnp.broadcast_to(dec, (SUBLANES, LANES))
        return carry

    lax.fori_loop(0, nc // nb, intra, 0)

    def scan(i, carry):
        st_f, st_b = carry
        rows = pl.ds(pl.multiple_of(i * c, c), c)
        o_ref[rows, :] = o_ref[rows, :] + _dot_nt(qef_ref[rows, :], st_f.astype(BF16))
        st_f = st_f * decf_ref[i][0:1, :] + updf_ref[i]
        ib = jnp.where(i < ncc, ncc - 1 - i, nc + ncc - 1 - i)
        rows = pl.ds(pl.multiple_of(ib * c, c), c)
        o_ref[rows, :] = o_ref[rows, :] + _dot_nt(qeb_ref[rows, :], st_b.astype(BF16))
        st_b = st_b * decb_ref[ib][0:1, :] + updb_ref[ib]
        return st_f, st_b

    zero = jnp.zeros((LANES, LANES), F32)
    lax.fori_loop(0, nc, scan, (zero, zero), unroll=2)

    tr = TOKEN_TILE

    def readout(i, carry):
        rows = pl.ds(pl.multiple_of(i * tr, tr), tr)
        o = o_ref[rows, :]
        ms = _dot((o * o).astype(BF16), cst["bo"]) * (1.0 / HG_DV)
        r = o * lax.rsqrt(ms + EPS) * ng * _silu(og_ref[0, rows, :])
        r_ref[0, rows, :] = r.astype(r_ref.dtype)
        return carry

    lax.fori_loop(0, n_all // tr, readout, 0)


def _hgrn_call(p_hg, lb, ng, n_ctx):
    b, n_all, _ = p_hg.shape
    nc = n_all // HG_CHUNK
    col = lambda j: pl.BlockSpec((1, n_all, LANES), lambda bb, pr: (bb, 0, 2 * j + pr))
    scratch = [pltpu.VMEM((n_all, LANES), F32), pltpu.VMEM((n_all, LANES), BF16), pltpu.VMEM((n_all, LANES), BF16),
               pltpu.VMEM((nc, LANES, LANES), F32), pltpu.VMEM((nc, LANES, LANES), F32),
               pltpu.VMEM((nc, SUBLANES, LANES), F32), pltpu.VMEM((nc, SUBLANES, LANES), F32)]
    est = 2 * 6 * n_all * LANES * 4 + 2 * n_all * LANES * 4 + 2 * nc * LANES * LANES * 4 + 8 * 1024 * 1024
    return pl.pallas_call(
        functools.partial(_hgrn_kernel, n_ctx=n_ctx),
        grid=(b, 2),
        in_specs=[col(0), col(1), col(2), col(3), col(4),
                  pl.BlockSpec((2, LANES), lambda bb, pr: (0, pr)),
                  pl.BlockSpec((1, LANES), lambda bb, pr: (0, pr))],
        out_specs=pl.BlockSpec((1, n_all, LANES), lambda bb, pr: (bb, 0, pr)),
        out_shape=jax.ShapeDtypeStruct((b, n_all, HG_HEADS * HG_DV), BF16),
        scratch_shapes=scratch,
        compiler_params=pltpu.CompilerParams(vmem_limit_bytes=_vmem_limit(est)),
        name="hgrn2_bidir",
    )(p_hg, p_hg, p_hg, p_hg, p_hg, lb, ng)


def _ssd_intra(chunks, *, cst, dtb, a_neg):
    n = chunks[0][0].shape[0]
    lo = cst["lane"] < SSM_STATE
    hi = jnp.logical_not(lo)
    xss = [xa[:, 0:LANES] for xa, _ in chunks]
    bms, cms = [], []
    for xa, _ in chunks:
        bc = xa[:, LANES:2 * LANES]
        bms.append(jnp.where(lo, bc, 0.0).astype(BF16))
        cms.append(jnp.where(lo, pltpu.roll(bc, SSM_STATE, 1), 0.0).astype(BF16))
    gmats = [_dot_nt(cm, bm) for cm, bm in zip(cms, bms)]
    pairs = [(ci, d) for ci in range(len(chunks)) for d in range(2)]
    dts, css = [], []
    for ci, d in pairs:
        sl = slice(d * LANES, (d + 1) * LANES)
        xdt = chunks[ci][1][:, sl] + dtb[:, sl]
        dt = jnp.maximum(xdt, 0.0) + jnp.log(1.0 + jnp.exp(-jnp.abs(xdt)))
        dts.append(dt)
        css.append(_cumsum_mm(cst["triu"] if d else cst["tril"], dt * a_neg[:, sl]))
    tots = [cs[0:1, :] if d else cs[n - 1:n, :] for cs, (ci, d) in zip(css, pairs)]
    csrs = [pltpu.roll(cs, SSM_HEADDIM, 1) for cs in css]
    csts = [cs.T for cs in css]
    xds = [xss[ci] * dt for dt, (ci, d) in zip(dts, pairs)]
    lhs, rhs = [], []
    for pi, (ci, d) in enumerate(pairs):
        for j in range(2):
            colb = jnp.where(lo, css[pi], csrs[pi]) if j == 0 else jnp.where(lo, csrs[pi], css[pi])
            rowb = jnp.broadcast_to(csts[pi][j * SSM_HEADDIM:j * SSM_HEADDIM + 1, :], (n, n))
            lmat = jnp.where(cst["tri_mask_b" if d else "tri_mask_f"], jnp.exp2(jnp.minimum(colb - rowb, 0.0)), 0.0)
            lhs.append((gmats[ci] * lmat).astype(BF16))
            rhs.append(jnp.where(lo if j == 0 else hi, xds[pi], 0.0).astype(BF16))
    prods = [_dot(a, b) for a, b in zip(lhs, rhs)]
    xins = [(xd * jnp.exp2(tot - cs)).astype(BF16) for xd, tot, cs in zip(xds, tots, css)]
    upds = [_dot_tn(bms[ci], xin) for xin, (ci, d) in zip(xins, pairs)]
    res = []
    for ci in range(len(chunks)):
        y = prods[4 * ci] + prods[4 * ci + 1] + prods[4 * ci + 2] + prods[4 * ci + 3]
        outs = [(jnp.exp2(css[2 * ci + d]), upds[2 * ci + d], jnp.exp2(tots[2 * ci + d])) for d in range(2)]
        res.append((y, cms[ci], outs))
    return res


def _ssd_kernel(z_ref, xbc_ref, dt_ref, cw_ref, cb_ref, dtb_ref, alog_ref, dsk_ref, ng_ref,
                y_ref, xa_ref, ya_ref, cm_ref, dof_ref, dob_ref, updf_ref, updb_ref, decf_ref, decb_ref, *, n_ctx):
    c = SSM_CHUNK
    n_all = z_ref.shape[1]
    nc = n_all // c
    ncc = n_ctx // c
    ri = lax.broadcasted_iota(jnp.int32, (c, c), 0)
    ci_ = lax.broadcasted_iota(jnp.int32, (c, c), 1)
    cst = {
        "tril": jnp.where(ci_ <= ri, 1.0, 0.0).astype(BF16),
        "triu": jnp.where(ci_ >= ri, 1.0, 0.0).astype(BF16),
        "tri_mask_f": ci_ <= ri,
        "tri_mask_b": ci_ >= ri,
        "lane": lax.broadcasted_iota(jnp.int32, (c, LANES), 1),
    }
    dtb = dtb_ref[...]
    a_neg = -jnp.exp(alog_ref[...]) * float(np.log2(np.e))
    cb = cb_ref[...]
    halo = SUBLANES

    def conv_body(ci, carry):
        r0 = pl.multiple_of(ci * c, c)
        cur = xbc_ref[0, pl.ds(r0, c), :]
        first = jnp.logical_or(ci == 0, ci == ncc)
        last = jnp.logical_or(ci == ncc - 1, ci == nc - 1)
        rp = pl.multiple_of(jnp.maximum(r0 - halo, 0), halo)
        rn = pl.multiple_of(jnp.minimum(r0 + c, n_all - halo), halo)
        prev = xbc_ref[0, pl.ds(rp, halo), :] * jnp.where(first, 0.0, 1.0)
        nxt = xbc_ref[0, pl.ds(rn, halo), :] * jnp.where(last, 0.0, 1.0)
        ext = jnp.concatenate([prev, cur, nxt], axis=0)
        acc = jnp.broadcast_to(cb, (c, 2 * LANES))
        for j in range(SSM_CONV):
            s = (SSM_CONV // 2 - j) % (c + 2 * halo)
            sh = ext if s == 0 else pltpu.roll(ext, s, 0)
            acc = acc + cw_ref[j:j + 1, :] * sh[halo:halo + c, :]
        xa_ref[pl.ds(r0, c), :] = _silu(acc)
        return carry

    lax.fori_loop(0, nc, conv_body, 0)

    nb = SSM_BATCH
    assert nc % nb == 0

    def intra(i, carry):
        cis = [i * nb + u for u in range(nb)]
        rws = [pl.ds(pl.multiple_of(ci * c, c), c) for ci in cis]
        res = _ssd_intra([(xa_ref[rows, :], dt_ref[0, rows, :]) for rows in rws], cst=cst, dtb=dtb, a_neg=a_neg)
        for ci, rows, (y, cm, outs) in zip(cis, rws, res):
            ya_ref[rows, :] = y
            cm_ref[rows, :] = cm
            for (dec_out, upd, dec), do_ref, upd_ref, dec_ref in zip(outs, (dof_ref, dob_ref), (updf_ref, updb_ref), (decf_ref, decb_ref)):
                do_ref[rows, :] = dec_out
                upd_ref[ci] = upd
                dec_ref[ci] = jnp.broadcast_to(dec, (SUBLANES, LANES))
        return carry

    lax.fori_loop(0, nc // nb, intra, 0)

    def scan(i, carry):
        st_f, st_b = carry
        rows = pl.ds(pl.multiple_of(i * c, c), c)
        ya_ref[rows, :] = ya_ref[rows, :] + _dot(cm_ref[rows, :], st_f.astype(BF16)) * dof_ref[rows, :]
        st_f = st_f * decf_ref[i][0:1, :] + updf_ref[i]
        ib = jnp.where(i < ncc, ncc - 1 - i, nc + ncc - 1 - i)
        rows = pl.ds(pl.multiple_of(ib * c, c), c)
        ya_ref[rows, :] = ya_ref[rows, :] + _dot(cm_ref[rows, :], st_b.astype(BF16)) * dob_ref[rows, :]
        st_b = st_b * decb_ref[ib][0:1, :] + updb_ref[ib]
        return st_f, st_b

    zst = jnp.zeros((LANES, LANES), F32)
    lax.fori_loop(0, nc, scan, (zst, zst), unroll=2)

    dsk = dsk_ref[...]
    ng = ng_ref[...]
    tr = TOKEN_TILE

    def readout(i, carry):
        rows = pl.ds(pl.multiple_of(i * tr, tr), tr)
        y = ya_ref[rows, :] + dsk * xa_ref[rows, 0:LANES]
        yz = y * _silu(z_ref[0, rows, :])
        y_ref[0, rows, :] = _rms(yz, ng).astype(y_ref.dtype)
        return carry

    lax.fori_loop(0, n_all // tr, readout, 0)


def _ssd_call(p_ssm, conv_w, conv_b, dtb, alog, dsk, ng, n_ctx):
    b, n_all, _ = p_ssm.shape
    assert SSM_CHUNK == LANES
    nc = n_all // SSM_CHUNK
    vec = pl.BlockSpec((1, LANES), lambda bb, g: (0, g))
    vec2 = pl.BlockSpec((1, 2 * LANES), lambda bb, g: (0, g))
    scratch = [pltpu.VMEM((n_all, 2 * LANES), F32), pltpu.VMEM((n_all, LANES), F32), pltpu.VMEM((n_all, LANES), BF16),
               pltpu.VMEM((n_all, LANES), F32), pltpu.VMEM((n_all, LANES), F32),
               pltpu.VMEM((nc, LANES, LANES), F32), pltpu.VMEM((nc, LANES, LANES), F32),
               pltpu.VMEM((nc, SUBLANES, LANES), F32), pltpu.VMEM((nc, SUBLANES, LANES), F32)]
    est = 2 * (n_all * 5 * LANES * 4 + n_all * LANES * 2) + n_all * 6 * LANES * 4 + 2 * nc * LANES * LANES * 4 + 8 * 1024 * 1024
    return pl.pallas_call(
        functools.partial(_ssd_kernel, n_ctx=n_ctx),
        grid=(b, SSM_GROUPS),
        in_specs=[
            pl.BlockSpec((1, n_all, LANES), lambda bb, g: (bb, 0, g)),
            pl.BlockSpec((1, n_all, 2 * LANES), lambda bb, g: (bb, 0, 1 + g)),
            pl.BlockSpec((1, n_all, 2 * LANES), lambda bb, g: (bb, 0, 3 + g)),
            pl.BlockSpec((SSM_CONV, 2 * LANES), lambda bb, g: (0, g)),
            pl.BlockSpec((1, 2 * LANES), lambda bb, g: (0, g)),
            vec2, vec2, vec, vec,
        ],
        out_specs=pl.BlockSpec((1, n_all, LANES), lambda bb, g: (bb, 0, g)),
        out_shape=jax.ShapeDtypeStruct((b, n_all, SSM_INNER), BF16),
        scratch_shapes=scratch,
        compiler_params=pltpu.CompilerParams(vmem_limit_bytes=_vmem_limit(est)),
        name="ssd_bidir",
    )(p_ssm, p_ssm, p_ssm, conv_w, conv_b, dtb, alog, dsk, ng)


def _ffn_kernel(x_ref, a_ref, r_ref, s_ref, mod_ref, g2_ref, wa_ref, wr_ref, ws_ref, w1_ref, w2_ref,
                o_ref, *, hid_chunk):
    d = x_ref.shape[2]
    hidden = w2_ref.shape[0]
    x = x_ref[0]
    mod = lambda j: mod_ref[0, 0, :, j * d:(j + 1) * d]
    mix = _dot(a_ref[0], wa_ref[...]) + _dot(r_ref[0], wr_ref[...]) + _dot(s_ref[0], ws_ref[...])
    x1 = x + mod(2) * mix
    h2 = (_rms(x1, g2_ref[...]) * (1.0 + mod(4)) + mod(3)).astype(BF16)
    y = None
    for c0 in range(0, hidden, hid_chunk):
        ha = _dot(h2, w1_ref[:, c0:c0 + hid_chunk])
        hb = _dot(h2, w1_ref[:, hidden + c0:hidden + c0 + hid_chunk])
        act = (_silu(ha) * hb).astype(BF16)
        yc = _dot(act, w2_ref[c0:c0 + hid_chunk, :])
        y = yc if y is None else y + yc
    o_ref[0] = x1 + mod(5) * y


def _ffn_call(x_all, a, r, s, modsel, g2, wa, wr, ws, w1, w2):
    b, n_all, d = x_all.shape
    tm = TOKEN_TILE
    hidden = w2.shape[0]
    hid_chunk = hidden // 2
    full = lambda arr: pl.BlockSpec(arr.shape, lambda bb, i: (0,) * arr.ndim, pipeline_mode=pl.Buffered(1))
    tok = lambda w: pl.BlockSpec((1, tm, w), lambda bb, i: (bb, i, 0))
    est = (wa.size + wr.size + ws.size + w1.size + w2.size) * 2 + 2 * (2 * tm * d * 4 + tm * 1024 * 2) \
        + 6 * tm * hid_chunk * 4 + 6 * tm * d * 4
    return pl.pallas_call(
        functools.partial(_ffn_kernel, hid_chunk=hid_chunk),
        grid=(b, n_all // tm),
        in_specs=[
            tok(d), tok(a.shape[2]), tok(r.shape[2]), tok(s.shape[2]),
            pl.BlockSpec((1, 1, 1, modsel.shape[3]), lambda bb, i: (bb, jnp.minimum(i, 1), 0, 0)),
            pl.BlockSpec((1, d), lambda bb, i: (0, 0)),
            full(wa), full(wr), full(ws), full(w1), full(w2),
        ],
        out_specs=tok(d),
        out_shape=jax.ShapeDtypeStruct((b, n_all, d), F32),
        compiler_params=pltpu.CompilerParams(vmem_limit_bytes=_vmem_limit(est)),
        name="outproj_ffn",
    )(x_all, a, r, s, modsel, g2, wa, wr, ws, w1, w2)


def _final_kernel(x_ref, g_ref, o_ref):
    o_ref[0] = _rms(x_ref[0], g_ref[...])


def _final_call(x_all, g, n_ctx):
    b, n_all, d = x_all.shape
    tm = TOKEN_TILE
    skip = n_ctx // tm
    return pl.pallas_call(
        _final_kernel,
        grid=(b, (n_all - n_ctx) // tm),
        in_specs=[pl.BlockSpec((1, tm, d), lambda bb, i: (bb, i + skip, 0)),
                  pl.BlockSpec((1, d), lambda bb, i: (0, 0))],
        out_specs=pl.BlockSpec((1, tm, d), lambda bb, i: (bb, i, 0)),
        out_shape=jax.ShapeDtypeStruct((b, n_all - n_ctx, d), F32),
        name="final_norm",
    )(x_all, g)


def _win_perm():
    perm = np.full((P_TOTAL,), -1, np.int64)
    perm[P_QLAT:P_QLAT + MLA_Q_RANK + MLA_KV_RANK] = np.arange(MLA_Q_RANK + MLA_KV_RANK)
    perm[P_KR + MLA_NOPE:P_KR + MLA_QK] = MLA_Q_RANK + MLA_KV_RANK + np.arange(MLA_ROPE)
    perm[P_HG:P_HG + HG_COLS] = MLA_COLS + np.arange(HG_COLS)
    o2 = MLA_COLS + HG_COLS
    perm[P_SSM:P_SSM + SSM_INNER] = o2 + np.arange(SSM_INNER)
    xo = o2 + SSM_INNER
    bo = xo + SSM_INNER
    co = bo + SSM_GROUPS * SSM_STATE
    dto = o2 + SSM_INNER + SSM_XBC
    for g in range(SSM_GROUPS):
        base = P_SSM + SSM_INNER + g * 2 * LANES
        perm[base:base + LANES] = xo + g * LANES + np.arange(LANES)
        perm[base + LANES:base + LANES + SSM_STATE] = bo + g * SSM_STATE + np.arange(SSM_STATE)
        perm[base + LANES + SSM_STATE:base + 2 * LANES] = co + g * SSM_STATE + np.arange(SSM_STATE)
        dbase = P_SSM + SSM_INNER + 2 * 2 * LANES + g * 2 * LANES
        for d in range(2):
            for j in range(2):
                lo = dbase + d * LANES + j * SSM_HEADDIM
                perm[lo:lo + SSM_HEADDIM] = dto + d * SSM_HEADS + 2 * g + j
    return perm


def _conv_perm():
    perm = np.zeros((SSM_XBC,), np.int64)
    for g in range(SSM_GROUPS):
        base = g * 2 * LANES
        perm[base:base + LANES] = g * LANES + np.arange(LANES)
        perm[base + LANES:base + LANES + SSM_STATE] = SSM_INNER + g * SSM_STATE + np.arange(SSM_STATE)
        perm[base + LANES + SSM_STATE:base + 2 * LANES] = SSM_INNER + SSM_GROUPS * SSM_STATE + g * SSM_STATE + np.arange(SSM_STATE)
    return perm


def _gather_cols(w, perm):
    idx = jnp.asarray(np.maximum(perm, 0), jnp.int32)
    out = jnp.take(w, idx, axis=-1)
    return jnp.where(jnp.asarray(perm >= 0), out, 0.0)


def _head_vec(v):
    depth = v.shape[0]
    v5 = v.reshape(depth, 2, SSM_GROUPS, 2, 1)
    v5 = jnp.broadcast_to(v5, (depth, 2, SSM_GROUPS, 2, SSM_HEADDIM))
    return jnp.transpose(v5, (0, 2, 1, 3, 4)).reshape(depth, 1, SSM_GROUPS * 2 * LANES)


def _rope_tables(n_ctx, n_lat, scale):
    rows = n_lat // GRID_W
    row = jnp.repeat(jnp.arange(rows, dtype=F32), GRID_W)
    col = jnp.tile(jnp.arange(GRID_W, dtype=F32), rows)
    n_freq = MLA_ROPE // 4
    inv = ROPE_BASE ** (-jnp.arange(n_freq, dtype=F32) / n_freq)
    ang = jnp.stack([row[:, None] * inv, col[:, None] * inv], axis=1)
    cos, sin = jnp.cos(ang), jnp.sin(ang)
    c_r = jnp.concatenate([cos[:, 0], cos[:, 0], cos[:, 1], cos[:, 1]], axis=-1)
    s_r = jnp.concatenate([-sin[:, 0], sin[:, 0], -sin[:, 1], sin[:, 1]], axis=-1)
    c_lat = jnp.concatenate([jnp.ones((n_lat, MLA_NOPE), F32), c_r, jnp.ones((n_lat, LANES - MLA_QK), F32)], axis=-1)
    s_lat = jnp.concatenate([jnp.zeros((n_lat, MLA_NOPE), F32), s_r, jnp.zeros((n_lat, LANES - MLA_QK), F32)], axis=-1)
    c_all = jnp.concatenate([jnp.ones((n_ctx, LANES), F32), c_lat], axis=0)
    s_all = jnp.concatenate([jnp.zeros((n_ctx, LANES), F32), s_lat], axis=0)
    return c_all * scale, s_all * scale, c_all, s_all


def kernel(x, c, ctx, c_ctx, w_ada, b_ada, norm1_g, norm2_g, w_in, mla_qa_g, mla_wqb, mla_kva_g, mla_wkvb, hg_lb_logits, hg_norm_g, ssm_conv_w, ssm_conv_b, ssm_dt_bias, ssm_a_log, ssm_d, ssm_norm_g, w_out, w_ffn_in, w_ffn_out, final_g):
    bsz, n_lat, d = x.shape
    n_ctx = ctx.shape[1]
    depth = w_ada.shape[0]
    assert n_ctx % TOKEN_TILE == 0 and n_lat % TOKEN_TILE == 0 and n_lat % GRID_W == 0

    rows = -(-(bsz + 1) // SUBLANES) * SUBLANES
    cc = jnp.zeros((rows, d), F32).at[:bsz].set(c).at[bsz].set(c_ctx)
    mods = _mod_call(cc, w_ada.astype(BF16), b_ada.reshape(depth, 1, 6 * d))
    modsel = jnp.stack([jnp.broadcast_to(mods[:, bsz:bsz + 1], (depth, bsz, 6 * d)), mods[:, :bsz]], axis=2)[:, :, :, None, :]

    w_in_p = _gather_cols(w_in, _win_perm()).astype(BF16)
    wqb4 = mla_wqb.reshape(depth, MLA_Q_RANK, MLA_HEADS, MLA_QK)
    wqb_p = jnp.pad(wqb4, ((0, 0), (0, 0), (0, 0), (0, LANES - MLA_QK))).reshape(depth, MLA_Q_RANK, MLA_HEADS * LANES).astype(BF16)
    wkvb_b = mla_wkvb.astype(BF16)
    cperm = _conv_perm()
    conv_w_p = jnp.take(ssm_conv_w, jnp.asarray(cperm, jnp.int32), axis=-1)
    conv_b_p = jnp.take(ssm_conv_b, jnp.asarray(cperm, jnp.int32), axis=-1).reshape(depth, 1, SSM_XBC)
    dtb_p = _head_vec(ssm_dt_bias)
    alog_p = _head_vec(ssm_a_log)
    dsk_p = jnp.repeat(ssm_d, SSM_HEADDIM, axis=-1).reshape(depth, 1, SSM_INNER)
    lb_soft = jax.nn.softmax(hg_lb_logits.astype(F32), axis=0)
    lb_all = jnp.cumsum(lb_soft, axis=0) - lb_soft[0]
    w_out_b = w_out.astype(BF16)
    a_w = MLA_HEADS * MLA_V
    r_w = HG_HEADS * HG_DV
    w1_b = w_ffn_in.astype(BF16)
    w2_b = w_ffn_out.astype(BF16)
    tabs = _rope_tables(n_ctx, n_lat, MLA_QK ** -0.5 * float(np.log2(np.e)))

    x_all = jnp.concatenate([ctx, x], axis=1)
    for l in range(depth):
        q, k, v, p_hg, p_ssm = _inproj_call(
            x_all, modsel[l], norm1_g[l].reshape(1, d), w_in_p[l], mla_qa_g[l].reshape(1, -1),
            mla_kva_g[l].reshape(1, -1), wqb_p[l], wkvb_b[l], tabs)
        a = _attn_call(q, k, v, n_ctx)
        r = _hgrn_call(p_hg, lb_all[l], hg_norm_g[l].reshape(1, -1), n_ctx)
        s = _ssd_call(p_ssm, conv_w_p[l], conv_b_p[l], dtb_p[l], alog_p[l], dsk_p[l],
                      ssm_norm_g[l].reshape(1, -1), n_ctx)
        x_all = _ffn_call(x_all, a, r, s, modsel[l], norm2_g[l].reshape(1, d),
                          w_out_b[l, :a_w], w_out_b[l, a_w:a_w + r_w], w_out_b[l, a_w + r_w:], w1_b[l], w2_b[l])
    return _final_call(x_all, final_g.reshape(1, d), n_ctx)
```

```python
import functools

import numpy as np
import jax
import jax.numpy as jnp
from jax import lax
from jax.experimental import pallas as pl
from jax.experimental.pallas import tpu as pltpu

F32 = jnp.float32
BF16 = jnp.bfloat16
EPS = 1e-6

LANES = 128
SUBLANES = 8
VMEM_BYTES = 64 * 1024 * 1024

GRID_W = 64
MLA_HEADS = 8
MLA_Q_RANK = 384
MLA_KV_RANK = 256
MLA_NOPE = 64
MLA_ROPE = 32
MLA_V = 64
MLA_QK = MLA_NOPE + MLA_ROPE
ROPE_BASE = 10000.0
HG_HEADS = 4
HG_DK = 64
HG_DV = 64
HG_W = HG_HEADS * HG_DK
SSM_HEADS = 4
SSM_HEADDIM = 64
SSM_GROUPS = 2
SSM_STATE = 64
SSM_CONV = 5
SSM_INNER = SSM_HEADS * SSM_HEADDIM
SSM_XBC = SSM_INNER + 2 * SSM_GROUPS * SSM_STATE

MLA_COLS = MLA_Q_RANK + MLA_KV_RANK + MLA_ROPE
HG_COLS = 3 * HG_W + 2 * HG_HEADS * HG_DV
SSM_COLS = SSM_INNER + SSM_XBC + 2 * SSM_HEADS

P_QLAT = 0
P_KVLAT = MLA_Q_RANK
P_KR = P_KVLAT + MLA_KV_RANK
P_HG = P_KR + LANES
P_SSM = P_HG + HG_COLS
SSM_BLOCK = SSM_INNER + 2 * 2 * LANES + 2 * 2 * LANES
P_TOTAL = P_SSM + SSM_BLOCK

TOKEN_TILE = 512
ROW_TILE = 256
HG_CHUNK = 128
HG_BATCH = 2
SSM_CHUNK = 128
SSM_BATCH = 2
FFN_SPLIT = 2

NT_DIMS = (((1,), (1,)), ((), ()))
TN_DIMS = (((0,), (0,)), ((), ()))


def _vmem_limit(nbytes):
    return int(min(VMEM_BYTES - 8 * 1024 * 1024, max(nbytes, 16 * 1024 * 1024)))


def _sigmoid_pair(x):
    e = jnp.exp(-jnp.abs(x))
    d = 1.0 / (1.0 + e)
    ed = e * d
    pos = x >= 0
    return jnp.where(pos, d, ed), jnp.where(pos, ed, d)


def _silu(x):
    return x / (1.0 + jnp.exp(-x))


def _rms(x, g):
    ms = jnp.mean(x * x, axis=-1, keepdims=True)
    return x * lax.rsqrt(ms + EPS) * g


def _dot(a, b):
    return jnp.dot(a, b, preferred_element_type=F32)


def _dot_nt(a, b):
    return lax.dot_general(a, b, NT_DIMS, preferred_element_type=F32)


def _dot_tn(a, b):
    return lax.dot_general(a, b, TN_DIMS, preferred_element_type=F32)


def _split3(x):
    x1 = x.astype(BF16)
    r1 = x - x1.astype(F32)
    x2 = r1.astype(BF16)
    x3 = (r1 - x2.astype(F32)).astype(BF16)
    return x1, x2, x3


def _cumsum_mm(tri, x):
    x1, x2, x3 = _split3(x)
    return _dot(tri, x1) + _dot(tri, x2) + _dot(tri, x3)


def _mod_kernel(c_ref, w_ref, b_ref, o_ref):
    s = _silu(c_ref[...]).astype(BF16)
    o_ref[0] = _dot(s, w_ref[0]) + b_ref[0]


def _mod_call(cc, w_ada, b_ada):
    depth, d, six_d = w_ada.shape
    rows = cc.shape[0]
    tn = 1536
    return pl.pallas_call(
        _mod_kernel,
        grid=(depth, six_d // tn),
        in_specs=[
            pl.BlockSpec((rows, d), lambda l, j: (0, 0)),
            pl.BlockSpec((1, d, tn), lambda l, j: (l, 0, j)),
            pl.BlockSpec((1, 1, tn), lambda l, j: (l, 0, j)),
        ],
        out_specs=pl.BlockSpec((1, rows, tn), lambda l, j: (l, 0, j)),
        out_shape=jax.ShapeDtypeStruct((depth, rows, six_d), F32),
        name="adaln_mod",
    )(cc, w_ada, b_ada)


def _rope(xh, c, s, half0):
    sw = jnp.where(half0, pltpu.roll(xh, LANES - 8, 1), pltpu.roll(xh, 8, 1))
    return xh * c + sw * s


def _inproj_kernel(x_ref, mod_ref, g1_ref, win_ref, qag_ref, kvag_ref, wqb_ref, wkvb_ref,
                   cq_ref, sq_ref, ck_ref, sk_ref,
                   q_ref, k_ref, v_ref, hg_ref, ssm_ref):
    d = x_ref.shape[2]
    tm = x_ref.shape[1]
    x = x_ref[0]
    sh = mod_ref[0, 0, :, 0:d]
    sc = mod_ref[0, 0, :, d:2 * d]
    h = _rms(x, g1_ref[...]) * (1.0 + sc) + sh
    p = _dot(h.astype(BF16), win_ref[...])
    hg_ref[0] = p[:, P_HG:P_SSM]
    ssm_ref[0] = p[:, P_SSM:P_TOTAL]
    q_lat = _rms(p[:, P_QLAT:P_KVLAT], qag_ref[...])
    kv_lat = _rms(p[:, P_KVLAT:P_KR], kvag_ref[...])
    q = _dot(q_lat.astype(BF16), wqb_ref[...])
    kv = _dot(kv_lat.astype(BF16), wkvb_ref[...])
    lane = lax.broadcasted_iota(jnp.int32, (tm, LANES), 1)
    half0 = jnp.bitwise_and(lane, 15) < 8
    cq, sq, ck, sk = cq_ref[...], sq_ref[...], ck_ref[...], sk_ref[...]
    kr = _rope(p[:, P_KR:P_HG], ck, sk, half0)
    nope = lane < MLA_NOPE
    for hh in range(MLA_HEADS):
        sl = slice(hh * LANES, (hh + 1) * LANES)
        q_ref[0, hh] = _rope(q[:, sl], cq, sq, half0).astype(q_ref.dtype)
        kvh = kv[:, sl]
        k_ref[0, hh] = jnp.where(nope, kvh, kr).astype(k_ref.dtype)
        v_ref[0, hh] = jnp.where(nope, 1.0, kvh).astype(v_ref.dtype)


def _inproj_call(x_all, modsel, g1, w_in, qa_g, kva_g, wqb, wkvb, tabs, n_lat):
    b, n_all, d = x_all.shape
    tm = TOKEN_TILE
    nt = pl.cdiv(n_all, tm)
    nlt = n_lat // tm
    full = lambda shape: pl.BlockSpec(shape, lambda bb, i: (0,) * len(shape), pipeline_mode=pl.Buffered(1))
    tab = pl.BlockSpec((tm, LANES), lambda bb, i: (i, 0))
    head_out = pl.BlockSpec((1, MLA_HEADS, tm, LANES), lambda bb, i: (bb, 0, i, 0))
    est = (w_in.size + wqb.size + wkvb.size) * 2 + 2 * (tm * d * 4 + 3 * MLA_HEADS * tm * LANES * 2
                                                        + tm * (HG_COLS + SSM_BLOCK) * 4) + 4 * tm * P_TOTAL * 4
    return pl.pallas_call(
        _inproj_kernel,
        grid=(b, nt),
        in_specs=[
            pl.BlockSpec((1, tm, d), lambda bb, i: (bb, i, 0)),
            pl.BlockSpec((1, 1, 1, modsel.shape[3]), lambda bb, i: (bb, jnp.minimum(i // nlt, 1), 0, 0)),
            full((1, d)),
            full(w_in.shape),
            full((1, MLA_Q_RANK)),
            full((1, MLA_KV_RANK)),
            full(wqb.shape),
            full(wkvb.shape),
            tab, tab, tab, tab,
        ],
        out_specs=[
            head_out, head_out, head_out,
            pl.BlockSpec((1, tm, HG_COLS), lambda bb, i: (bb, i, 0)),
            pl.BlockSpec((1, tm, SSM_BLOCK), lambda bb, i: (bb, i, 0)),
        ],
        out_shape=[
            jax.ShapeDtypeStruct((b, MLA_HEADS, n_all, LANES), BF16),
            jax.ShapeDtypeStruct((b, MLA_HEADS, n_all, LANES), BF16),
            jax.ShapeDtypeStruct((b, MLA_HEADS, n_all, LANES), BF16),
            jax.ShapeDtypeStruct((b, n_all, HG_COLS), F32),
            jax.ShapeDtypeStruct((b, n_all, SSM_BLOCK), F32),
        ],
        compiler_params=pltpu.CompilerParams(vmem_limit_bytes=_vmem_limit(est)),
        name="inproj_mla",
    )(x_all, modsel, g1, w_in, qa_g, kva_g, wqb, wkvb, *tabs)


def _attn_kernel(q_ref, k_ref, v_ref, o_ref, *, n_lat):
    n_all = k_ref.shape[2]
    tq = q_ref.shape[2]

    def run(k0, k1):
        ss = [_dot_nt(q_ref[0, j], k_ref[0, j, k0:k1, :]) for j in range(2)]
        ps = [jnp.exp2(s - jnp.max(s, axis=-1, keepdims=True)).astype(BF16) for s in ss]
        outs = []
        for j in range(2):
            o = _dot(ps[j], v_ref[0, j, k0:k1, :])
            outs.append(o / pltpu.roll(o, MLA_V, 1))
        lane = lax.broadcasted_iota(jnp.int32, (tq, LANES), 1)
        o_ref[0] = jnp.where(lane < MLA_V, pltpu.roll(outs[0], MLA_V, 1), outs[1]).astype(o_ref.dtype)

    is_ctx = pl.program_id(2) >= n_lat // tq

    @pl.when(is_ctx)
    def _():
        run(n_lat, n_all)

    @pl.when(jnp.logical_not(is_ctx))
    def _():
        run(0, n_all)


def _attn_call(q, k, v, n_lat):
    b, h, n_all, _ = q.shape
    tq = TOKEN_TILE
    kv_spec = pl.BlockSpec((1, 2, n_all, LANES), lambda bb, hp, i: (bb, hp, 0, 0))
    est = 2 * (2 * tq * LANES * 2 + 2 * 2 * n_all * LANES * 2 + tq * LANES * 2) + 7 * tq * n_all * 4
    return pl.pallas_call(
        functools.partial(_attn_kernel, n_lat=n_lat),
        grid=(b, h // 2, pl.cdiv(n_all, tq)),
        in_specs=[pl.BlockSpec((1, 2, tq, LANES), lambda bb, hp, i: (bb, hp, i, 0)), kv_spec, kv_spec],
        out_specs=pl.BlockSpec((1, tq, LANES), lambda bb, hp, i: (bb, i, hp)),
        out_shape=jax.ShapeDtypeStruct((b, n_all, h * MLA_V), BF16),
        compiler_params=pltpu.CompilerParams(vmem_limit_bytes=_vmem_limit(est)),
        name="mla_attention",
    )(q, k, v)


def _hg_consts(c):
    ri = lax.broadcasted_iota(jnp.int32, (c, c), 0)
    ci = lax.broadcasted_iota(jnp.int32, (c, c), 1)
    cst = {
        "tril": jnp.where(ci <= ri, 1.0, 0.0).astype(BF16),
        "triu": jnp.where(ci >= ri, 1.0, 0.0).astype(BF16),
        "lvl": {},
    }
    h = SUBLANES
    while 2 * h <= c:
        same = jnp.right_shift(ri, int(np.log2(2 * h))) == jnp.right_shift(ci, int(np.log2(2 * h)))
        r_lo = jnp.bitwise_and(ri, 2 * h - 1) < h
        c_lo = jnp.bitwise_and(ci, 2 * h - 1) < h
        fwd = jnp.where(same, jnp.where(r_lo, 0.0, jnp.where(c_lo, 1.0, 0.0)), 0.0)
        bwd = jnp.where(same, jnp.where(r_lo, jnp.where(c_lo, 0.0, 1.0), 0.0), 0.0)
        cst["lvl"][(h, False)] = fwd
        cst["lvl"][(h, True)] = bwd
        h *= 2
    r2 = lax.broadcasted_iota(jnp.int32, (LANES, LANES), 0)
    c2 = lax.broadcasted_iota(jnp.int32, (LANES, LANES), 1)
    bd = (r2 < HG_DK) == (c2 < HG_DK)
    cst["bd"] = bd
    cst["bo"] = jnp.where(bd, 1.0, 0.0).astype(BF16)
    lane = lax.broadcasted_iota(jnp.int32, (c, LANES), 1)
    cst["hm"] = [lane < HG_DK, lane >= HG_DK]
    cst["rit"] = jnp.bitwise_and(lax.broadcasted_iota(jnp.int32, (c, LANES), 0), SUBLANES - 1)
    return cst


def _hg_intra(streams, *, cst):
    c = streams[0][0].shape[0]
    ns = len(streams)
    bs = [_cumsum_mm(cst["triu"] if s[5] else cst["tril"], s[3]) for s in streams]
    tots = [b[0:1, :] if s[5] else b[c - 1:c, :] for b, s in zip(bs, streams)]
    qes = [(s[0] * jnp.exp2(b)).astype(BF16) for b, s in zip(bs, streams)]
    atts = [[None, None] for _ in range(ns)]
    sign = jnp.uint32(0x80000000)
    h = SUBLANES
    while 2 * h <= c:
        ops = []
        for si, (b, (q, k, f, g, v, rev)) in enumerate(zip(bs, streams)):
            pieces = []
            for blk in range(c // (2 * h)):
                row = blk * 2 * h + (h if rev else h - 1)
                pieces.append(jnp.broadcast_to(b[row:row + 1, :], (2 * h, LANES)))
            ref = pieces[0] if len(pieces) == 1 else jnp.concatenate(pieces, axis=0)
            nabs = pltpu.bitcast(pltpu.bitcast(b - ref, jnp.uint32) | sign, F32)
            e = jnp.exp2(nabs)
            qh = q * e
            ops.append(([jnp.where(cst["hm"][j], qh, 0.0).astype(BF16) for j in range(2)], (k * e).astype(BF16)))
        prods = [[_dot_nt(qj, kh) for qj in qjs] for qjs, kh in ops]
        for si in range(ns):
            msk = cst["lvl"][(h, streams[si][5])]
            for j in range(2):
                t = prods[si][j] * msk
                atts[si][j] = t if atts[si][j] is None else atts[si][j] + t
        h *= 2
    outs = []
    for si, (q, k, f, g, v, rev) in enumerate(streams):
        o = None
        for j in range(2):
            vj = jnp.where(cst["hm"][j], v, 0.0).astype(BF16)
            oj = _dot(atts[si][j].astype(BF16), vj)
            o = oj if o is None else o + oj
        outs.append(o)

    def shift(x, jj, rev):
        if jj == 0:
            return x
        x3 = x.reshape(c // SUBLANES, SUBLANES, LANES)
        return pltpu.roll(x3, (SUBLANES - jj) if rev else jj, 1).reshape(c, LANES)

    es = [None] * ns
    for dd in range(SUBLANES):
        ws = []
        for si, (q, k, f, g, v, rev) in enumerate(streams):
            if dd == 0:
                w = q * k
            else:
                fd = shift(f, dd - 1, rev)
                es[si] = fd if es[si] is None else es[si] * fd
                valid = (cst["rit"] <= SUBLANES - 1 - dd) if rev else (cst["rit"] >= dd)
                w = jnp.where(valid, q * shift(k, dd, rev) * es[si], 0.0)
            ws.append(w.astype(BF16))
        sums = [_dot(w, cst["bo"]) for w in ws]
        for si, (q, k, f, g, v, rev) in enumerate(streams):
            outs[si] = outs[si] + sums[si] * shift(v, dd, rev)
    res = []
    for si, (q, k, f, g, v, rev) in enumerate(streams):
        ke = (k * jnp.exp2(tots[si] - bs[si])).astype(BF16)
        upd = jnp.where(cst["bd"], _dot_tn(v.astype(BF16), ke), 0.0)
        res.append((outs[si], qes[si], upd, jnp.exp2(tots[si])))
    return res


def _hgrn_kernel(q_ref, fff_ref, ffb_ref, iv_ref, og_ref, lb_ref, ng_ref, r_ref,
                 o_ref, qef_ref, qeb_ref, updf_ref, updb_ref, decf_ref, decb_ref, *, n_lat):
    c = HG_CHUNK
    n_all = q_ref.shape[1]
    nc = n_all // c
    ncc = (n_all - n_lat) // c
    cst = _hg_consts(c)
    lbf = lb_ref[0:1, :]
    lbb = lb_ref[1:2, :]
    ng = ng_ref[...]
    dirs = ((False, fff_ref, lbf, qef_ref, updf_ref, decf_ref), (True, ffb_ref, lbb, qeb_ref, updb_ref, decb_ref))

    nb = HG_BATCH
    assert nc % nb == 0

    def intra(i, carry):
        streams, where = [], []
        for u in range(nb):
            ci = i * nb + u
            rows = pl.ds(pl.multiple_of(ci * c, c), c)
            q = _silu(q_ref[0, rows, :])
            v = iv_ref[0, rows, :]
            for rev, ff_ref, lb, qe_ref, upd_ref, dec_ref in dirs:
                sig, sigm = _sigmoid_pair(ff_ref[0, rows, :])
                f = lb + (1.0 - lb) * sig
                streams.append((q, (1.0 - lb) * sigm, f, jnp.log2(f), v, rev))
                where.append((ci, rows, qe_ref, upd_ref, dec_ref))
        res = _hg_intra(streams, cst=cst)
        for u in range(nb):
            o = res[2 * u][0] + res[2 * u + 1][0]
            o_ref[where[2 * u][1], :] = o
        for (od, qe, upd, dec), (ci, rows, qe_ref, upd_ref, dec_ref) in zip(res, where):
            qe_ref[rows, :] = qe
            upd_ref[ci] = upd
            dec_ref[ci] = jnp.broadcast_to(dec, (SUBLANES, LANES))
        return carry

    lax.fori_loop(0, nc // nb, intra, 0)

    def scan(i, carry):
        st_f, st_b = carry
        jf = jnp.where(i < ncc, nc - ncc + i, i - ncc)
        rows = pl.ds(pl.multiple_of(jf * c, c), c)
        o_ref[rows, :] = o_ref[rows, :] + _dot_nt(qef_ref[rows, :], st_f.astype(BF16))
        st_f = st_f * decf_ref[jf][0:1, :] + updf_ref[jf]
        ib = nc - 1 - i
        rows = pl.ds(pl.multiple_of(ib * c, c), c)
        o_ref[rows, :] = o_ref[rows, :] + _dot_nt(qeb_ref[rows, :], st_b.astype(BF16))
        st_b = st_b * decb_ref[ib][0:1, :] + updb_ref[ib]
        return st_f, st_b

    zero = jnp.zeros((LANES, LANES), F32)
    lax.fori_loop(0, nc, scan, (zero, zero), unroll=2)

    tr = ROW_TILE

    def readout(i, carry):
        rows = pl.ds(pl.multiple_of(i * tr, tr), tr)
        o = o_ref[rows, :]
        ms = _dot((o * o).astype(BF16), cst["bo"]) * (1.0 / HG_DV)
        r = o * lax.rsqrt(ms + EPS) * ng * _silu(og_ref[0, rows, :])
        r_ref[0, rows, :] = r.astype(r_ref.dtype)
        return carry

    lax.fori_loop(0, n_all // tr, readout, 0)


def _hgrn_call(p_hg, lb, ng, n_lat):
    b, n_all, _ = p_hg.shape
    nc = n_all // HG_CHUNK
    col = lambda j: pl.BlockSpec((1, n_all, LANES), lambda bb, pr: (bb, 0, 2 * j + pr))
    scratch = [pltpu.VMEM((n_all, LANES), F32), pltpu.VMEM((n_all, LANES), BF16), pltpu.VMEM((n_all, LANES), BF16),
               pltpu.VMEM((nc, LANES, LANES), F32), pltpu.VMEM((nc, LANES, LANES), F32),
               pltpu.VMEM((nc, SUBLANES, LANES), F32), pltpu.VMEM((nc, SUBLANES, LANES), F32)]
    est = 2 * 6 * n_all * LANES * 4 + 2 * n_all * LANES * 4 + 2 * nc * LANES * LANES * 4 + 8 * 1024 * 1024
    return pl.pallas_call(
        functools.partial(_hgrn_kernel, n_lat=n_lat),
        grid=(b, 2),
        in_specs=[col(0), col(1), col(2), col(3), col(4),
                  pl.BlockSpec((2, LANES), lambda bb, pr: (0, pr)),
                  pl.BlockSpec((1, LANES), lambda bb, pr: (0, pr))],
        out_specs=pl.BlockSpec((1, n_all, LANES), lambda bb, pr: (bb, 0, pr)),
        out_shape=jax.ShapeDtypeStruct((b, n_all, HG_HEADS * HG_DV), BF16),
        scratch_shapes=scratch,
        compiler_params=pltpu.CompilerParams(vmem_limit_bytes=_vmem_limit(est)),
        name="hgrn2_bidir",
    )(p_hg, p_hg, p_hg, p_hg, p_hg, lb, ng)


def _ssd_intra(chunks, *, cst, dtb, a_neg):
    n = chunks[0][0].shape[0]
    lo = cst["lane"] < SSM_STATE
    hi = jnp.logical_not(lo)
    xss = [xa[:, 0:LANES] for xa, _ in chunks]
    bms, cms = [], []
    for xa, _ in chunks:
        bc = xa[:, LANES:2 * LANES]
        bms.append(jnp.where(lo, bc, 0.0).astype(BF16))
        cms.append(jnp.where(lo, pltpu.roll(bc, SSM_STATE, 1), 0.0).astype(BF16))
    gmats = [_dot_nt(cm, bm) for cm, bm in zip(cms, bms)]
    pairs = [(ci, d) for ci in range(len(chunks)) for d in range(2)]
    dts, css = [], []
    for ci, d in pairs:
        sl = slice(d * LANES, (d + 1) * LANES)
        xdt = chunks[ci][1][:, sl] + dtb[:, sl]
        dt = jnp.maximum(xdt, 0.0) + jnp.log(1.0 + jnp.exp(-jnp.abs(xdt)))
        dts.append(dt)
        css.append(_cumsum_mm(cst["triu"] if d else cst["tril"], dt * a_neg[:, sl]))
    tots = [cs[0:1, :] if d else cs[n - 1:n, :] for cs, (ci, d) in zip(css, pairs)]
    csrs = [pltpu.roll(cs, SSM_HEADDIM, 1) for cs in css]
    csts = [cs.T for cs in css]
    xds = [xss[ci] * dt for dt, (ci, d) in zip(dts, pairs)]
    lhs, rhs = [], []
    for pi, (ci, d) in enumerate(pairs):
        for j in range(2):
            colb = jnp.where(lo, css[pi], csrs[pi]) if j == 0 else jnp.where(lo, csrs[pi], css[pi])
            rowb = jnp.broadcast_to(csts[pi][j * SSM_HEADDIM:j * SSM_HEADDIM + 1, :], (n, n))
            lmat = jnp.where(cst["tri_mask_b" if d else "tri_mask_f"], jnp.exp2(jnp.minimum(colb - rowb, 0.0)), 0.0)
            lhs.append((gmats[ci] * lmat).astype(BF16))
            rhs.append(jnp.where(lo if j == 0 else hi, xds[pi], 0.0).astype(BF16))
    prods = [_dot(a, b) for a, b in zip(lhs, rhs)]
    xins = [(xd * jnp.exp2(tot - cs)).astype(BF16) for xd, tot, cs in zip(xds, tots, css)]
    upds = [_dot_tn(bms[ci], xin) for xin, (ci, d) in zip(xins, pairs)]
    res = []
    for ci in range(len(chunks)):
        y = prods[4 * ci] + prods[4 * ci + 1] + prods[4 * ci + 2] + prods[4 * ci + 3]
        outs = [(jnp.exp2(css[2 * ci + d]), upds[2 * ci + d], jnp.exp2(tots[2 * ci + d])) for d in range(2)]
        res.append((y, cms[ci], outs))
    return res


def _ssd_kernel(z_ref, xbc_ref, dt_ref, cw_ref, cb_ref, dtb_ref, alog_ref, dsk_ref, ng_ref,
                y_ref, xa_ref, ya_ref, cm_ref, dof_ref, dob_ref, updf_ref, updb_ref, decf_ref, decb_ref, *, n_lat):
    c = SSM_CHUNK
    n_all = z_ref.shape[1]
    nc = n_all // c
    ncc = (n_all - n_lat) // c
    nl = nc - ncc
    ri = lax.broadcasted_iota(jnp.int32, (c, c), 0)
    ci_ = lax.broadcasted_iota(jnp.int32, (c, c), 1)
    cst = {
        "tril": jnp.where(ci_ <= ri, 1.0, 0.0).astype(BF16),
        "triu": jnp.where(ci_ >= ri, 1.0, 0.0).astype(BF16),
        "tri_mask_f": ci_ <= ri,
        "tri_mask_b": ci_ >= ri,
        "lane": lax.broadcasted_iota(jnp.int32, (c, LANES), 1),
    }
    dtb = dtb_ref[...]
    a_neg = -jnp.exp(alog_ref[...]) * float(np.log2(np.e))
    cb = cb_ref[...]
    halo = SUBLANES

    def conv_body(ci, carry):
        r0 = pl.multiple_of(ci * c, c)
        cur = xbc_ref[0, pl.ds(r0, c), :]
        first = jnp.logical_or(ci == 0, ci == nl)
        last = jnp.logical_or(ci == nl - 1, ci == nc - 1)
        rp = pl.multiple_of(jnp.maximum(r0 - halo, 0), halo)
        rn = pl.multiple_of(jnp.minimum(r0 + c, n_all - halo), halo)
        prev = xbc_ref[0, pl.ds(rp, halo), :] * jnp.where(first, 0.0, 1.0)
        nxt = xbc_ref[0, pl.ds(rn, halo), :] * jnp.where(last, 0.0, 1.0)
        ext = jnp.concatenate([prev, cur, nxt], axis=0)
        acc = jnp.broadcast_to(cb, (c, 2 * LANES))
        for j in range(SSM_CONV):
            s = (SSM_CONV // 2 - j) % (c + 2 * halo)
            sh = ext if s == 0 else pltpu.roll(ext, s, 0)
            acc = acc + cw_ref[j:j + 1, :] * sh[halo:halo + c, :]
        xa_ref[pl.ds(r0, c), :] = _silu(acc)
        return carry

    lax.fori_loop(0, nc, conv_body, 0)

    nb = SSM_BATCH
    assert nc % nb == 0

    def intra(i, carry):
        cis = [i * nb + u for u in range(nb)]
        rws = [pl.ds(pl.multiple_of(ci * c, c), c) for ci in cis]
        res = _ssd_intra([(xa_ref[rows, :], dt_ref[0, rows, :]) for rows in rws], cst=cst, dtb=dtb, a_neg=a_neg)
        for ci, rows, (y, cm, outs) in zip(cis, rws, res):
            ya_ref[rows, :] = y
            cm_ref[rows, :] = cm
            for (dec_out, upd, dec), do_ref, upd_ref, dec_ref in zip(outs, (dof_ref, dob_ref), (updf_ref, updb_ref), (decf_ref, decb_ref)):
                do_ref[rows, :] = dec_out
                upd_ref[ci] = upd
                dec_ref[ci] = jnp.broadcast_to(dec, (SUBLANES, LANES))
        return carry

    lax.fori_loop(0, nc // nb, intra, 0)

    def scan(i, carry):
        st_f, st_b = carry
        jf = jnp.where(i < ncc, nl + i, i - ncc)
        rows = pl.ds(pl.multiple_of(jf * c, c), c)
        ya_ref[rows, :] = ya_ref[rows, :] + _dot(cm_ref[rows, :], st_f.astype(BF16)) * dof_ref[rows, :]
        st_f = st_f * decf_ref[jf][0:1, :] + updf_ref[jf]
        ib = nc - 1 - i
        rows = pl.ds(pl.multiple_of(ib * c, c), c)
        ya_ref[rows, :] = ya_ref[rows, :] + _dot(cm_ref[rows, :], st_b.astype(BF16)) * dob_ref[rows, :]
        st_b = st_b * decb_ref[ib][0:1, :] + updb_ref[ib]
        return st_f, st_b

    zst = jnp.zeros((LANES, LANES), F32)
    lax.fori_loop(0, nc, scan, (zst, zst), unroll=2)

    dsk = dsk_ref[...]
    ng = ng_ref[...]
    tr = ROW_TILE

    def readout(i, carry):
        rows = pl.ds(pl.multiple_of(i * tr, tr), tr)
        y = ya_ref[rows, :] + dsk * xa_ref[rows, 0:LANES]
        yz = y * _silu(z_ref[0, rows, :])
        y_ref[0, rows, :] = _rms(yz, ng).astype(y_ref.dtype)
        return carry

    lax.fori_loop(0, n_all // tr, readout, 0)


def _ssd_call(p_ssm, conv_w, conv_b, dtb, alog, dsk, ng, n_lat):
    b, n_all, _ = p_ssm.shape
    assert SSM_CHUNK == LANES
    nc = n_all // SSM_CHUNK
    vec = pl.BlockSpec((1, LANES), lambda bb, g: (0, g))
    vec2 = pl.BlockSpec((1, 2 * LANES), lambda bb, g: (0, g))
    scratch = [pltpu.VMEM((n_all, 2 * LANES), F32), pltpu.VMEM((n_all, LANES), F32), pltpu.VMEM((n_all, LANES), BF16),
               pltpu.VMEM((n_all, LANES), F32), pltpu.VMEM((n_all, LANES), F32),
               pltpu.VMEM((nc, LANES, LANES), F32), pltpu.VMEM((nc, LANES, LANES), F32),
               pltpu.VMEM((nc, SUBLANES, LANES), F32), pltpu.VMEM((nc, SUBLANES, LANES), F32)]
    est = 2 * (n_all * 5 * LANES * 4 + n_all * LANES * 2) + n_all * 6 * LANES * 4 + 2 * nc * LANES * LANES * 4 + 8 * 1024 * 1024
    return pl.pallas_call(
        functools.partial(_ssd_kernel, n_lat=n_lat),
        grid=(b, SSM_GROUPS),
        in_specs=[
            pl.BlockSpec((1, n_all, LANES), lambda bb, g: (bb, 0, g)),
            pl.BlockSpec((1, n_all, 2 * LANES), lambda bb, g: (bb, 0, 1 + g)),
            pl.BlockSpec((1, n_all, 2 * LANES), lambda bb, g: (bb, 0, 3 + g)),
            pl.BlockSpec((SSM_CONV, 2 * LANES), lambda bb, g: (0, g)),
            pl.BlockSpec((1, 2 * LANES), lambda bb, g: (0, g)),
            vec2, vec2, vec, vec,
        ],
        out_specs=pl.BlockSpec((1, n_all, LANES), lambda bb, g: (bb, 0, g)),
        out_shape=jax.ShapeDtypeStruct((b, n_all, SSM_INNER), BF16),
        scratch_shapes=scratch,
        compiler_params=pltpu.CompilerParams(vmem_limit_bytes=_vmem_limit(est)),
        name="ssd_bidir",
    )(p_ssm, p_ssm, p_ssm, conv_w, conv_b, dtb, alog, dsk, ng)


def _ffn_kernel(x_ref, a_ref, r_ref, s_ref, mod_ref, g2_ref, gf_ref, wa_ref, wr_ref, ws_ref, w1_ref, w2_ref,
                o_ref, *, hid_chunk, final):
    d = x_ref.shape[2]
    hidden = w2_ref.shape[0]
    x = x_ref[0]
    mod = lambda j: mod_ref[0, 0, :, j * d:(j + 1) * d]
    mix = _dot(a_ref[0], wa_ref[...]) + _dot(r_ref[0], wr_ref[...]) + _dot(s_ref[0], ws_ref[...])
    x1 = x + mod(2) * mix
    h2 = (_rms(x1, g2_ref[...]) * (1.0 + mod(4)) + mod(3)).astype(BF16)
    y = None
    for c0 in range(0, hidden, hid_chunk):
        ha = _dot(h2, w1_ref[:, c0:c0 + hid_chunk])
        hb = _dot(h2, w1_ref[:, hidden + c0:hidden + c0 + hid_chunk])
        act = (_silu(ha) * hb).astype(BF16)
        yc = _dot(act, w2_ref[c0:c0 + hid_chunk, :])
        y = yc if y is None else y + yc
    x2 = x1 + mod(5) * y
    o_ref[0] = _rms(x2, gf_ref[...]) if final else x2


def _ffn_call(x_all, a, r, s, modsel, g2, gf, wa, wr, ws, w1, w2, n_lat, final):
    b, n_all, d = x_all.shape
    tm = TOKEN_TILE
    hidden = w2.shape[0]
    hid_chunk = hidden // FFN_SPLIT
    nlt = n_lat // tm
    n_out = n_lat if final else n_all
    full = lambda arr: pl.BlockSpec(arr.shape, lambda bb, i: (0,) * arr.ndim, pipeline_mode=pl.Buffered(1))
    tok = lambda w: pl.BlockSpec((1, tm, w), lambda bb, i: (bb, i, 0))
    est = (wa.size + wr.size + ws.size + w1.size + w2.size) * 2 + 2 * (2 * tm * d * 4 + tm * 1024 * 2) \
        + 5 * tm * hid_chunk * 4 + 5 * tm * d * 4
    return pl.pallas_call(
        functools.partial(_ffn_kernel, hid_chunk=hid_chunk, final=final),
        grid=(b, pl.cdiv(n_out, tm)),
        in_specs=[
            tok(d), tok(a.shape[2]), tok(r.shape[2]), tok(s.shape[2]),
            pl.BlockSpec((1, 1, 1, modsel.shape[3]), lambda bb, i: (bb, jnp.minimum(i // nlt, 1), 0, 0)),
            pl.BlockSpec((1, d), lambda bb, i: (0, 0)),
            pl.BlockSpec((1, d), lambda bb, i: (0, 0)),
            full(wa), full(wr), full(ws), full(w1), full(w2),
        ],
        out_specs=tok(d),
        out_shape=jax.ShapeDtypeStruct((b, n_out, d), F32),
        compiler_params=pltpu.CompilerParams(vmem_limit_bytes=_vmem_limit(est)),
        name="outproj_ffn",
    )(x_all, a, r, s, modsel, g2, gf, wa, wr, ws, w1, w2)


def _win_perm():
    perm = np.full((P_TOTAL,), -1, np.int64)
    perm[P_QLAT:P_QLAT + MLA_Q_RANK + MLA_KV_RANK] = np.arange(MLA_Q_RANK + MLA_KV_RANK)
    perm[P_KR + MLA_NOPE:P_KR + MLA_QK] = MLA_Q_RANK + MLA_KV_RANK + np.arange(MLA_ROPE)
    perm[P_HG:P_HG + HG_COLS] = MLA_COLS + np.arange(HG_COLS)
    o2 = MLA_COLS + HG_COLS
    perm[P_SSM:P_SSM + SSM_INNER] = o2 + np.arange(SSM_INNER)
    xo = o2 + SSM_INNER
    bo = xo + SSM_INNER
    co = bo + SSM_GROUPS * SSM_STATE
    dto = o2 + SSM_INNER + SSM_XBC
    for g in range(SSM_GROUPS):
        base = P_SSM + SSM_INNER + g * 2 * LANES
        perm[base:base + LANES] = xo + g * LANES + np.arange(LANES)
        perm[base + LANES:base + LANES + SSM_STATE] = bo + g * SSM_STATE + np.arange(SSM_STATE)
        perm[base + LANES + SSM_STATE:base + 2 * LANES] = co + g * SSM_STATE + np.arange(SSM_STATE)
        dbase = P_SSM + SSM_INNER + 2 * 2 * LANES + g * 2 * LANES
        for d in range(2):
            for j in range(2):
                lo = dbase + d * LANES + j * SSM_HEADDIM
                perm[lo:lo + SSM_HEADDIM] = dto + d * SSM_HEADS + 2 * g + j
    return perm


def _conv_perm():
    perm = np.zeros((SSM_XBC,), np.int64)
    for g in range(SSM_GROUPS):
        base = g * 2 * LANES
        perm[base:base + LANES] = g * LANES + np.arange(LANES)
        perm[base + LANES:base + LANES + SSM_STATE] = SSM_INNER + g * SSM_STATE + np.arange(SSM_STATE)
        perm[base + LANES + SSM_STATE:base + 2 * LANES] = SSM_INNER + SSM_GROUPS * SSM_STATE + g * SSM_STATE + np.arange(SSM_STATE)
    return perm


def _gather_cols(w, perm):
    idx = jnp.asarray(np.maximum(perm, 0), jnp.int32)
    out = jnp.take(w, idx, axis=-1)
    return jnp.where(jnp.asarray(perm >= 0), out, 0.0)


def _head_vec(v):
    depth = v.shape[0]
    v5 = v.reshape(depth, 2, SSM_GROUPS, 2, 1)
    v5 = jnp.broadcast_to(v5, (depth, 2, SSM_GROUPS, 2, SSM_HEADDIM))
    return jnp.transpose(v5, (0, 2, 1, 3, 4)).reshape(depth, 1, SSM_GROUPS * 2 * LANES)


def _rope_tables(n_ctx, n_lat, scale):
    rows = n_lat // GRID_W
    row = jnp.repeat(jnp.arange(rows, dtype=F32), GRID_W)
    col = jnp.tile(jnp.arange(GRID_W, dtype=F32), rows)
    n_freq = MLA_ROPE // 4
    inv = ROPE_BASE ** (-jnp.arange(n_freq, dtype=F32) / n_freq)
    ang = jnp.stack([row[:, None] * inv, col[:, None] * inv], axis=1)
    cos, sin = jnp.cos(ang), jnp.sin(ang)
    c_r = jnp.concatenate([cos[:, 0], cos[:, 0], cos[:, 1], cos[:, 1]], axis=-1)
    s_r = jnp.concatenate([-sin[:, 0], sin[:, 0], -sin[:, 1], sin[:, 1]], axis=-1)
    c_lat = jnp.concatenate([jnp.ones((n_lat, MLA_NOPE), F32), c_r, jnp.ones((n_lat, LANES - MLA_QK), F32)], axis=-1)
    s_lat = jnp.concatenate([jnp.zeros((n_lat, MLA_NOPE), F32), s_r, jnp.zeros((n_lat, LANES - MLA_QK), F32)], axis=-1)
    c_all = jnp.concatenate([c_lat, jnp.ones((n_ctx, LANES), F32)], axis=0)
    s_all = jnp.concatenate([s_lat, jnp.zeros((n_ctx, LANES), F32)], axis=0)
    return c_all * scale, s_all * scale, c_all, s_all


def kernel(x, c, ctx, c_ctx, w_ada, b_ada, norm1_g, norm2_g, w_in, mla_qa_g, mla_wqb, mla_kva_g, mla_wkvb, hg_lb_logits, hg_norm_g, ssm_conv_w, ssm_conv_b, ssm_dt_bias, ssm_a_log, ssm_d, ssm_norm_g, w_out, w_ffn_in, w_ffn_out, final_g):
    bsz, n_lat, d = x.shape
    n_ctx = ctx.shape[1]
    depth = w_ada.shape[0]
    assert n_lat % TOKEN_TILE == 0 and n_lat % GRID_W == 0 and n_ctx <= TOKEN_TILE
    assert n_ctx % (HG_BATCH * HG_CHUNK) == 0 and n_ctx % (SSM_BATCH * SSM_CHUNK) == 0 and n_ctx % ROW_TILE == 0

    rows = -(-(bsz + 1) // SUBLANES) * SUBLANES
    cc = jnp.zeros((rows, d), F32).at[:bsz].set(c).at[bsz].set(c_ctx)
    mods = _mod_call(cc, w_ada.astype(BF16), b_ada.reshape(depth, 1, 6 * d))
    modsel = jnp.stack([mods[:, :bsz], jnp.broadcast_to(mods[:, bsz:bsz + 1], (depth, bsz, 6 * d))], axis=2)[:, :, :, None, :]

    w_in_p = _gather_cols(w_in, _win_perm()).astype(BF16)
    wqb4 = mla_wqb.reshape(depth, MLA_Q_RANK, MLA_HEADS, MLA_QK)
    wqb_p = jnp.pad(wqb4, ((0, 0), (0, 0), (0, 0), (0, LANES - MLA_QK))).reshape(depth, MLA_Q_RANK, MLA_HEADS * LANES).astype(BF16)
    wkvb_b = mla_wkvb.astype(BF16)
    cperm = _conv_perm()
    conv_w_p = jnp.take(ssm_conv_w, jnp.asarray(cperm, jnp.int32), axis=-1)
    conv_b_p = jnp.take(ssm_conv_b, jnp.asarray(cperm, jnp.int32), axis=-1).reshape(depth, 1, SSM_XBC)
    dtb_p = _head_vec(ssm_dt_bias)
    alog_p = _head_vec(ssm_a_log)
    dsk_p = jnp.repeat(ssm_d, SSM_HEADDIM, axis=-1).reshape(depth, 1, SSM_INNER)
    lb_soft = jax.nn.softmax(hg_lb_logits.astype(F32), axis=0)
    lb_all = jnp.cumsum(lb_soft, axis=0) - lb_soft[0]
    w_out_b = w_out.astype(BF16)
    a_w = MLA_HEADS * MLA_V
    r_w = HG_HEADS * HG_DV
    w1_b = w_ffn_in.astype(BF16)
    w2_b = w_ffn_out.astype(BF16)
    tabs = _rope_tables(n_ctx, n_lat, MLA_QK ** -0.5 * float(np.log2(np.e)))

    x_all = jnp.concatenate([x, ctx], axis=1)
    for l in range(depth):
        q, k, v, p_hg, p_ssm = _inproj_call(
            x_all, modsel[l], norm1_g[l].reshape(1, d), w_in_p[l], mla_qa_g[l].reshape(1, -1),
            mla_kva_g[l].reshape(1, -1), wqb_p[l], wkvb_b[l], tabs, n_lat)
        a = _attn_call(q, k, v, n_lat)
        r = _hgrn_call(p_hg, lb_all[l], hg_norm_g[l].reshape(1, -1), n_lat)
        s = _ssd_call(p_ssm, conv_w_p[l], conv_b_p[l], dtb_p[l], alog_p[l], dsk_p[l],
                      ssm_norm_g[l].reshape(1, -1), n_lat)
        x_all = _ffn_call(x_all, a, r, s, modsel[l], norm2_g[l].reshape(1, d), final_g.reshape(1, d),
                          w_out_b[l, :a_w], w_out_b[l, a_w:a_w + r_w], w_out_b[l, a_w + r_w:], w1_b[l], w2_b[l],
                          n_lat, final=(l == depth - 1))
    return x_all
```

```python
import functools

import numpy as np
import jax
import jax.numpy as jnp
from jax import lax
from jax.experimental import pallas as pl
from jax.experimental.pallas import tpu as pltpu

F32 = jnp.float32
BF16 = jnp.bfloat16
EPS = 1e-6

LANES = 128
SUBLANES = 8
MXU_TILE = 256
VMEM_BYTES = 64 * 1024 * 1024

GRID_W = 64
MLA_HEADS = 8
MLA_Q_RANK = 384
MLA_KV_RANK = 256
MLA_NOPE = 64
MLA_ROPE = 32
MLA_V = 64
MLA_QK = MLA_NOPE + MLA_ROPE
ROPE_BASE = 10000.0
HG_HEADS = 4
HG_DK = 64
HG_DV = 64
HG_W = HG_HEADS * HG_DK
SSM_HEADS = 4
SSM_HEADDIM = 64
SSM_GROUPS = 2
SSM_STATE = 64
SSM_CONV = 5
SSM_INNER = SSM_HEADS * SSM_HEADDIM
SSM_XBC = SSM_INNER + 2 * SSM_GROUPS * SSM_STATE

MLA_COLS = MLA_Q_RANK + MLA_KV_RANK + MLA_ROPE
HG_COLS = 3 * HG_W + 2 * HG_HEADS * HG_DV
SSM_COLS = SSM_INNER + SSM_XBC + 2 * SSM_HEADS

P_QLAT = 0
P_KVLAT = MLA_Q_RANK
P_KR = P_KVLAT + MLA_KV_RANK
P_HG = P_KR + LANES
P_SSM = P_HG + HG_COLS
SSM_BLOCK = SSM_INNER + 2 * 2 * LANES + 2 * 2 * LANES
P_TOTAL = P_SSM + SSM_BLOCK

TOKEN_TILE = 512
ROW_TILE = 256
HG_CHUNK = 128
HG_BATCH = 2
SSM_CHUNK = 128
SSM_BATCH = 2
FFN_SPLIT = 2

NT_DIMS = (((1,), (1,)), ((), ()))
TN_DIMS = (((0,), (0,)), ((), ()))


def _vmem_limit(nbytes):
    return int(min(VMEM_BYTES - 8 * 1024 * 1024, max(nbytes, 16 * 1024 * 1024)))


def _sigmoid_pair(x):
    e = jnp.exp(-jnp.abs(x))
    d = 1.0 / (1.0 + e)
    ed = e * d
    pos = x >= 0
    return jnp.where(pos, d, ed), jnp.where(pos, ed, d)


def _silu(x):
    return x / (1.0 + jnp.exp(-x))


def _rms(x, g):
    ms = jnp.mean(x * x, axis=-1, keepdims=True)
    return x * lax.rsqrt(ms + EPS) * g


def _dot(a, b):
    return jnp.dot(a, b, preferred_element_type=F32)


def _dot_nt(a, b):
    return lax.dot_general(a, b, NT_DIMS, preferred_element_type=F32)


def _dot_tn(a, b):
    return lax.dot_general(a, b, TN_DIMS, preferred_element_type=F32)


def _split3(x):
    x1 = x.astype(BF16)
    r1 = x - x1.astype(F32)
    x2 = r1.astype(BF16)
    x3 = (r1 - x2.astype(F32)).astype(BF16)
    return x1, x2, x3


def _cumsum_mm(tri, x):
    x1, x2, x3 = _split3(x)
    return _dot(tri, x1) + _dot(tri, x2) + _dot(tri, x3)


def _mod_kernel(c_ref, w_ref, b_ref, o_ref):
    s = _silu(c_ref[...]).astype(BF16)
    o_ref[0] = _dot(s, w_ref[0]) + b_ref[0]


def _mod_call(cc, w_ada, b_ada):
    depth, d, six_d = w_ada.shape
    rows = cc.shape[0]
    tn = 1536
    return pl.pallas_call(
        _mod_kernel,
        grid=(depth, six_d // tn),
        in_specs=[
            pl.BlockSpec((rows, d), lambda l, j: (0, 0)),
            pl.BlockSpec((1, d, tn), lambda l, j: (l, 0, j)),
            pl.BlockSpec((1, 1, tn), lambda l, j: (l, 0, j)),
        ],
        out_specs=pl.BlockSpec((1, rows, tn), lambda l, j: (l, 0, j)),
        out_shape=jax.ShapeDtypeStruct((depth, rows, six_d), F32),
        name="adaln_mod",
    )(cc, w_ada, b_ada)


def _rope(xh, c, s, half0):
    sw = jnp.where(half0, pltpu.roll(xh, LANES - 8, 1), pltpu.roll(xh, 8, 1))
    return xh * c + sw * s


def _inproj_kernel(x_ref, mod_ref, g1_ref, win_ref, qag_ref, kvag_ref, wqb_ref, wkvb_ref,
                   cq_ref, sq_ref, ck_ref, sk_ref,
                   q_ref, k_ref, v_ref, hg_ref, ssm_ref):
    d = x_ref.shape[2]
    tm = x_ref.shape[1]
    x = x_ref[0]
    sh = mod_ref[0, 0, :, 0:d]
    sc = mod_ref[0, 0, :, d:2 * d]
    h = _rms(x, g1_ref[...]) * (1.0 + sc) + sh
    p = _dot(h.astype(BF16), win_ref[...])
    hg_ref[0] = p[:, P_HG:P_SSM]
    ssm_ref[0] = p[:, P_SSM:P_TOTAL]
    q_lat = _rms(p[:, P_QLAT:P_KVLAT], qag_ref[...])
    kv_lat = _rms(p[:, P_KVLAT:P_KR], kvag_ref[...])
    q = _dot(q_lat.astype(BF16), wqb_ref[...])
    kv = _dot(kv_lat.astype(BF16), wkvb_ref[...])
    lane = lax.broadcasted_iota(jnp.int32, (tm, LANES), 1)
    half0 = jnp.bitwise_and(lane, 15) < 8
    cq, sq, ck, sk = cq_ref[...], sq_ref[...], ck_ref[...], sk_ref[...]
    kr = _rope(p[:, P_KR:P_HG], ck, sk, half0)
    nope = lane < MLA_NOPE
    for hh in range(MLA_HEADS):
        sl = slice(hh * LANES, (hh + 1) * LANES)
        q_ref[0, hh] = _rope(q[:, sl], cq, sq, half0).astype(q_ref.dtype)
        kvh = kv[:, sl]
        k_ref[0, hh] = jnp.where(nope, kvh, kr).astype(k_ref.dtype)
        v_ref[0, hh] = jnp.where(nope, 1.0, kvh).astype(v_ref.dtype)


def _inproj_call(x_all, modsel, g1, w_in, qa_g, kva_g, wqb, wkvb, tabs, n_lat):
    b, n_all, d = x_all.shape
    tm = TOKEN_TILE
    nt = pl.cdiv(n_all, tm)
    nlt = n_lat // tm
    full = lambda shape: pl.BlockSpec(shape, lambda bb, i: (0,) * len(shape), pipeline_mode=pl.Buffered(1))
    tab = pl.BlockSpec((tm, LANES), lambda bb, i: (i, 0))
    head_out = pl.BlockSpec((1, MLA_HEADS, tm, LANES), lambda bb, i: (bb, 0, i, 0))
    est = (w_in.size + wqb.size + wkvb.size) * 2 + 2 * (tm * d * 4 + 3 * MLA_HEADS * tm * LANES * 2
                                                        + tm * (HG_COLS + SSM_BLOCK) * 4) + 4 * tm * P_TOTAL * 4
    return pl.pallas_call(
        _inproj_kernel,
        grid=(b, nt),
        in_specs=[
            pl.BlockSpec((1, tm, d), lambda bb, i: (bb, i, 0)),
            pl.BlockSpec((1, 1, 1, modsel.shape[3]), lambda bb, i: (bb, jnp.minimum(i // nlt, 1), 0, 0)),
            full((1, d)),
            full(w_in.shape),
            full((1, MLA_Q_RANK)),
            full((1, MLA_KV_RANK)),
            full(wqb.shape),
            full(wkvb.shape),
            tab, tab, tab, tab,
        ],
        out_specs=[
            head_out, head_out, head_out,
            pl.BlockSpec((1, tm, HG_COLS), lambda bb, i: (bb, i, 0)),
            pl.BlockSpec((1, tm, SSM_BLOCK), lambda bb, i: (bb, i, 0)),
        ],
        out_shape=[
            jax.ShapeDtypeStruct((b, MLA_HEADS, n_all, LANES), BF16),
            jax.ShapeDtypeStruct((b, MLA_HEADS, n_all, LANES), BF16),
            jax.ShapeDtypeStruct((b, MLA_HEADS, n_all, LANES), BF16),
            jax.ShapeDtypeStruct((b, n_all, HG_COLS), F32),
            jax.ShapeDtypeStruct((b, n_all, SSM_BLOCK), F32),
        ],
        compiler_params=pltpu.CompilerParams(vmem_limit_bytes=_vmem_limit(est)),
        name="inproj_mla",
    )(x_all, modsel, g1, w_in, qa_g, kva_g, wqb, wkvb, *tabs)


def _attn_kernel(q_ref, k_ref, v_ref, o_ref, *, n_lat):
    n_all = k_ref.shape[2]
    tq = q_ref.shape[2]

    def run(k0, k1):
        ss = [_dot_nt(q_ref[0, j], k_ref[0, j, k0:k1, :]) for j in range(2)]
        ps = [jnp.exp2(s - jnp.max(s, axis=-1, keepdims=True)).astype(BF16) for s in ss]
        outs = []
        for j in range(2):
            o = _dot(ps[j], v_ref[0, j, k0:k1, :])
            outs.append(o / pltpu.roll(o, MLA_V, 1))
        lane = lax.broadcasted_iota(jnp.int32, (tq, LANES), 1)
        o_ref[0] = jnp.where(lane < MLA_V, pltpu.roll(outs[0], MLA_V, 1), outs[1]).astype(o_ref.dtype)

    is_ctx = pl.program_id(2) >= n_lat // tq

    @pl.when(is_ctx)
    def _():
        run(n_lat, n_all)

    @pl.when(jnp.logical_not(is_ctx))
    def _():
        run(0, n_all)


def _attn_call(q, k, v, n_lat):
    b, h, n_all, _ = q.shape
    tq = TOKEN_TILE
    kv_spec = pl.BlockSpec((1, 2, n_all, LANES), lambda bb, hp, i: (bb, hp, 0, 0))
    est = 2 * (2 * tq * LANES * 2 + 2 * 2 * n_all * LANES * 2 + tq * LANES * 2) + 7 * tq * n_all * 4
    return pl.pallas_call(
        functools.partial(_attn_kernel, n_lat=n_lat),
        grid=(b, h // 2, pl.cdiv(n_all, tq)),
        in_specs=[pl.BlockSpec((1, 2, tq, LANES), lambda bb, hp, i: (bb, hp, i, 0)), kv_spec, kv_spec],
        out_specs=pl.BlockSpec((1, tq, LANES), lambda bb, hp, i: (bb, i, hp)),
        out_shape=jax.ShapeDtypeStruct((b, n_all, h * MLA_V), BF16),
        compiler_params=pltpu.CompilerParams(vmem_limit_bytes=_vmem_limit(est)),
        name="mla_attention",
    )(q, k, v)


def _hg_consts(c):
    ri = lax.broadcasted_iota(jnp.int32, (c, c), 0)
    ci = lax.broadcasted_iota(jnp.int32, (c, c), 1)
    cst = {
        "tril": jnp.where(ci <= ri, 1.0, 0.0).astype(BF16),
        "triu": jnp.where(ci >= ri, 1.0, 0.0).astype(BF16),
        "lvl": {},
    }
    h = 1
    while 2 * h <= c:
        same =jnp.right_shift(ri, int(np.log2(2 * h))) == jnp.right_shift(ci, int(np.log2(2 * h)))
        r_lo = jnp.bitwise_and(ri, 2 * h - 1) < h
        c_lo = jnp.bitwise_and(ci, 2 * h - 1) < h
        fwd = jnp.where(same, jnp.where(r_lo, 0.0, jnp.where(c_lo, 1.0, 0.0)), 0.0)
        bwd = jnp.where(same, jnp.where(r_lo, jnp.where(c_lo, 0.0, 1.0), 0.0), 0.0)
        cst["lvl"][(h, False)] = fwd
        cst["lvl"][(h, True)] = bwd
        h *= 2
    r2 = lax.broadcasted_iota(jnp.int32, (LANES, LANES), 0)
    c2 = lax.broadcasted_iota(jnp.int32, (LANES, LANES), 1)
    bd = (r2 < HG_DK) == (c2 < HG_DK)
    cst["bd"] = bd
    cst["bo"] = jnp.where(bd, 1.0, 0.0).astype(BF16)
    lane = lax.broadcasted_iota(jnp.int32, (c, LANES), 1)
    cst["hm"] = [lane < HG_DK, lane >= HG_DK]
    cst["rit"] = jnp.bitwise_and(lax.broadcasted_iota(jnp.int32, (c, LANES), 0), SUBLANES - 1)
    return cst


def _hg_level_ref(b, h, rev, rit):
    c = b.shape[0]
    off = h if rev else h - 1
    if h >= SUBLANES:
        pieces = [jnp.broadcast_to(b[blk * 2 * h + off:blk * 2 * h + off + 1, :], (2 * h, LANES))
                  for blk in range(c // (2 * h))]
        return pieces[0] if len(pieces) == 1 else jnp.concatenate(pieces, axis=0)
    b3 = b.reshape(c // SUBLANES, SUBLANES, LANES)
    if h == 1:
        if rev:
            return jnp.where(jnp.bitwise_and(rit, 1) == 0, pltpu.roll(b3, SUBLANES - 1, 1).reshape(c, LANES), b)
        return jnp.where(jnp.bitwise_and(rit, 1) == 1, pltpu.roll(b3, 1, 1).reshape(c, LANES), b)
    rows = [jnp.broadcast_to(b3[:, r0 + off:r0 + off + 1, :], b3.shape).reshape(c, LANES)
            for r0 in range(0, SUBLANES, 2 * h)]
    return rows[0] if len(rows) == 1 else jnp.where(rit < 2 * h, rows[0], rows[1])


def _hg_intra(streams, *, cst):
    c = streams[0][0].shape[0]
    ns = len(streams)
    bs = [_cumsum_mm(cst["triu"] if s[4] else cst["tril"], s[2]) for s in streams]
    tots = [b[0:1, :] if s[4] else b[c - 1:c, :] for b, s in zip(bs, streams)]
    qes = [(s[0] * jnp.exp2(b)).astype(BF16) for b, s in zip(bs, streams)]
    atts = [[None, None] for _ in range(ns)]
    sign = jnp.uint32(0x80000000)
    h = 1
    while 2 * h <= c:
        ops = []
        for si, (b, (q, k, g, v, rev)) in enumerate(zip(bs, streams)):
            ref = _hg_level_ref(b, h, rev, cst["rit"])
            nabs = pltpu.bitcast(pltpu.bitcast(b - ref, jnp.uint32) | sign, F32)
            e = jnp.exp2(nabs)
            qh = q * e
            ops.append(([jnp.where(cst["hm"][j], qh, 0.0).astype(BF16) for j in range(2)], (k * e).astype(BF16)))
        prods = [[_dot_nt(qj, kh) for qj in qjs] for qjs, kh in ops]
        for si in range(ns):
            msk = cst["lvl"][(h, streams[si][4])]
            for j in range(2):
                t = prods[si][j] * msk
                atts[si][j] = t if atts[si][j] is None else atts[si][j] + t
        h *= 2
    diag = [_dot((s[0] * s[1]).astype(BF16), cst["bo"]) for s in streams]
    res = []
    for si, (q, k, g, v, rev) in enumerate(streams):
        o = diag[si] * v
        for j in range(2):
            vj = jnp.where(cst["hm"][j], v, 0.0).astype(BF16)
            o = o + _dot(atts[si][j].astype(BF16), vj)
        ke = (k * jnp.exp2(tots[si] - bs[si])).astype(BF16)
        upd = jnp.where(cst["bd"], _dot_tn(v.astype(BF16), ke), 0.0)
        res.append((o, qes[si], upd, jnp.exp2(tots[si])))
    return res


def _hgrn_kernel(q_ref, fff_ref, ffb_ref, iv_ref, og_ref, lb_ref, ng_ref, r_ref,
                 o_ref, qef_ref, qeb_ref, updf_ref, updb_ref, decf_ref, decb_ref, *, n_lat):
    c = HG_CHUNK
    n_all = q_ref.shape[1]
    nc = n_all // c
    ncc = (n_all - n_lat) // c
    cst = _hg_consts(c)
    bo = cst["bo"]
    ng = ng_ref[...]
    tr = ROW_TILE
    dirs = ((False, fff_ref, lb_ref[0:1, :], qef_ref, updf_ref, decf_ref),
            (True, ffb_ref, lb_ref[1:2, :], qeb_ref, updb_ref, decb_ref))
    nb = HG_BATCH
    assert nc % nb == 0

    def intra(i, carry):
        streams, where = [], []
        for u in range(nb):
            ci = i * nb + u
            rows = pl.ds(pl.multiple_of(ci * c, c), c)
            q = _silu(q_ref[0, rows, :])
            v = iv_ref[0, rows, :]
            for rev, ff_ref, lb, qe_ref, upd_ref, dec_ref in dirs:
                sig, sigm = _sigmoid_pair(ff_ref[0, rows, :])
                f = lb + (1.0 - lb) * sig
                streams.append((q, (1.0 - lb) * sigm, jnp.log2(f), v, rev))
                where.append((ci, rows, qe_ref, upd_ref, dec_ref))
        res = _hg_intra(streams, cst=cst)
        for u in range(nb):
            o_ref[where[2 * u][1], :] = res[2 * u][0] + res[2 * u + 1][0]
        for (_, qe, upd, dec), (ci, rows, qe_ref, upd_ref, dec_ref) in zip(res, where):
            qe_ref[rows, :] = qe
            upd_ref[ci] = upd
            dec_ref[ci] = jnp.broadcast_to(dec, (SUBLANES, LANES))
        return carry

    lax.fori_loop(0, nc // nb, intra, 0)

    def scan(i, carry):
        st_f, st_b = carry
        jf = jnp.where(i < ncc, nc - ncc + i, i - ncc)
        rows = pl.ds(pl.multiple_of(jf * c, c), c)
        o_ref[rows, :] = o_ref[rows, :] + _dot_nt(qef_ref[rows, :], st_f.astype(BF16))
        st_f = st_f * decf_ref[jf][0:1, :] + updf_ref[jf]
        ib = nc - 1 - i
        rows = pl.ds(pl.multiple_of(ib * c, c), c)
        o_ref[rows, :] = o_ref[rows, :] + _dot_nt(qeb_ref[rows, :], st_b.astype(BF16))
        st_b = st_b * decb_ref[ib][0:1, :] + updb_ref[ib]
        return st_f, st_b

    zero = jnp.zeros((LANES, LANES), F32)
    lax.fori_loop(0, nc, scan, (zero, zero), unroll=2)

    def readout(i, carry):
        rows = pl.ds(pl.multiple_of(i * tr, tr), tr)
        o = o_ref[rows, :]
        ms = _dot((o * o).astype(BF16), bo) * (1.0 / HG_DV)
        r = o * lax.rsqrt(ms + EPS) * ng * _silu(og_ref[0, rows, :])
        r_ref[0, rows, :] = r.astype(r_ref.dtype)
        return carry

    lax.fori_loop(0, n_all // tr, readout, 0)


def _hgrn_call(p_hg, lb, ng, n_lat):
    b, n_all, _ = p_hg.shape
    nc = n_all // HG_CHUNK
    col = lambda j: pl.BlockSpec((1, n_all, LANES), lambda bb, pr: (bb, 0, 2 * j + pr))
    scratch = [pltpu.VMEM((n_all, LANES), F32), pltpu.VMEM((n_all, LANES), BF16), pltpu.VMEM((n_all, LANES), BF16),
               pltpu.VMEM((nc, LANES, LANES), F32), pltpu.VMEM((nc, LANES, LANES), F32),
               pltpu.VMEM((nc, SUBLANES, LANES), F32), pltpu.VMEM((nc, SUBLANES, LANES), F32)]
    est = 2 * 6 * n_all * LANES * 4 + 2 * n_all * LANES * 4 + 2 * nc * LANES * LANES * 4 + 8 * 1024 * 1024
    return pl.pallas_call(
        functools.partial(_hgrn_kernel, n_lat=n_lat),
        grid=(b, 2),
        in_specs=[col(0), col(1), col(2), col(3), col(4),
                  pl.BlockSpec((2, LANES), lambda bb, pr: (0, pr)),
                  pl.BlockSpec((1, LANES), lambda bb, pr: (0, pr))],
        out_specs=pl.BlockSpec((1, n_all, LANES), lambda bb, pr: (bb, 0, pr)),
        out_shape=jax.ShapeDtypeStruct((b, n_all, HG_HEADS * HG_DV), BF16),
        scratch_shapes=scratch,
        compiler_params=pltpu.CompilerParams(vmem_limit_bytes=_vmem_limit(est)),
        name="hgrn2_bidir",
    )(p_hg, p_hg, p_hg, p_hg, p_hg, lb, ng)


def _ssd_intra(chunks, *, cst, dtb, a_neg):
    n = chunks[0][0].shape[0]
    lo = cst["lane"] < SSM_STATE
    hi = jnp.logical_not(lo)
    xss = [xa[:, 0:LANES] for xa, _ in chunks]
    bms, cms = [], []
    for xa, _ in chunks:
        bc = xa[:, LANES:2 * LANES]
        bms.append(jnp.where(lo, bc, 0.0).astype(BF16))
        cms.append(jnp.where(lo, pltpu.roll(bc, SSM_STATE, 1), 0.0).astype(BF16))
    gmats = [_dot_nt(cm, bm) for cm, bm in zip(cms, bms)]
    pairs = [(ci, d) for ci in range(len(chunks)) for d in range(2)]
    dts, css = [], []
    for ci, d in pairs:
        sl = slice(d * LANES, (d + 1) * LANES)
        xdt = chunks[ci][1][:, sl] + dtb[:, sl]
        dt = jnp.maximum(xdt, 0.0) + jnp.log(1.0 + jnp.exp(-jnp.abs(xdt)))
        dts.append(dt)
        css.append(_cumsum_mm(cst["triu"] if d else cst["tril"], dt * a_neg[:, sl]))
    tots = [cs[0:1, :] if d else cs[n - 1:n, :] for cs, (ci, d) in zip(css, pairs)]
    csrs = [pltpu.roll(cs, SSM_HEADDIM, 1) for cs in css]
    csts = [cs.T for cs in css]
    xds = [xss[ci] * dt for dt, (ci, d) in zip(dts, pairs)]
    lhs, rhs = [], []
    for pi, (ci, d) in enumerate(pairs):
        for j in range(2):
            colb = jnp.where(lo, css[pi], csrs[pi]) if j == 0 else jnp.where(lo, csrs[pi], css[pi])
            rowb = jnp.broadcast_to(csts[pi][j * SSM_HEADDIM:j * SSM_HEADDIM + 1, :], (n, n))
            lmat = jnp.where(cst["tri_mask_b" if d else "tri_mask_f"], jnp.exp2(jnp.minimum(colb - rowb, 0.0)), 0.0)
            lhs.append((gmats[ci] * lmat).astype(BF16))
            rhs.append(jnp.where(lo if j == 0 else hi, xds[pi], 0.0).astype(BF16))
    prods = [_dot(a, b) for a, b in zip(lhs, rhs)]
    xins = [(xd * jnp.exp2(tot - cs)).astype(BF16) for xd, tot, cs in zip(xds, tots, css)]
    upds = [_dot_tn(bms[ci], xin) for xin, (ci, d) in zip(xins, pairs)]
    res = []
    for ci in range(len(chunks)):
        y = prods[4 * ci] + prods[4 * ci + 1] + prods[4 * ci + 2] + prods[4 * ci + 3]
        outs = [(jnp.exp2(css[2 * ci + d]), upds[2 * ci + d], jnp.exp2(tots[2 * ci + d])) for d in range(2)]
        res.append((y, cms[ci], outs))
    return res


def _ssd_kernel(z_ref, xbc_ref, dt_ref, cw_ref, cb_ref, dtb_ref, alog_ref, dsk_ref, ng_ref,
                y_ref, xa_ref, ya_ref, cm_ref, dof_ref, dob_ref, updf_ref, updb_ref, decf_ref, decb_ref, *, n_lat):
    c = SSM_CHUNK
    n_all = z_ref.shape[1]
    nc = n_all // c
    ncc = (n_all - n_lat) // c
    nl = nc - ncc
    ri = lax.broadcasted_iota(jnp.int32, (c, c), 0)
    ci_ = lax.broadcasted_iota(jnp.int32, (c, c), 1)
    cst = {
        "tril": jnp.where(ci_ <= ri, 1.0, 0.0).astype(BF16),
        "triu": jnp.where(ci_ >= ri, 1.0, 0.0).astype(BF16),
        "tri_mask_f": ci_ <= ri,
        "tri_mask_b": ci_ >= ri,
        "lane": lax.broadcasted_iota(jnp.int32, (c, LANES), 1),
    }
    dtb = dtb_ref[...]
    a_neg = -jnp.exp(alog_ref[...]) * float(np.log2(np.e))
    cb = cb_ref[...]
    halo = SUBLANES

    def conv_body(ci, carry):
        r0 = pl.multiple_of(ci * c, c)
        cur = xbc_ref[0, pl.ds(r0, c), :]
        first = jnp.logical_or(ci == 0, ci == nl)
        last = jnp.logical_or(ci == nl - 1, ci == nc - 1)
        rp = pl.multiple_of(jnp.maximum(r0 - halo, 0), halo)
        rn = pl.multiple_of(jnp.minimum(r0 + c, n_all - halo), halo)
        prev = xbc_ref[0, pl.ds(rp, halo), :] * jnp.where(first, 0.0, 1.0)
        nxt = xbc_ref[0, pl.ds(rn, halo), :] * jnp.where(last, 0.0, 1.0)
        ext = jnp.concatenate([prev, cur, nxt], axis=0)
        acc = jnp.broadcast_to(cb, (c, 2 * LANES))
        for j in range(SSM_CONV):
            s = (SSM_CONV // 2 - j) % (c + 2 * halo)
            sh = ext if s == 0 else pltpu.roll(ext, s, 0)
            acc = acc + cw_ref[j:j + 1, :] * sh[halo:halo + c, :]
        xa_ref[pl.ds(r0, c), :] = _silu(acc)
        return carry

    lax.fori_loop(0, nc, conv_body, 0)

    nb = SSM_BATCH
    assert nc % nb == 0

    def intra(i, carry):
        cis = [i * nb + u for u in range(nb)]
        rws = [pl.ds(pl.multiple_of(ci * c, c), c) for ci in cis]
        res = _ssd_intra([(xa_ref[rows, :], dt_ref[0, rows, :]) for rows in rws], cst=cst, dtb=dtb, a_neg=a_neg)
        for ci, rows, (y, cm, outs) in zip(cis, rws, res):
            ya_ref[rows, :] = y
            cm_ref[rows, :] = cm
            for (dec_out, upd, dec), do_ref, upd_ref, dec_ref in zip(outs, (dof_ref, dob_ref), (updf_ref, updb_ref), (decf_ref, decb_ref)):
                do_ref[rows, :] = dec_out
                upd_ref[ci] = upd
                dec_ref[ci] = jnp.broadcast_to(dec, (SUBLANES, LANES))
        return carry

    lax.fori_loop(0, nc // nb, intra, 0)

    def scan(i, carry):
        st_f, st_b = carry
        jf = jnp.where(i < ncc, nl + i, i - ncc)
        rows = pl.ds(pl.multiple_of(jf * c, c), c)
        ya_ref[rows, :] = ya_ref[rows, :] + _dot(cm_ref[rows, :], st_f.astype(BF16)) * dof_ref[rows, :]
        st_f = st_f * decf_ref[jf][0:1, :] + updf_ref[jf]
        ib = nc - 1 - i
        rows = pl.ds(pl.multiple_of(ib * c, c), c)
        ya_ref[rows, :] = ya_ref[rows, :] + _dot(cm_ref[rows, :], st_b.astype(BF16)) * dob_ref[rows, :]
        st_b = st_b * decb_ref[ib][0:1, :] + updb_ref[ib]
        return st_f, st_b

    zst = jnp.zeros((LANES, LANES), F32)
    lax.fori_loop(0, nc, scan, (zst, zst), unroll=2)

    dsk = dsk_ref[...]
    ng = ng_ref[...]
    tr = ROW_TILE

    def readout(i, carry):
        rows = pl.ds(pl.multiple_of(i * tr, tr), tr)
        y = ya_ref[rows, :] + dsk * xa_ref[rows, 0:LANES]
        yz = y * _silu(z_ref[0, rows, :])
        y_ref[0, rows, :] = _rms(yz, ng).astype(y_ref.dtype)
        return carry

    lax.fori_loop(0, n_all // tr, readout, 0)


def _ssd_call(p_ssm, conv_w, conv_b, dtb, alog, dsk, ng, n_lat):
    b, n_all, _ = p_ssm.shape
    assert SSM_CHUNK == LANES
    nc = n_all // SSM_CHUNK
    vec = pl.BlockSpec((1, LANES), lambda bb, g: (0, g))
    vec2 = pl.BlockSpec((1, 2 * LANES), lambda bb, g: (0, g))
    scratch = [pltpu.VMEM((n_all, 2 * LANES), F32), pltpu.VMEM((n_all, LANES), F32), pltpu.VMEM((n_all, LANES), BF16),
               pltpu.VMEM((n_all, LANES), F32), pltpu.VMEM((n_all, LANES), F32),
               pltpu.VMEM((nc, LANES, LANES), F32), pltpu.VMEM((nc, LANES, LANES), F32),
               pltpu.VMEM((nc, SUBLANES, LANES), F32), pltpu.VMEM((nc, SUBLANES, LANES), F32)]
    est = 2 * (n_all * 5 * LANES * 4 + n_all * LANES * 2) + n_all * 6 * LANES * 4 + 2 * nc * LANES * LANES * 4 + 8 * 1024 * 1024
    return pl.pallas_call(
        functools.partial(_ssd_kernel, n_lat=n_lat),
        grid=(b, SSM_GROUPS),
        in_specs=[
            pl.BlockSpec((1, n_all, LANES), lambda bb, g: (bb, 0, g)),
            pl.BlockSpec((1, n_all, 2 * LANES), lambda bb, g: (bb, 0, 1 + g)),
            pl.BlockSpec((1, n_all, 2 * LANES), lambda bb, g: (bb, 0, 3 + g)),
            pl.BlockSpec((SSM_CONV, 2 * LANES), lambda bb, g: (0, g)),
            pl.BlockSpec((1, 2 * LANES), lambda bb, g: (0, g)),
            vec2, vec2, vec, vec,
        ],
        out_specs=pl.BlockSpec((1, n_all, LANES), lambda bb, g: (bb, 0, g)),
        out_shape=jax.ShapeDtypeStruct((b, n_all, SSM_INNER), BF16),
        scratch_shapes=scratch,
        compiler_params=pltpu.CompilerParams(vmem_limit_bytes=_vmem_limit(est)),
        name="ssd_bidir",
    )(p_ssm, p_ssm, p_ssm, conv_w, conv_b, dtb, alog, dsk, ng)


def _ffn_kernel(x_ref, a_ref, r_ref, s_ref, mod_ref, g2_ref, gf_ref, wa_ref, wr_ref, ws_ref, w1_ref, w2_ref,
                o_ref, *, hid_cuts, final):
    d = x_ref.shape[2]
    hidden = w2_ref.shape[0]
    x = x_ref[0]
    mod = lambda j: mod_ref[0, 0, :, j * d:(j + 1) * d]
    mix = _dot(a_ref[0], wa_ref[...]) + _dot(r_ref[0], wr_ref[...]) + _dot(s_ref[0], ws_ref[...])
    x1 = x + mod(2) * mix
    h2 = (_rms(x1, g2_ref[...]) * (1.0 + mod(4)) + mod(3)).astype(BF16)
    y = None
    for c0, c1 in zip(hid_cuts[:-1], hid_cuts[1:]):
        ha = _dot(h2, w1_ref[:, c0:c1])
        hb = _dot(h2, w1_ref[:, hidden + c0:hidden + c1])
        act = (_silu(ha) * hb).astype(BF16)
        yc = _dot(act, w2_ref[c0:c1, :])
        y = yc if y is None else y + yc
    x2 = x1 + mod(5) * y
    o_ref[0] = _rms(x2, gf_ref[...]) if final else x2


def _ffn_call(x_all, a, r, s, modsel, g2, gf, wa, wr, ws, w1, w2, n_lat, final):
    b, n_all, d = x_all.shape
    tm = TOKEN_TILE
    hidden = w2.shape[0]
    n_tiles = hidden // MXU_TILE
    assert hidden % MXU_TILE == 0
    hid_cuts = tuple(MXU_TILE * ((n_tiles * j + FFN_SPLIT - 1) // FFN_SPLIT) for j in range(FFN_SPLIT + 1))
    hid_chunk = max(b1 - b0 for b0, b1 in zip(hid_cuts[:-1], hid_cuts[1:]))
    nlt = n_lat // tm
    n_out = n_lat if final else n_all
    full = lambda arr: pl.BlockSpec(arr.shape, lambda bb, i: (0,) * arr.ndim, pipeline_mode=pl.Buffered(1))
    tok = lambda w: pl.BlockSpec((1, tm, w), lambda bb, i: (bb, i, 0))
    est = (wa.size + wr.size + ws.size + w1.size + w2.size) * 2 + 2 * (2 * tm * d * 4 + tm * 1024 * 2) \
        + 5 * tm * hid_chunk * 4 + 5 * tm * d * 4
    return pl.pallas_call(
        functools.partial(_ffn_kernel, hid_cuts=hid_cuts, final=final),
        grid=(b, pl.cdiv(n_out, tm)),
        in_specs=[
            tok(d), tok(a.shape[2]), tok(r.shape[2]), tok(s.shape[2]),
            pl.BlockSpec((1, 1, 1, modsel.shape[3]), lambda bb, i: (bb, jnp.minimum(i // nlt, 1), 0, 0)),
            pl.BlockSpec((1, d), lambda bb, i: (0, 0)),
            pl.BlockSpec((1, d), lambda bb, i: (0, 0)),
            full(wa), full(wr), full(ws), full(w1), full(w2),
        ],
        out_specs=tok(d),
        out_shape=jax.ShapeDtypeStruct((b, n_out, d), F32),
        compiler_params=pltpu.CompilerParams(vmem_limit_bytes=_vmem_limit(est)),
        name="outproj_ffn",
    )(x_all, a, r, s, modsel, g2, gf, wa, wr, ws, w1, w2)


def _win_perm():
    perm = np.full((P_TOTAL,), -1, np.int64)
    perm[P_QLAT:P_QLAT + MLA_Q_RANK + MLA_KV_RANK] = np.arange(MLA_Q_RANK + MLA_KV_RANK)
    perm[P_KR + MLA_NOPE:P_KR + MLA_QK] = MLA_Q_RANK + MLA_KV_RANK + np.arange(MLA_ROPE)
    perm[P_HG:P_HG + HG_COLS] = MLA_COLS + np.arange(HG_COLS)
    o2 = MLA_COLS + HG_COLS
    perm[P_SSM:P_SSM + SSM_INNER] = o2 + np.arange(SSM_INNER)
    xo = o2 + SSM_INNER
    bo = xo + SSM_INNER
    co = bo + SSM_GROUPS * SSM_STATE
    dto = o2 + SSM_INNER + SSM_XBC
    for g in range(SSM_GROUPS):
        base = P_SSM + SSM_INNER + g * 2 * LANES
        perm[base:base + LANES] = xo + g * LANES + np.arange(LANES)
        perm[base + LANES:base + LANES + SSM_STATE] = bo + g * SSM_STATE + np.arange(SSM_STATE)
        perm[base + LANES + SSM_STATE:base + 2 * LANES] = co + g * SSM_STATE + np.arange(SSM_STATE)
        dbase = P_SSM + SSM_INNER + 2 * 2 * LANES + g * 2 * LANES
        for d in range(2):
            for j in range(2):
                lo = dbase + d * LANES + j * SSM_HEADDIM
                perm[lo:lo + SSM_HEADDIM] = dto + d * SSM_HEADS + 2 * g + j
    return perm


def _conv_perm():
    perm = np.zeros((SSM_XBC,), np.int64)
    for g in range(SSM_GROUPS):
        base = g * 2 * LANES
        perm[base:base + LANES] = g * LANES + np.arange(LANES)
        perm[base + LANES:base + LANES + SSM_STATE] = SSM_INNER + g * SSM_STATE + np.arange(SSM_STATE)
        perm[base + LANES + SSM_STATE:base + 2 * LANES] = SSM_INNER + SSM_GROUPS * SSM_STATE + g * SSM_STATE + np.arange(SSM_STATE)
    return perm


def _gather_cols(w, perm):
    idx = jnp.asarray(np.maximum(perm, 0), jnp.int32)
    out = jnp.take(w, idx, axis=-1)
    return jnp.where(jnp.asarray(perm >= 0), out, 0.0)


def _head_vec(v):
    depth = v.shape[0]
    v5 = v.reshape(depth, 2, SSM_GROUPS, 2, 1)
    v5 = jnp.broadcast_to(v5, (depth, 2, SSM_GROUPS, 2, SSM_HEADDIM))
    return jnp.transpose(v5, (0, 2, 1, 3, 4)).reshape(depth, 1, SSM_GROUPS * 2 * LANES)


def _rope_tables(n_ctx, n_lat, scale):
    rows = n_lat // GRID_W
    row = jnp.repeat(jnp.arange(rows, dtype=F32), GRID_W)
    col = jnp.tile(jnp.arange(GRID_W, dtype=F32), rows)
    n_freq = MLA_ROPE // 4
    inv = ROPE_BASE ** (-jnp.arange(n_freq, dtype=F32) / n_freq)
    ang = jnp.stack([row[:, None] * inv, col[:, None] * inv], axis=1)
    cos, sin = jnp.cos(ang), jnp.sin(ang)
    c_r = jnp.concatenate([cos[:, 0], cos[:, 0], cos[:, 1], cos[:, 1]], axis=-1)
    s_r = jnp.concatenate([-sin[:, 0], sin[:, 0], -sin[:, 1], sin[:, 1]], axis=-1)
    c_lat = jnp.concatenate([jnp.ones((n_lat, MLA_NOPE), F32), c_r, jnp.ones((n_lat, LANES - MLA_QK), F32)], axis=-1)
    s_lat = jnp.concatenate([jnp.zeros((n_lat, MLA_NOPE), F32), s_r, jnp.zeros((n_lat, LANES - MLA_QK), F32)], axis=-1)
    c_all = jnp.concatenate([c_lat, jnp.ones((n_ctx, LANES), F32)], axis=0)
    s_all = jnp.concatenate([s_lat, jnp.zeros((n_ctx, LANES), F32)], axis=0)
    return c_all * scale, s_all * scale, c_all, s_all


def kernel(x, c, ctx, c_ctx, w_ada, b_ada, norm1_g, norm2_g, w_in, mla_qa_g, mla_wqb, mla_kva_g, mla_wkvb, hg_lb_logits, hg_norm_g, ssm_conv_w, ssm_conv_b, ssm_dt_bias, ssm_a_log, ssm_d, ssm_norm_g, w_out, w_ffn_in, w_ffn_out, final_g):
    bsz, n_lat, d = x.shape
    n_ctx = ctx.shape[1]
    depth = w_ada.shape[0]
    assert n_lat % TOKEN_TILE == 0 and n_lat % GRID_W == 0 and n_ctx <= TOKEN_TILE
    assert n_ctx % (HG_BATCH * HG_CHUNK) == 0 and n_ctx % (SSM_BATCH * SSM_CHUNK) == 0 and n_ctx % ROW_TILE == 0

    rows = -(-(bsz + 1) // SUBLANES) * SUBLANES
    cc = jnp.zeros((rows, d), F32).at[:bsz].set(c).at[bsz].set(c_ctx)
    mods = _mod_call(cc, w_ada.astype(BF16), b_ada.reshape(depth, 1, 6 * d))
    modsel = jnp.stack([mods[:, :bsz], jnp.broadcast_to(mods[:, bsz:bsz + 1], (depth, bsz, 6 * d))], axis=2)[:, :, :, None, :]

    w_in_p = _gather_cols(w_in, _win_perm()).astype(BF16)
    wqb4 = mla_wqb.reshape(depth, MLA_Q_RANK, MLA_HEADS, MLA_QK)
    wqb_p = jnp.pad(wqb4, ((0, 0), (0, 0), (0, 0), (0, LANES - MLA_QK))).reshape(depth, MLA_Q_RANK, MLA_HEADS * LANES).astype(BF16)
    wkvb_b = mla_wkvb.astype(BF16)
    cperm = _conv_perm()
    conv_w_p = jnp.take(ssm_conv_w, jnp.asarray(cperm, jnp.int32), axis=-1)
    conv_b_p = jnp.take(ssm_conv_b, jnp.asarray(cperm, jnp.int32), axis=-1).reshape(depth, 1, SSM_XBC)
    dtb_p = _head_vec(ssm_dt_bias)
    alog_p = _head_vec(ssm_a_log)
    dsk_p = jnp.repeat(ssm_d, SSM_HEADDIM, axis=-1).reshape(depth, 1, SSM_INNER)
    lb_soft = jax.nn.softmax(hg_lb_logits.astype(F32), axis=0)
    lb_all = jnp.cumsum(lb_soft, axis=0) - lb_soft[0]
    w_out_b = w_out.astype(BF16)
    a_w = MLA_HEADS * MLA_V
    r_w = HG_HEADS * HG_DV
    w1_b = w_ffn_in.astype(BF16)
    w2_b = w_ffn_out.astype(BF16)
    tabs = _rope_tables(n_ctx, n_lat, MLA_QK ** -0.5 * float(np.log2(np.e)))

    x_all = jnp.concatenate([x, ctx], axis=1)
    for l in range(depth):
        q, k, v, p_hg, p_ssm = _inproj_call(
            x_all, modsel[l], norm1_g[l].reshape(1, d), w_in_p[l], mla_qa_g[l].reshape(1, -1),
            mla_kva_g[l].reshape(1, -1), wqb_p[l], wkvb_b[l], tabs, n_lat)
        a = _attn_call(q, k, v, n_lat)
        r = _hgrn_call(p_hg, lb_all[l], hg_norm_g[l].reshape(1, -1), n_lat)
        s = _ssd_call(p_ssm, conv_w_p[l], conv_b_p[l], dtb_p[l], alog_p[l], dsk_p[l],
                      ssm_norm_g[l].reshape(1, -1), n_lat)
        x_all = _ffn_call(x_all, a, r, s, modsel[l], norm2_g[l].reshape(1, d), final_g.reshape(1, d),
                          w_out_b[l, :a_w], w_out_b[l, a_w:a_w + r_w], w_out_b[l, a_w + r_w:], w1_b[l], w2_b[l],
                          n_lat, final=(l == depth - 1))
    return x_all
```

```python
import functools

import numpy as np
import jax
import jax.numpy as jnp
from jax import lax
from jax.experimental import pallas as pl
from jax.experimental.pallas import tpu as pltpu

F32 = jnp.float32
BF16 = jnp.bfloat16
EPS = 1e-6

LANES = 128
SUBLANES = 8
MXU_TILE = 256
VMEM_BYTES = 64 * 1024 * 1024

GRID_W = 64
MLA_HEADS = 8
MLA_Q_RANK = 384
MLA_KV_RANK = 256
MLA_NOPE = 64
MLA_ROPE = 32
MLA_V = 64
MLA_QK = MLA_NOPE + MLA_ROPE
ROPE_BASE = 10000.0
HG_HEADS = 4
HG_DK = 64
HG_DV = 64
HG_W = HG_HEADS * HG_DK
SSM_HEADS = 4
SSM_HEADDIM = 64
SSM_GROUPS = 2
SSM_STATE = 64
SSM_CONV = 5
SSM_INNER = SSM_HEADS * SSM_HEADDIM
SSM_XBC = SSM_INNER + 2 * SSM_GROUPS * SSM_STATE

MLA_COLS = MLA_Q_RANK + MLA_KV_RANK + MLA_ROPE
HG_COLS = 3 * HG_W + 2 * HG_HEADS * HG_DV
SSM_COLS = SSM_INNER + SSM_XBC + 2 * SSM_HEADS

P_QLAT = 0
P_KVLAT = MLA_Q_RANK
P_KR = P_KVLAT + MLA_KV_RANK
P_HG = P_KR + LANES
P_SSM = P_HG + HG_COLS
SSM_BLOCK = SSM_INNER + 2 * 2 * LANES + 2 * 2 * LANES
P_TOTAL = P_SSM + SSM_BLOCK

TOKEN_TILE = 512
ROW_TILE = 256
HG_CHUNK = 128
HG_BATCH = 2
SSM_CHUNK = 128
SSM_BATCH = 4
FFN_SPLIT = 2

NT_DIMS = (((1,), (1,)), ((), ()))
TN_DIMS = (((0,), (0,)), ((), ()))


def _vmem_limit(nbytes):
    return int(min(VMEM_BYTES - 8 * 1024 * 1024, max(nbytes, 16 * 1024 * 1024)))


def _sigmoid_pair(x):
    e = jnp.exp(-jnp.abs(x))
    d = 1.0 / (1.0 + e)
    ed = e * d
    pos = x >= 0
    return jnp.where(pos, d, ed), jnp.where(pos, ed, d)


def _silu(x):
    return x / (1.0 + jnp.exp(-x))


def _rms(x, g):
    ms = jnp.mean(x * x, axis=-1, keepdims=True)
    return x * lax.rsqrt(ms + EPS) * g


def _dot(a, b):
    return jnp.dot(a, b, preferred_element_type=F32)


def _dot_nt(a, b):
    return lax.dot_general(a, b, NT_DIMS, preferred_element_type=F32)


def _dot_tn(a, b):
    return lax.dot_general(a, b, TN_DIMS, preferred_element_type=F32)


def _split3(x):
    x1 = x.astype(BF16)
    r1 = x - x1.astype(F32)
    x2 = r1.astype(BF16)
    x3 = (r1 - x2.astype(F32)).astype(BF16)
    return x1, x2, x3


def _cumsum_mm(tri, x):
    x1, x2, x3 = _split3(x)
    return _dot(tri, x1) + _dot(tri, x2) + _dot(tri, x3)


def _mod_kernel(c_ref, w_ref, b_ref, o_ref):
    s = _silu(c_ref[...]).astype(BF16)
    o_ref[0] = _dot(s, w_ref[0]) + b_ref[0]


def _mod_call(cc, w_ada, b_ada):
    depth, d, six_d = w_ada.shape
    rows = cc.shape[0]
    tn = 1536
    return pl.pallas_call(
        _mod_kernel,
        grid=(depth, six_d // tn),
        in_specs=[
            pl.BlockSpec((rows, d), lambda l, j: (0, 0)),
            pl.BlockSpec((1, d, tn), lambda l, j: (l, 0, j)),
            pl.BlockSpec((1, 1, tn), lambda l, j: (l, 0, j)),
        ],
        out_specs=pl.BlockSpec((1, rows, tn), lambda l, j: (l, 0, j)),
        out_shape=jax.ShapeDtypeStruct((depth, rows, six_d), F32),
        name="adaln_mod",
    )(cc, w_ada, b_ada)


def _rope(xh, c, s, half0):
    sw = jnp.where(half0, pltpu.roll(xh, LANES - 8, 1), pltpu.roll(xh, 8, 1))
    return xh * c + sw * s


def _inproj_kernel(x_ref, mod_ref, g1_ref, win_ref, qag_ref, kvag_ref, wqb_ref, wkvb_ref,
                   cq_ref, sq_ref, ck_ref, sk_ref,
                   q_ref, k_ref, v_ref, hg_ref, ssm_ref):
    d = x_ref.shape[2]
    tm = x_ref.shape[1]
    x = x_ref[0]
    sh = mod_ref[0, 0, :, 0:d]
    sc = mod_ref[0, 0, :, d:2 * d]
    h = _rms(x, g1_ref[...]) * (1.0 + sc) + sh
    p = _dot(h.astype(BF16), win_ref[...])
    hg_ref[0] = p[:, P_HG:P_SSM]
    ssm_ref[0] = p[:, P_SSM:P_TOTAL]
    q_lat = _rms(p[:, P_QLAT:P_KVLAT], qag_ref[...])
    kv_lat = _rms(p[:, P_KVLAT:P_KR], kvag_ref[...])
    q = _dot(q_lat.astype(BF16), wqb_ref[...])
    kv = _dot(kv_lat.astype(BF16), wkvb_ref[...])
    lane = lax.broadcasted_iota(jnp.int32, (tm, LANES), 1)
    half0 = jnp.bitwise_and(lane, 15) < 8
    cq, sq, ck, sk = cq_ref[...], sq_ref[...], ck_ref[...], sk_ref[...]
    kr = _rope(p[:, P_KR:P_HG], ck, sk, half0)
    nope = lane < MLA_NOPE
    for hh in range(MLA_HEADS):
        sl = slice(hh * LANES, (hh + 1) * LANES)
        q_ref[0, hh] = _rope(q[:, sl], cq, sq, half0).astype(q_ref.dtype)
        kvh = kv[:, sl]
        k_ref[0, hh] = jnp.where(nope, kvh, kr).astype(k_ref.dtype)
        v_ref[0, hh] = jnp.where(nope, 1.0, kvh).astype(v_ref.dtype)


def _inproj_call(x_all, modsel, g1, w_in, qa_g, kva_g, wqb, wkvb, tabs, n_lat):
    b, n_all, d = x_all.shape
    tm = TOKEN_TILE
    nt = pl.cdiv(n_all, tm)
    nlt = n_lat // tm
    full = lambda shape: pl.BlockSpec(shape, lambda bb, i: (0,) * len(shape), pipeline_mode=pl.Buffered(1))
    tab = pl.BlockSpec((tm, LANES), lambda bb, i: (i, 0))
    head_out = pl.BlockSpec((1, MLA_HEADS, tm, LANES), lambda bb, i: (bb, 0, i, 0))
    est = (w_in.size + wqb.size + wkvb.size) * 2 + 2 * (tm * d * 4 + 3 * MLA_HEADS * tm * LANES * 2
                                                        + tm * (HG_COLS + SSM_BLOCK) * 4) + 4 * tm * P_TOTAL * 4
    return pl.pallas_call(
        _inproj_kernel,
        grid=(b, nt),
        in_specs=[
            pl.BlockSpec((1, tm, d), lambda bb, i: (bb, i, 0)),
            pl.BlockSpec((1, 1, 1, modsel.shape[3]), lambda bb, i: (bb, jnp.minimum(i // nlt, 1), 0, 0)),
            full((1, d)),
            full(w_in.shape),
            full((1, MLA_Q_RANK)),
            full((1, MLA_KV_RANK)),
            full(wqb.shape),
            full(wkvb.shape),
            tab, tab, tab, tab,
        ],
        out_specs=[
            head_out, head_out, head_out,
            pl.BlockSpec((1, tm, HG_COLS), lambda bb, i: (bb, i, 0)),
            pl.BlockSpec((1, tm, SSM_BLOCK), lambda bb, i: (bb, i, 0)),
        ],
        out_shape=[
            jax.ShapeDtypeStruct((b, MLA_HEADS, n_all, LANES), BF16),
            jax.ShapeDtypeStruct((b, MLA_HEADS, n_all, LANES), BF16),
            jax.ShapeDtypeStruct((b, MLA_HEADS, n_all, LANES), BF16),
            jax.ShapeDtypeStruct((b, n_all, HG_COLS), F32),
            jax.ShapeDtypeStruct((b, n_all, SSM_BLOCK), F32),
        ],
        compiler_params=pltpu.CompilerParams(vmem_limit_bytes=_vmem_limit(est)),
        name="inproj_mla",
    )(x_all, modsel, g1, w_in, qa_g, kva_g, wqb, wkvb, *tabs)


def _attn_kernel(q_ref, k_ref, v_ref, o_ref, *, n_lat):
    n_all = k_ref.shape[2]
    tq = q_ref.shape[2]

    def run(k0, k1):
        ss = [_dot_nt(q_ref[0, j], k_ref[0, j, k0:k1, :]) for j in range(2)]
        ps = [jnp.exp2(s - jnp.max(s, axis=-1, keepdims=True)).astype(BF16) for s in ss]
        outs = []
        for j in range(2):
            o = _dot(ps[j], v_ref[0, j, k0:k1, :])
            outs.append(o / pltpu.roll(o, MLA_V, 1))
        lane = lax.broadcasted_iota(jnp.int32, (tq, LANES), 1)
        o_ref[0] = jnp.where(lane < MLA_V, pltpu.roll(outs[0], MLA_V, 1), outs[1]).astype(o_ref.dtype)

    is_ctx = pl.program_id(2) >= n_lat // tq

    @pl.when(is_ctx)
    def _():
        run(n_lat, n_all)

    @pl.when(jnp.logical_not(is_ctx))
    def _():
        run(0, n_all)


def _attn_call(q, k, v, n_lat):
    b, h, n_all, _ = q.shape
    tq = TOKEN_TILE
    kv_spec = pl.BlockSpec((1, 2, n_all, LANES), lambda bb, hp, i: (bb, hp, 0, 0))
    est = 2 * (2 * tq * LANES * 2 + 2 * 2 * n_all * LANES * 2 + tq * LANES * 2) + 7 * tq * n_all * 4
    return pl.pallas_call(
        functools.partial(_attn_kernel, n_lat=n_lat),
        grid=(b, h // 2, pl.cdiv(n_all, tq)),
        in_specs=[pl.BlockSpec((1, 2, tq, LANES), lambda bb, hp, i: (bb, hp, i, 0)), kv_spec, kv_spec],
        out_specs=pl.BlockSpec((1, tq, LANES), lambda bb, hp, i: (bb, i, hp)),
        out_shape=jax.ShapeDtypeStruct((b, n_all, h * MLA_V), BF16),
        compiler_params=pltpu.CompilerParams(vmem_limit_bytes=_vmem_limit(est)),
        name="mla_attention",
    )(q, k, v)


def _hg_consts(c):
    ri = lax.broadcasted_iota(jnp.int32, (c, c), 0)
    ci = lax.broadcasted_iota(jnp.int32, (c, c), 1)
    cst = {
        "tril": jnp.where(ci <= ri, 1.0, 0.0).astype(BF16),
        "triu": jnp.where(ci >= ri, 1.0, 0.0).astype(BF16),
        "lvl": {},
    }
    h = SUBLANES
    while 2 * h <= c:
        same =jnp.right_shift(ri, int(np.log2(2 * h))) == jnp.right_shift(ci, int(np.log2(2 * h)))
        r_lo = jnp.bitwise_and(ri, 2 * h - 1) < h
        c_lo = jnp.bitwise_and(ci, 2 * h - 1) < h
        fwd = jnp.where(same, jnp.where(r_lo, 0.0, jnp.where(c_lo, 1.0, 0.0)), 0.0)
        bwd = jnp.where(same, jnp.where(r_lo, jnp.where(c_lo, 0.0, 1.0), 0.0), 0.0)
        cst["lvl"][(h, False)] = fwd
        cst["lvl"][(h, True)] = bwd
        h *= 2
    r2 = lax.broadcasted_iota(jnp.int32, (LANES, LANES), 0)
    c2 = lax.broadcasted_iota(jnp.int32, (LANES, LANES), 1)
    bd = (r2 < HG_DK) == (c2 < HG_DK)
    cst["bd"] = bd
    cst["bo"] = jnp.where(bd, 1.0, 0.0).astype(BF16)
    lane = lax.broadcasted_iota(jnp.int32, (c, LANES), 1)
    cst["hm"] = [lane < HG_DK, lane >= HG_DK]
    cst["rit"] = jnp.bitwise_and(lax.broadcasted_iota(jnp.int32, (c, LANES), 0), SUBLANES - 1)
    return cst


def _hg_level_ref(b, h, rev):
    c = b.shape[0]
    off = h if rev else h - 1
    pieces = [jnp.broadcast_to(b[blk * 2 * h + off:blk * 2 * h + off + 1, :], (2 * h, LANES))
              for blk in range(c // (2 * h))]
    return pieces[0] if len(pieces) == 1 else jnp.concatenate(pieces, axis=0)


def _hg_intra(streams, *, cst):
    c = streams[0][0].shape[0]
    ns = len(streams)
    bs = [_cumsum_mm(cst["triu"] if s[5] else cst["tril"], s[3]) for s in streams]
    tots = [b[0:1, :] if s[5] else b[c - 1:c, :] for b, s in zip(bs, streams)]
    qes = [(s[0] * jnp.exp2(b)).astype(BF16) for b, s in zip(bs, streams)]
    atts = [[None, None] for _ in range(ns)]
    sign = jnp.uint32(0x80000000)
    h = SUBLANES
    while 2 * h <= c:
        ops = []
        for si, (b, (q, k, f, g, v, rev)) in enumerate(zip(bs, streams)):
            ref = _hg_level_ref(b, h, rev)
            nabs = pltpu.bitcast(pltpu.bitcast(b - ref, jnp.uint32) | sign, F32)
            e = jnp.exp2(nabs)
            qh = q * e
            ops.append(([jnp.where(cst["hm"][j], qh, 0.0).astype(BF16) for j in range(2)], (k * e).astype(BF16)))
        prods = [[_dot_nt(qj, kh) for qj in qjs] for qjs, kh in ops]
        for si in range(ns):
            msk = cst["lvl"][(h, streams[si][5])]
            for j in range(2):
                t = prods[si][j] * msk
                atts[si][j] = t if atts[si][j] is None else atts[si][j] + t
        h *= 2
    outs = []
    for si, (q, k, f, g, v, rev) in enumerate(streams):
        o = None
        for j in range(2):
            vj = jnp.where(cst["hm"][j], v, 0.0).astype(BF16)
            oj = _dot(atts[si][j].astype(BF16), vj)
            o = oj if o is None else o + oj
        outs.append(o)

    def shift(x, jj, rev):
        if jj == 0:
            return x
        x3 = x.reshape(c // SUBLANES, SUBLANES, LANES)
        return pltpu.roll(x3, (SUBLANES - jj) if rev else jj, 1).reshape(c, LANES)

    es = [None] * ns
    for dd in range(SUBLANES):
        ws = []
        for si, (q, k, f, g, v, rev) in enumerate(streams):
            if dd == 0:
                w = q * k
            else:
                fd = shift(f, dd - 1, rev)
                es[si] = fd if es[si] is None else es[si] * fd
                valid = (cst["rit"] <= SUBLANES - 1 - dd) if rev else (cst["rit"] >= dd)
                w = jnp.where(valid, q * shift(k, dd, rev) * es[si], 0.0)
            ws.append(w.astype(BF16))
        sums = [_dot(w, cst["bo"]) for w in ws]
        for si, (q, k, f, g, v, rev) in enumerate(streams):
            outs[si] = outs[si] + sums[si] * shift(v, dd, rev)
    res = []
    for si, (q, k, f, g, v, rev) in enumerate(streams):
        ke = (k * jnp.exp2(tots[si] - bs[si])).astype(BF16)
        upd = jnp.where(cst["bd"], _dot_tn(v.astype(BF16), ke), 0.0)
        res.append((outs[si], qes[si], upd, jnp.exp2(tots[si])))
    return res


def _hgrn_kernel(q_ref, fff_ref, ffb_ref, iv_ref, og_ref, lb_ref, ng_ref, r_ref,
                 o_ref, qef_ref, qeb_ref, updf_ref, updb_ref, decf_ref, decb_ref, *, n_lat):
    c = HG_CHUNK
    n_all = q_ref.shape[1]
    nc = n_all // c
    ncc = (n_all - n_lat) // c
    cst = _hg_consts(c)
    bo = cst["bo"]
    ng = ng_ref[...]
    tr = ROW_TILE
    dirs = ((False, fff_ref, lb_ref[0:1, :], qef_ref, updf_ref, decf_ref),
            (True, ffb_ref, lb_ref[1:2, :], qeb_ref, updb_ref, decb_ref))
    nb = HG_BATCH
    assert nc % nb == 0

    def intra(i, carry):
        streams, where = [], []
        for u in range(nb):
            ci = i * nb + u
            rows = pl.ds(pl.multiple_of(ci * c, c), c)
            q = _silu(q_ref[0, rows, :])
            v = iv_ref[0, rows, :]
            for rev, ff_ref, lb, qe_ref, upd_ref, dec_ref in dirs:
                sig, sigm = _sigmoid_pair(ff_ref[0, rows, :])
                f = lb + (1.0 - lb) * sig
                streams.append((q, (1.0 - lb) * sigm, f, jnp.log2(f), v, rev))
                where.append((ci, rows, qe_ref, upd_ref, dec_ref))
        res = _hg_intra(streams, cst=cst)
        for u in range(nb):
            o_ref[where[2 * u][1], :] = res[2 * u][0] + res[2 * u + 1][0]
        for (_, qe, upd, dec), (ci, rows, qe_ref, upd_ref, dec_ref) in zip(res, where):
            qe_ref[rows, :] = qe
            upd_ref[ci] = upd
            dec_ref[ci] = jnp.broadcast_to(dec, (SUBLANES, LANES))
        return carry

    lax.fori_loop(0, nc // nb, intra, 0)

    def scan(i, carry):
        st_f, st_b = carry
        jf = jnp.where(i < ncc, nc - ncc + i, i - ncc)
        rows = pl.ds(pl.multiple_of(jf * c, c), c)
        o_ref[rows, :] = o_ref[rows, :] + _dot_nt(qef_ref[rows, :], st_f.astype(BF16))
        st_f = st_f * decf_ref[jf][0:1, :] + updf_ref[jf]
        ib = nc - 1 - i
        rows = pl.ds(pl.multiple_of(ib * c, c), c)
        o_ref[rows, :] = o_ref[rows, :] + _dot_nt(qeb_ref[rows, :], st_b.astype(BF16))
        st_b = st_b * decb_ref[ib][0:1, :] + updb_ref[ib]
        return st_f, st_b

    zero = jnp.zeros((LANES, LANES), F32)
    lax.fori_loop(0, nc, scan, (zero, zero), unroll=2)

    def readout(i, carry):
        rows = pl.ds(pl.multiple_of(i * tr, tr), tr)
        o = o_ref[rows, :]
        ms = _dot((o * o).astype(BF16), bo) * (1.0 / HG_DV)
        r = o * lax.rsqrt(ms + EPS) * ng * _silu(og_ref[0, rows, :])
        r_ref[0, rows, :] = r.astype(r_ref.dtype)
        return carry

    lax.fori_loop(0, n_all // tr, readout, 0)


def _hgrn_call(p_hg, lb, ng, n_lat):
    b, n_all, _ = p_hg.shape
    nc = n_all // HG_CHUNK
    col = lambda j: pl.BlockSpec((1, n_all, LANES), lambda bb, pr: (bb, 0, 2 * j + pr))
    scratch = [pltpu.VMEM((n_all, LANES), F32), pltpu.VMEM((n_all, LANES), BF16), pltpu.VMEM((n_all, LANES), BF16),
               pltpu.VMEM((nc, LANES, LANES), F32), pltpu.VMEM((nc, LANES, LANES), F32),
               pltpu.VMEM((nc, SUBLANES, LANES), F32), pltpu.VMEM((nc, SUBLANES, LANES), F32)]
    est = 2 * 6 * n_all * LANES * 4 + 2 * n_all * LANES * 4 + 2 * nc * LANES * LANES * 4 + 8 * 1024 * 1024
    return pl.pallas_call(
        functools.partial(_hgrn_kernel, n_lat=n_lat),
        grid=(b, 2),
        in_specs=[col(0), col(1), col(2), col(3), col(4),
                  pl.BlockSpec((2, LANES), lambda bb, pr: (0, pr)),
                  pl.BlockSpec((1, LANES), lambda bb, pr: (0, pr))],
        out_specs=pl.BlockSpec((1, n_all, LANES), lambda bb, pr: (bb, 0, pr)),
        out_shape=jax.ShapeDtypeStruct((b, n_all, HG_HEADS * HG_DV), BF16),
        scratch_shapes=scratch,
        compiler_params=pltpu.CompilerParams(vmem_limit_bytes=_vmem_limit(est)),
        name="hgrn2_bidir",
    )(p_hg, p_hg, p_hg, p_hg, p_hg, lb, ng)


def _ssd_intra(chunks, *, cst, dtb, a_neg):
    n = chunks[0][0].shape[0]
    lo = cst["lane"] < SSM_STATE
    hi = jnp.logical_not(lo)
    xss = [xa[:, 0:LANES] for xa, _ in chunks]
    bms, cms = [], []
    for xa, _ in chunks:
        bc = xa[:, LANES:2 * LANES]
        bms.append(jnp.where(lo, bc, 0.0).astype(BF16))
        cms.append(jnp.where(lo, pltpu.roll(bc, SSM_STATE, 1), 0.0).astype(BF16))
    gmats = [_dot_nt(cm, bm) for cm, bm in zip(cms, bms)]
    pairs = [(ci, d) for ci in range(len(chunks)) for d in range(2)]
    dts, css = [], []
    for ci, d in pairs:
        sl = slice(d * LANES, (d + 1) * LANES)
        xdt = chunks[ci][1][:, sl] + dtb[:, sl]
        dt = jnp.maximum(xdt, 0.0) + jnp.log(1.0 + jnp.exp(-jnp.abs(xdt)))
        dts.append(dt)
        css.append(_cumsum_mm(cst["triu"] if d else cst["tril"], dt * a_neg[:, sl]))
    tots = [cs[0:1, :] if d else cs[n - 1:n, :] for cs, (ci, d) in zip(css, pairs)]
    csrs = [pltpu.roll(cs, SSM_HEADDIM, 1) for cs in css]
    csts = [cs.T for cs in css]
    xds = [xss[ci] * dt for dt, (ci, d) in zip(dts, pairs)]
    lhs, rhs = [], []
    for pi, (ci, d) in enumerate(pairs):
        for j in range(2):
            colb = jnp.where(lo, css[pi], csrs[pi]) if j == 0 else jnp.where(lo, csrs[pi], css[pi])
            rowb = jnp.broadcast_to(csts[pi][j * SSM_HEADDIM:j * SSM_HEADDIM + 1, :], (n, n))
            lmat = jnp.where(cst["tri_mask_b" if d else "tri_mask_f"], jnp.exp2(jnp.minimum(colb - rowb, 0.0)), 0.0)
            lhs.append((gmats[ci] * lmat).astype(BF16))
            rhs.append(jnp.where(lo if j == 0 else hi, xds[pi], 0.0).astype(BF16))
    prods = [_dot(a, b) for a, b in zip(lhs, rhs)]
    xins = [(xd * jnp.exp2(tot - cs)).astype(BF16) for xd, tot, cs in zip(xds, tots, css)]
    upds = [_dot_tn(bms[ci], xin) for xin, (ci, d) in zip(xins, pairs)]
    res = []
    for ci in range(len(chunks)):
        y = prods[4 * ci] + prods[4 * ci + 1] + prods[4 * ci + 2] + prods[4 * ci + 3]
        outs = [(jnp.exp2(css[2 * ci + d]), upds[2 * ci + d], jnp.exp2(tots[2 * ci + d])) for d in range(2)]
        res.append((y, cms[ci], outs))
    return res


def _ssd_kernel(z_ref, xbc_ref, dt_ref, cw_ref, cb_ref, dtb_ref, alog_ref, dsk_ref, ng_ref,
                y_ref, xa_ref, ya_ref, cm_ref, dof_ref, dob_ref, updf_ref, updb_ref, decf_ref, decb_ref, *, n_lat):
    c = SSM_CHUNK
    n_all = z_ref.shape[1]
    nc = n_all // c
    ncc = (n_all - n_lat) // c
    nl = nc - ncc
    ri = lax.broadcasted_iota(jnp.int32, (c, c), 0)
    ci_ = lax.broadcasted_iota(jnp.int32, (c, c), 1)
    cst = {
        "tril": jnp.where(ci_ <= ri, 1.0, 0.0).astype(BF16),
        "triu": jnp.where(ci_ >= ri, 1.0, 0.0).astype(BF16),
        "tri_mask_f": ci_ <= ri,
        "tri_mask_b": ci_ >= ri,
        "lane": lax.broadcasted_iota(jnp.int32, (c, LANES), 1),
    }
    dtb = dtb_ref[...]
    a_neg = -jnp.exp(alog_ref[...]) * float(np.log2(np.e))
    cb = cb_ref[...]
    halo = SUBLANES

    def conv_body(ci, carry):
        r0 = pl.multiple_of(ci * c, c)
        cur = xbc_ref[0, pl.ds(r0, c), :]
        first = jnp.logical_or(ci == 0, ci == nl)
        last = jnp.logical_or(ci == nl - 1, ci == nc - 1)
        rp = pl.multiple_of(jnp.maximum(r0 - halo, 0), halo)
        rn = pl.multiple_of(jnp.minimum(r0 + c, n_all - halo), halo)
        prev = xbc_ref[0, pl.ds(rp, halo), :] * jnp.where(first, 0.0, 1.0)
        nxt = xbc_ref[0, pl.ds(rn, halo), :] * jnp.where(last, 0.0, 1.0)
        ext = jnp.concatenate([prev, cur, nxt], axis=0)
        acc = jnp.broadcast_to(cb, (c, 2 * LANES))
        for j in range(SSM_CONV):
            s = (SSM_CONV // 2 - j) % (c + 2 * halo)
            sh = ext if s == 0 else pltpu.roll(ext, s, 0)
            acc = acc + cw_ref[j:j + 1, :] * sh[halo:halo + c, :]
        xa_ref[pl.ds(r0, c), :] = _silu(acc)
        return carry

    lax.fori_loop(0, nc, conv_body, 0)

    def intra(i, carry, nb, first):
        cis = [first + i * nb + u for u in range(nb)]
        rws = [pl.ds(pl.multiple_of(ci * c, c), c) for ci in cis]
        res = _ssd_intra([(xa_ref[rows, :], dt_ref[0, rows, :]) for rows in rws], cst=cst, dtb=dtb, a_neg=a_neg)
        for ci, rows, (y, cm, outs) in zip(cis, rws, res):
            ya_ref[rows, :] = y
            cm_ref[rows, :] = cm
            for (dec_out, upd, dec), do_ref, upd_ref, dec_ref in zip(outs, (dof_ref, dob_ref), (updf_ref, updb_ref), (decf_ref, decb_ref)):
                do_ref[rows, :] = dec_out
                upd_ref[ci] = upd
                dec_ref[ci] = jnp.broadcast_to(dec, (SUBLANES, LANES))
        return carry

    nbl = SSM_BATCH if nl % SSM_BATCH == 0 else 1
    nbc = ncc if ncc <= SSM_BATCH else 1
    lax.fori_loop(0, nl // nbl, functools.partial(intra, nb=nbl, first=0), 0)
    lax.fori_loop(0, ncc // nbc, functools.partial(intra, nb=nbc, first=nl), 0)

    def scan(i, carry):
        st_f, st_b = carry
        jf = jnp.where(i < ncc, nl + i, i - ncc)
        rows = pl.ds(pl.multiple_of(jf * c, c), c)
        ya_ref[rows, :] = ya_ref[rows, :] + _dot(cm_ref[rows, :], st_f.astype(BF16)) * dof_ref[rows, :]
        st_f = st_f * decf_ref[jf][0:1, :] + updf_ref[jf]
        ib = nc - 1 - i
        rows = pl.ds(pl.multiple_of(ib * c, c), c)
        ya_ref[rows, :] = ya_ref[rows, :] + _dot(cm_ref[rows, :], st_b.astype(BF16)) * dob_ref[rows, :]
        st_b = st_b * decb_ref[ib][0:1, :] + updb_ref[ib]
        return st_f, st_b

    zst = jnp.zeros((LANES, LANES), F32)
    lax.fori_loop(0, nc, scan, (zst, zst), unroll=2)

    dsk = dsk_ref[...]
    ng = ng_ref[...]
    tr = ROW_TILE

    def readout(i, carry):
        rows = pl.ds(pl.multiple_of(i * tr, tr), tr)
        y = ya_ref[rows, :] + dsk * xa_ref[rows, 0:LANES]
        yz = y * _silu(z_ref[0, rows, :])
        y_ref[0, rows, :] = _rms(yz, ng).astype(y_ref.dtype)
        return carry

    lax.fori_loop(0, n_all // tr, readout, 0)


def _ssd_call(p_ssm, conv_w, conv_b, dtb, alog, dsk, ng, n_lat):
    b, n_all, _ = p_ssm.shape
    assert SSM_CHUNK == LANES
    nc = n_all // SSM_CHUNK
    vec = pl.BlockSpec((1, LANES), lambda bb, g: (0, g))
    vec2 = pl.BlockSpec((1, 2 * LANES), lambda bb, g: (0, g))
    scratch = [pltpu.VMEM((n_all, 2 * LANES), F32), pltpu.VMEM((n_all, LANES), F32), pltpu.VMEM((n_all, LANES), BF16),
               pltpu.VMEM((n_all, LANES), F32), pltpu.VMEM((n_all, LANES), F32),
               pltpu.VMEM((nc, LANES, LANES), F32), pltpu.VMEM((nc, LANES, LANES), F32),
               pltpu.VMEM((nc, SUBLANES, LANES), F32), pltpu.VMEM((nc, SUBLANES, LANES), F32)]
    est = 2 * (n_all * 5 * LANES * 4 + n_all * LANES * 2) + n_all * 6 * LANES * 4 + 2 * nc * LANES * LANES * 4 + 8 * 1024 * 1024
    return pl.pallas_call(
        functools.partial(_ssd_kernel, n_lat=n_lat),
        grid=(b, SSM_GROUPS),
        in_specs=[
            pl.BlockSpec((1, n_all, LANES), lambda bb, g: (bb, 0, g)),
            pl.BlockSpec((1, n_all, 2 * LANES), lambda bb, g: (bb, 0, 1 + g)),
            pl.BlockSpec((1, n_all, 2 * LANES), lambda bb, g: (bb, 0, 3 + g)),
            pl.BlockSpec((SSM_CONV, 2 * LANES), lambda bb, g: (0, g)),
            pl.BlockSpec((1, 2 * LANES), lambda bb, g: (0, g)),
            vec2, vec2, vec, vec,
        ],
        out_specs=pl.BlockSpec((1, n_all, LANES), lambda bb, g: (bb, 0, g)),
        out_shape=jax.ShapeDtypeStruct((b, n_all, SSM_INNER), BF16),
        scratch_shapes=scratch,
        compiler_params=pltpu.CompilerParams(vmem_limit_bytes=_vmem_limit(est)),
        name="ssd_bidir",
    )(p_ssm, p_ssm, p_ssm, conv_w, conv_b, dtb, alog, dsk, ng)


def _ffn_kernel(x_ref, a_ref, r_ref, s_ref, mod_ref, g2_ref, gf_ref, wa_ref, wr_ref, ws_ref, w1_ref, w2_ref,
                o_ref, *, hid_cuts, final):
    d = x_ref.shape[2]
    hidden = w2_ref.shape[0]
    x = x_ref[0]
    mod = lambda j: mod_ref[0, 0, :, j * d:(j + 1) * d]
    mix = _dot(a_ref[0], wa_ref[...]) + _dot(r_ref[0], wr_ref[...]) + _dot(s_ref[0], ws_ref[...])
    x1 = x + mod(2) * mix
    h2 = (_rms(x1, g2_ref[...]) * (1.0 + mod(4)) + mod(3)).astype(BF16)
    y = None
    for c0, c1 in zip(hid_cuts[:-1], hid_cuts[1:]):
        ha = _dot(h2, w1_ref[:, c0:c1])
        hb = _dot(h2, w1_ref[:, hidden + c0:hidden + c1])
        act = (_silu(ha) * hb).astype(BF16)
        yc = _dot(act, w2_ref[c0:c1, :])
        y = yc if y is None else y + yc
    x2 = x1 + mod(5) * y
    o_ref[0] = _rms(x2, gf_ref[...]) if final else x2


def _ffn_call(x_all, a, r, s, modsel, g2, gf, wa, wr, ws, w1, w2, n_lat, final):
    b, n_all, d = x_all.shape
    tm = TOKEN_TILE
    hidden = w2.shape[0]
    n_tiles = hidden // MXU_TILE
    assert hidden % MXU_TILE == 0
    hid_cuts = tuple(MXU_TILE * ((n_tiles * j + FFN_SPLIT - 1) // FFN_SPLIT) for j in range(FFN_SPLIT + 1))
    hid_chunk = max(b1 - b0 for b0, b1 in zip(hid_cuts[:-1], hid_cuts[1:]))
    nlt = n_lat // tm
    n_out = n_lat if final else n_all
    full = lambda arr: pl.BlockSpec(arr.shape, lambda bb, i: (0,) * arr.ndim, pipeline_mode=pl.Buffered(1))
    tok = lambda w: pl.BlockSpec((1, tm, w), lambda bb, i: (bb, i, 0))
    est = (wa.size + wr.size + ws.size + w1.size + w2.size) * 2 + 2 * (2 * tm * d * 4 + tm * 1024 * 2) \
        + 5 * tm * hid_chunk * 4 + 5 * tm * d * 4
    return pl.pallas_call(
        functools.partial(_ffn_kernel, hid_cuts=hid_cuts, final=final),
        grid=(b, pl.cdiv(n_out, tm)),
        in_specs=[
            tok(d), tok(a.shape[2]), tok(r.shape[2]), tok(s.shape[2]),
            pl.BlockSpec((1, 1, 1, modsel.shape[3]), lambda bb, i: (bb, jnp.minimum(i // nlt, 1), 0, 0)),
            pl.BlockSpec((1, d), lambda bb, i: (0, 0)),
            pl.BlockSpec((1, d), lambda bb, i: (0, 0)),
            full(wa), full(wr), full(ws), full(w1), full(w2),
        ],
        out_specs=tok(d),
        out_shape=jax.ShapeDtypeStruct((b, n_out, d), F32),
        compiler_params=pltpu.CompilerParams(vmem_limit_bytes=_vmem_limit(est)),
        name="outproj_ffn",
    )(x_all, a, r, s, modsel, g2, gf, wa, wr, ws, w1, w2)


def _win_perm():
    perm = np.full((P_TOTAL,), -1, np.int64)
    perm[P_QLAT:P_QLAT + MLA_Q_RANK + MLA_KV_RANK] = np.arange(MLA_Q_RANK + MLA_KV_RANK)
    perm[P_KR + MLA_NOPE:P_KR + MLA_QK] = MLA_Q_RANK + MLA_KV_RANK + np.arange(MLA_ROPE)
    perm[P_HG:P_HG + HG_COLS] = MLA_COLS + np.arange(HG_COLS)
    o2 = MLA_COLS + HG_COLS
    perm[P_SSM:P_SSM + SSM_INNER] = o2 + np.arange(SSM_INNER)
    xo = o2 + SSM_INNER
    bo = xo + SSM_INNER
    co = bo + SSM_GROUPS * SSM_STATE
    dto = o2 + SSM_INNER + SSM_XBC
    for g in range(SSM_GROUPS):
        base = P_SSM + SSM_INNER + g * 2 * LANES
        perm[base:base + LANES] = xo + g * LANES + np.arange(LANES)
        perm[base + LANES:base + LANES + SSM_STATE] = bo + g * SSM_STATE + np.arange(SSM_STATE)
        perm[base + LANES + SSM_STATE:base + 2 * LANES] = co + g * SSM_STATE + np.arange(SSM_STATE)
        dbase = P_SSM + SSM_INNER + 2 * 2 * LANES + g * 2 * LANES
        for d in range(2):
            for j in range(2):
                lo = dbase + d * LANES + j * SSM_HEADDIM
                perm[lo:lo + SSM_HEADDIM] = dto + d * SSM_HEADS + 2 * g + j
    return perm


def _conv_perm():
    perm = np.zeros((SSM_XBC,), np.int64)
    for g in range(SSM_GROUPS):
        base = g * 2 * LANES
        perm[base:base + LANES] = g * LANES + np.arange(LANES)
        perm[base + LANES:base + LANES + SSM_STATE] = SSM_INNER + g * SSM_STATE + np.arange(SSM_STATE)
        perm[base + LANES + SSM_STATE:base + 2 * LANES] = SSM_INNER + SSM_GROUPS * SSM_STATE + g * SSM_STATE + np.arange(SSM_STATE)
    return perm


def _gather_cols(w, perm):
    idx = jnp.asarray(np.maximum(perm, 0), jnp.int32)
    out = jnp.take(w, idx, axis=-1)
    return jnp.where(jnp.asarray(perm >= 0), out, 0.0)


def _head_vec(v):
    depth = v.shape[0]
    v5 = v.reshape(depth, 2, SSM_GROUPS, 2, 1)
    v5 = jnp.broadcast_to(v5, (depth, 2, SSM_GROUPS, 2, SSM_HEADDIM))
    return jnp.transpose(v5, (0, 2, 1, 3, 4)).reshape(depth, 1, SSM_GROUPS * 2 * LANES)


def _rope_tables(n_ctx, n_lat, scale):
    rows = n_lat // GRID_W
    row = jnp.repeat(jnp.arange(rows, dtype=F32), GRID_W)
    col = jnp.tile(jnp.arange(GRID_W, dtype=F32), rows)
    n_freq = MLA_ROPE // 4
    inv = ROPE_BASE ** (-jnp.arange(n_freq, dtype=F32) / n_freq)
    ang = jnp.stack([row[:, None] * inv, col[:, None] * inv], axis=1)
    cos, sin = jnp.cos(ang), jnp.sin(ang)
    c_r = jnp.concatenate([cos[:, 0], cos[:, 0], cos[:, 1], cos[:, 1]], axis=-1)
    s_r = jnp.concatenate([-sin[:, 0], sin[:, 0], -sin[:, 1], sin[:, 1]], axis=-1)
    c_lat = jnp.concatenate([jnp.ones((n_lat, MLA_NOPE), F32), c_r, jnp.ones((n_lat, LANES - MLA_QK), F32)], axis=-1)
    s_lat = jnp.concatenate([jnp.zeros((n_lat, MLA_NOPE), F32), s_r, jnp.zeros((n_lat, LANES - MLA_QK), F32)], axis=-1)
    c_all = jnp.concatenate([c_lat, jnp.ones((n_ctx, LANES), F32)], axis=0)
    s_all = jnp.concatenate([s_lat, jnp.zeros((n_ctx, LANES), F32)], axis=0)
    return c_all * scale, s_all * scale, c_all, s_all


def kernel(x, c, ctx, c_ctx, w_ada, b_ada, norm1_g, norm2_g, w_in, mla_qa_g, mla_wqb, mla_kva_g, mla_wkvb, hg_lb_logits, hg_norm_g, ssm_conv_w, ssm_conv_b, ssm_dt_bias, ssm_a_log, ssm_d, ssm_norm_g, w_out, w_ffn_in, w_ffn_out, final_g):
    bsz, n_lat, d = x.shape
    n_ctx = ctx.shape[1]
    depth = w_ada.shape[0]
    assert n_lat % TOKEN_TILE == 0 and n_lat % GRID_W == 0 and n_ctx <= TOKEN_TILE
    assert n_ctx % (HG_BATCH * HG_CHUNK) == 0 and n_ctx % SSM_CHUNK == 0 and n_ctx % ROW_TILE == 0

    rows = -(-(bsz + 1) // SUBLANES) * SUBLANES
    cc = jnp.zeros((rows, d), F32).at[:bsz].set(c).at[bsz].set(c_ctx)
    mods = _mod_call(cc, w_ada.astype(BF16), b_ada.reshape(depth, 1, 6 * d))
    modsel = jnp.stack([mods[:, :bsz], jnp.broadcast_to(mods[:, bsz:bsz + 1], (depth, bsz, 6 * d))], axis=2)[:, :, :, None, :]

    w_in_p = _gather_cols(w_in, _win_perm()).astype(BF16)
    wqb4 = mla_wqb.reshape(depth, MLA_Q_RANK, MLA_HEADS, MLA_QK)
    wqb_p = jnp.pad(wqb4, ((0, 0), (0, 0), (0, 0), (0, LANES - MLA_QK))).reshape(depth, MLA_Q_RANK, MLA_HEADS * LANES).astype(BF16)
    wkvb_b = mla_wkvb.astype(BF16)
    cperm = _conv_perm()
    conv_w_p = jnp.take(ssm_conv_w, jnp.asarray(cperm, jnp.int32), axis=-1)
    conv_b_p = jnp.take(ssm_conv_b, jnp.asarray(cperm, jnp.int32), axis=-1).reshape(depth, 1, SSM_XBC)
    dtb_p = _head_vec(ssm_dt_bias)
    alog_p = _head_vec(ssm_a_log)
    dsk_p = jnp.repeat(ssm_d, SSM_HEADDIM, axis=-1).reshape(depth, 1, SSM_INNER)
    lb_soft = jax.nn.softmax(hg_lb_logits.astype(F32), axis=0)
    lb_all = jnp.cumsum(lb_soft, axis=0) - lb_soft[0]
    w_out_b = w_out.astype(BF16)
    a_w = MLA_HEADS * MLA_V
    r_w = HG_HEADS * HG_DV
    w1_b = w_ffn_in.astype(BF16)
    w2_b = w_ffn_out.astype(BF16)
    tabs = _rope_tables(n_ctx, n_lat, MLA_QK ** -0.5 * float(np.log2(np.e)))

    x_all = jnp.concatenate([x, ctx], axis=1)
    for l in range(depth):
        q, k, v, p_hg, p_ssm = _inproj_call(
            x_all, modsel[l], norm1_g[l].reshape(1, d), w_in_p[l], mla_qa_g[l].reshape(1, -1),
            mla_kva_g[l].reshape(1, -1), wqb_p[l], wkvb_b[l], tabs, n_lat)
        a = _attn_call(q, k, v, n_lat)
        r = _hgrn_call(p_hg, lb_all[l], hg_norm_g[l].reshape(1, -1), n_lat)
        s = _ssd_call(p_ssm, conv_w_p[l], conv_b_p[l], dtb_p[l], alog_p[l], dsk_p[l],
                      ssm_norm_g[l].reshape(1, -1), n_lat)
        x_all = _ffn_call(x_all, a, r, s, modsel[l], norm2_g[l].reshape(1, d), final_g.reshape(1, d),
                          w_out_b[l, :a_w], w_out_b[l, a_w:a_w + r_w], w_out_b[l, a_w + r_w:], w1_b[l], w2_b[l],
                          n_lat, final=(l == depth - 1))
    return x_all
```

```python
import functools

import numpy as np
import jax
import jax.numpy as jnp
from jax import lax
from jax.experimental import pallas as pl
from jax.experimental.pallas import tpu as pltpu

F32 = jnp.float32
BF16 = jnp.bfloat16
EPS = 1e-6

LANES = 128
SUBLANES = 8
MXU_TILE = 256
VMEM_BYTES = 64 * 1024 * 1024

GRID_W = 64
MLA_HEADS = 8
MLA_Q_RANK = 384
MLA_KV_RANK = 256
MLA_NOPE = 64
MLA_ROPE = 32
MLA_V = 64
MLA_QK = MLA_NOPE + MLA_ROPE
ROPE_BASE = 10000.0
HG_HEADS = 4
HG_DK = 64
HG_DV = 64
HG_W = HG_HEADS * HG_DK
SSM_HEADS = 4
SSM_HEADDIM = 64
SSM_GROUPS = 2
SSM_STATE = 64
SSM_CONV = 5
SSM_INNER = SSM_HEADS * SSM_HEADDIM
SSM_XBC = SSM_INNER + 2 * SSM_GROUPS * SSM_STATE

ROPE_F = MLA_ROPE // 4
ROPE_PERM = np.concatenate([np.concatenate([a * 2 * ROPE_F + np.arange(2 * ROPE_F), a * 2 * ROPE_F + np.arange(ROPE_F)])
                            for a in range(2)])
ROPE_LANES = ROPE_PERM.size

MLA_COLS = MLA_Q_RANK + MLA_KV_RANK + MLA_ROPE
HG_COLS = 3 * HG_W + 2 * HG_HEADS * HG_DV
SSM_COLS = SSM_INNER + SSM_XBC + 2 * SSM_HEADS

P_QLAT = 0
P_KVLAT = MLA_Q_RANK
P_KR = P_KVLAT + MLA_KV_RANK
P_HG = P_KR + LANES
P_SSM = P_HG + HG_COLS
SSM_BLOCK = SSM_INNER + 2 * 2 * LANES + 2 * 2 * LANES
P_TOTAL = P_SSM + SSM_BLOCK

TOKEN_TILE = 512
ROW_TILE = 256
HG_CHUNK = 128
HG_BATCH = 2
SSM_CHUNK = 128
SSM_BATCH = 4
FFN_SPLIT = 2

NT_DIMS = (((1,), (1,)), ((), ()))
TN_DIMS = (((0,), (0,)), ((), ()))


def _vmem_limit(nbytes):
    return int(min(VMEM_BYTES - 8 * 1024 * 1024, max(nbytes, 16 * 1024 * 1024)))


def _sigmoid_pair(x):
    e = jnp.exp(-jnp.abs(x))
    d = 1.0 / (1.0 + e)
    ed = e * d
    pos = x >= 0
    return jnp.where(pos, d, ed), jnp.where(pos, ed, d)


def _silu(x):
    return x / (1.0 + jnp.exp(-x))


def _rms(x, g):
    ms = jnp.mean(x * x, axis=-1, keepdims=True)
    return x * lax.rsqrt(ms + EPS) * g


def _dot(a, b):
    return jnp.dot(a, b, preferred_element_type=F32)


def _dot_nt(a, b):
    return lax.dot_general(a, b, NT_DIMS, preferred_element_type=F32)


def _dot_tn(a, b):
    return lax.dot_general(a, b, TN_DIMS, preferred_element_type=F32)


def _full_or_tail(tile, tm, tail_rows, i, n):
    if tail_rows == tm:
        tile(tm)
        return
    last = i == n - 1
    pl.when(last)(lambda: tile(tail_rows))
    pl.when(jnp.logical_not(last))(lambda: tile(tm))


def _split3(x):
    x1 = x.astype(BF16)
    r1 = x - x1.astype(F32)
    x2 = r1.astype(BF16)
    x3 = (r1 - x2.astype(F32)).astype(BF16)
    return x1, x2, x3


def _cumsum_mm(tri, x):
    x1, x2, x3 = _split3(x)
    return _dot(tri, x1) + _dot(tri, x2) + _dot(tri, x3)


def _mod_kernel(c_ref, w_ref, b_ref, o_ref):
    s = _silu(c_ref[...]).astype(BF16)
    o_ref[0] = _dot(s, w_ref[0]) + b_ref[0]


def _mod_call(cc, w_ada, b_ada):
    depth, d, six_d = w_ada.shape
    rows = cc.shape[0]
    tn = 1536
    return pl.pallas_call(
        _mod_kernel,
        grid=(depth, six_d // tn),
        in_specs=[
            pl.BlockSpec((rows, d), lambda l, j: (0, 0)),
            pl.BlockSpec((1, d, tn), lambda l, j: (l, 0, j)),
            pl.BlockSpec((1, 1, tn), lambda l, j: (l, 0, j)),
        ],
        out_specs=pl.BlockSpec((1, rows, tn), lambda l, j: (l, 0, j)),
        out_shape=jax.ShapeDtypeStruct((depth, rows, six_d), F32),
        name="adaln_mod",
    )(cc, w_ada, b_ada)


def _rope(xh, c, s):
    return xh * c + pltpu.roll(xh, LANES - ROPE_F, 1) * s


def _inproj_kernel(x_ref, mod_ref, g1_ref, win_ref, qag_ref, kvag_ref, wqb_ref, wkvb_ref,
                   cq_ref, sq_ref, ck_ref, sk_ref,
                   q_ref, k_ref, v_ref, hg_ref, ssm_ref, *, tail_rows):
    d = x_ref.shape[2]

    def tile(rows):
        r = slice(0, rows)
        x = x_ref[0, r, :]
        sh = mod_ref[0, 0, :, 0:d]
        sc = mod_ref[0, 0, :, d:2 * d]
        h = _rms(x, g1_ref[...]) * (1.0 + sc) + sh
        p = _dot(h.astype(BF16), win_ref[...])
        hg_ref[0, r, :] = p[:, P_HG:P_SSM]
        ssm_ref[0, r, :] = p[:, P_SSM:P_TOTAL]
        q_lat = _rms(p[:, P_QLAT:P_KVLAT], qag_ref[...])
        kv_lat = _rms(p[:, P_KVLAT:P_KR], kvag_ref[...])
        q = _dot(q_lat.astype(BF16), wqb_ref[...])
        kv = _dot(kv_lat.astype(BF16), wkvb_ref[...])
        lane = lax.broadcasted_iota(jnp.int32, (rows, LANES), 1)
        cq, sq, ck, sk = cq_ref[r, :], sq_ref[r, :], ck_ref[r, :], sk_ref[r, :]
        kr = _rope(p[:, P_KR:P_HG], ck, sk)
        nope = lane < MLA_NOPE
        for hh in range(MLA_HEADS):
            sl = slice(hh * LANES, (hh + 1) * LANES)
            q_ref[0, hh, r, :] = _rope(q[:, sl], cq, sq).astype(q_ref.dtype)
            kvh = kv[:, sl]
            k_ref[0, hh, r, :] = jnp.where(nope, kvh, kr).astype(k_ref.dtype)
            v_ref[0, hh, r, :] = jnp.where(nope, 1.0, kvh).astype(v_ref.dtype)

    _full_or_tail(tile, x_ref.shape[1], tail_rows, pl.program_id(1), pl.num_programs(1))


def _inproj_call(x_all, modsel, g1, w_in, qa_g, kva_g, wqb, wkvb, tabs, n_lat):
    b, n_all, d = x_all.shape
    tm = TOKEN_TILE
    nt = pl.cdiv(n_all, tm)
    nlt = n_lat // tm
    full = lambda shape: pl.BlockSpec(shape, lambda bb, i: (0,) * len(shape), pipeline_mode=pl.Buffered(1))
    tab = pl.BlockSpec((tm, LANES), lambda bb, i: (i, 0))
    head_out = pl.BlockSpec((1, MLA_HEADS, tm, LANES), lambda bb, i: (bb, 0, i, 0))
    est = (w_in.size + wqb.size + wkvb.size) * 2 + 2 * (tm * d * 4 + 3 * MLA_HEADS * tm * LANES * 2
                                                        + tm * (HG_COLS + SSM_BLOCK) * 4) + 4 * tm * P_TOTAL * 4
    return pl.pallas_call(
        functools.partial(_inproj_kernel, tail_rows=n_all - (nt - 1) * tm),
        grid=(b, nt),
        in_specs=[
            pl.BlockSpec((1, tm, d), lambda bb, i: (bb, i, 0)),
            pl.BlockSpec((1, 1, 1, modsel.shape[3]), lambda bb, i: (bb, jnp.minimum(i // nlt, 1), 0, 0)),
            full((1, d)),
            full(w_in.shape),
            full((1, MLA_Q_RANK)),
            full((1, MLA_KV_RANK)),
            full(wqb.shape),
            full(wkvb.shape),
            tab, tab, tab, tab,
        ],
        out_specs=[
            head_out, head_out, head_out,
            pl.BlockSpec((1, tm, HG_COLS), lambda bb, i: (bb, i, 0)),
            pl.BlockSpec((1, tm, SSM_BLOCK), lambda bb, i: (bb, i, 0)),
        ],
        out_shape=[
            jax.ShapeDtypeStruct((b, MLA_HEADS, n_all, LANES), BF16),
            jax.ShapeDtypeStruct((b, MLA_HEADS, n_all, LANES), BF16),
            jax.ShapeDtypeStruct((b, MLA_HEADS, n_all, LANES), BF16),
            jax.ShapeDtypeStruct((b, n_all, HG_COLS), F32),
            jax.ShapeDtypeStruct((b, n_all, SSM_BLOCK), F32),
        ],
        compiler_params=pltpu.CompilerParams(vmem_limit_bytes=_vmem_limit(est)),
        name="inproj_mla",
    )(x_all, modsel, g1, w_in, qa_g, kva_g, wqb, wkvb, *tabs)


def _attn_kernel(q_ref, k_ref, v_ref, o_ref, *, n_lat):
    n_all = k_ref.shape[2]
    tq = q_ref.shape[2]

    def run(k0, k1):
        ss = [_dot_nt(q_ref[0, j], k_ref[0, j, k0:k1, :]) for j in range(2)]
        ps = [jnp.exp2(s - jnp.max(s, axis=-1, keepdims=True)).astype(BF16) for s in ss]
        outs = []
        for j in range(2):
            o = _dot(ps[j], v_ref[0, j, k0:k1, :])
            outs.append(o / pltpu.roll(o, MLA_V, 1))
        lane = lax.broadcasted_iota(jnp.int32, (tq, LANES), 1)
        o_ref[0] = jnp.where(lane < MLA_V, pltpu.roll(outs[0], MLA_V, 1), outs[1]).astype(o_ref.dtype)

    is_ctx = pl.program_id(2) >= n_lat // tq

    @pl.when(is_ctx)
    def _():
        run(n_lat, n_all)

    @pl.when(jnp.logical_not(is_ctx))
    def _():
        run(0, n_all)


def _attn_call(q, k, v, n_lat):
    b, h, n_all, _ = q.shape
    tq = TOKEN_TILE
    kv_spec = pl.BlockSpec((1, 2, n_all, LANES), lambda bb, hp, i: (bb, hp, 0, 0))
    est = 2 * (2 * tq * LANES * 2 + 2 * 2 * n_all * LANES * 2 + tq * LANES * 2) + 7 * tq * n_all * 4
    return pl.pallas_call(
        functools.partial(_attn_kernel, n_lat=n_lat),
        grid=(b, h // 2, pl.cdiv(n_all, tq)),
        in_specs=[pl.BlockSpec((1, 2, tq, LANES), lambda bb, hp, i: (bb, hp, i, 0)), kv_spec, kv_spec],
        out_specs=pl.BlockSpec((1, tq, LANES), lambda bb, hp, i: (bb, i, hp)),
        out_shape=jax.ShapeDtypeStruct((b, n_all, h * MLA_V), BF16),
        compiler_params=pltpu.CompilerParams(vmem_limit_bytes=_vmem_limit(est)),
        name="mla_attention",
    )(q, k, v)


def _hg_consts(c):
    ri = lax.broadcasted_iota(jnp.int32, (c, c), 0)
    ci = lax.broadcasted_iota(jnp.int32, (c, c), 1)
    cst = {
        "tril": jnp.where(ci <= ri, 1.0, 0.0).astype(BF16),
        "triu": jnp.where(ci >= ri, 1.0, 0.0).astype(BF16),
        "lvl": {},
    }
    h = SUBLANES
    while 2 * h <= c:
        same =jnp.right_shift(ri, int(np.log2(2 * h))) == jnp.right_shift(ci, int(np.log2(2 * h)))
        r_lo = jnp.bitwise_and(ri, 2 * h - 1) < h
        c_lo = jnp.bitwise_and(ci, 2 * h - 1) < h
        fwd = jnp.where(same, jnp.where(r_lo, 0.0, jnp.where(c_lo, 1.0, 0.0)), 0.0)
        bwd = jnp.where(same, jnp.where(r_lo, jnp.where(c_lo, 0.0, 1.0), 0.0), 0.0)
        cst["lvl"][(h, False)] = fwd
        cst["lvl"][(h, True)] = bwd
        h *= 2
    r2 = lax.broadcasted_iota(jnp.int32, (LANES, LANES), 0)
    c2 = lax.broadcasted_iota(jnp.int32, (LANES, LANES), 1)
    bd = (r2 < HG_DK) == (c2 < HG_DK)
    cst["bd"] = bd
    cst["bo"] = jnp.where(bd, 1.0, 0.0).astype(BF16)
    lane = lax.broadcasted_iota(jnp.int32, (c, LANES), 1)
    cst["hm"] = [lane < HG_DK, lane >= HG_DK]
    cst["rit"] = jnp.bitwise_and(lax.broadcasted_iota(jnp.int32, (c, LANES), 0), SUBLANES - 1)
    return cst


def _hg_level_ref(b, h, rev):
    c = b.shape[0]
    off = h if rev else h - 1
    pieces = [jnp.broadcast_to(b[blk * 2 * h + off:blk * 2 * h + off + 1, :], (2 * h, LANES))
              for blk in range(c // (2 * h))]
    return pieces[0] if len(pieces) == 1 else jnp.concatenate(pieces, axis=0)


def _hg_intra(streams, *, cst):
    c = streams[0][0].shape[0]
    ns = len(streams)
    bs = [_cumsum_mm(cst["triu"] if s[5] else cst["tril"], s[3]) for s in streams]
    tots = [b[0:1, :] if s[5] else b[c - 1:c, :] for b, s in zip(bs, streams)]
    qes = [(s[0] * jnp.exp2(b)).astype(BF16) for b, s in zip(bs, streams)]
    atts = [[None, None] for _ in range(ns)]
    sign = jnp.uint32(0x80000000)
    h = SUBLANES
    while 2 * h <= c:
        ops = []
        for si, (b, (q, k, f, g, v, rev)) in enumerate(zip(bs, streams)):
            ref = _hg_level_ref(b, h, rev)
            nabs = pltpu.bitcast(pltpu.bitcast(b - ref, jnp.uint32) | sign, F32)
            e = jnp.exp2(nabs)
            qh = q * e
            ops.append(([jnp.where(cst["hm"][j], qh, 0.0).astype(BF16) for j in range(2)], (k * e).astype(BF16)))
        prods = [[_dot_nt(qj, kh) for qj in qjs] for qjs, kh in ops]
        for si in range(ns):
            msk = cst["lvl"][(h, streams[si][5])]
            for j in range(2):
                t = prods[si][j] * msk
                atts[si][j] = t if atts[si][j] is None else atts[si][j] + t
        h *= 2
    outs = []
    for si, (q, k, f, g, v, rev) in enumerate(streams):
        o = None
        for j in range(2):
            vj = jnp.where(cst["hm"][j], v, 0.0).astype(BF16)
            oj = _dot(atts[si][j].astype(BF16), vj)
            o = oj if o is None else o + oj
        outs.append(o)

    def shift(x, jj, rev):
        if jj == 0:
            return x
        x3 = x.reshape(c // SUBLANES, SUBLANES, LANES)
        return pltpu.roll(x3, (SUBLANES - jj) if rev else jj, 1).reshape(c, LANES)

    es = [None] * ns
    for dd in range(SUBLANES):
        ws = []
        for si, (q, k, f, g, v, rev) in enumerate(streams):
            if dd == 0:
                w = q * k
            else:
                fd = shift(f, dd - 1, rev)
                es[si] = fd if es[si] is None else es[si] * fd
                valid = (cst["rit"] <= SUBLANES - 1 - dd) if rev else (cst["rit"] >= dd)
                w = jnp.where(valid, q * shift(k, dd, rev) * es[si], 0.0)
            ws.append(w.astype(BF16))
        sums = [_dot(w, cst["bo"]) for w in ws]
        for si, (q, k, f, g, v, rev) in enumerate(streams):
            outs[si] = outs[si] + sums[si] * shift(v, dd, rev)
    res = []
    for si, (q, k, f, g, v, rev) in enumerate(streams):
        ke = (k * jnp.exp2(tots[si] - bs[si])).astype(BF16)
        upd = jnp.where(cst["bd"], _dot_tn(v.astype(BF16), ke), 0.0)
        res.append((outs[si], qes[si], upd, jnp.exp2(tots[si])))
    return res


def _hgrn_kernel(q_ref, fff_ref, ffb_ref, iv_ref, og_ref, lb_ref, ng_ref, r_ref,
                 o_ref, qef_ref, qeb_ref, updf_ref, updb_ref, decf_ref, decb_ref, *, n_lat):
    c = HG_CHUNK
    n_all = q_ref.shape[1]
    nc = n_all // c
    ncc = (n_all - n_lat) // c
    cst = _hg_consts(c)
    bo = cst["bo"]
    ng = ng_ref[...]
    tr = ROW_TILE
    dirs = ((False, fff_ref, lb_ref[0:1, :], qef_ref, updf_ref, decf_ref),
            (True, ffb_ref, lb_ref[1:2, :], qeb_ref, updb_ref, decb_ref))
    nb = HG_BATCH
    assert nc % nb == 0

    def intra(i, carry):
        streams, where = [], []
        for u in range(nb):
            ci = i * nb + u
            rows = pl.ds(pl.multiple_of(ci * c, c), c)
            q = _silu(q_ref[0, rows, :])
            v = iv_ref[0, rows, :]
            for rev, ff_ref, lb, qe_ref, upd_ref, dec_ref in dirs:
                sig, sigm = _sigmoid_pair(ff_ref[0, rows, :])
                f = lb + (1.0 - lb) * sig
                streams.append((q, (1.0 - lb) * sigm, f, jnp.log2(f), v, rev))
                where.append((ci, rows, qe_ref, upd_ref, dec_ref))
        res = _hg_intra(streams, cst=cst)
        for u in range(nb):
            o_ref[where[2 * u][1], :] = res[2 * u][0] + res[2 * u + 1][0]
        for (_, qe, upd, dec), (ci, rows, qe_ref, upd_ref, dec_ref) in zip(res, where):
            qe_ref[rows, :] = qe
            upd_ref[ci] = upd
            dec_ref[ci] = jnp.broadcast_to(dec, (SUBLANES, LANES))
        return carry

    lax.fori_loop(0, nc // nb, intra, 0)

    def scan(i, carry):
        st_f, st_b = carry
        jf = jnp.where(i < ncc, nc - ncc + i, i - ncc)
        rows = pl.ds(pl.multiple_of(jf * c, c), c)
        o_ref[rows, :] = o_ref[rows, :] + _dot_nt(qef_ref[rows, :], st_f.astype(BF16))
        st_f = st_f * decf_ref[jf][0:1, :] + updf_ref[jf]
        ib = nc - 1 - i
        rows = pl.ds(pl.multiple_of(ib * c, c), c)
        o_ref[rows, :] = o_ref[rows, :] + _dot_nt(qeb_ref[rows, :], st_b.astype(BF16))
        st_b = st_b * decb_ref[ib][0:1, :] + updb_ref[ib]
        return st_f, st_b

    zero = jnp.zeros((LANES, LANES), F32)
    lax.fori_loop(0, nc, scan, (zero, zero), unroll=2)

    def readout(i, carry):
        rows = pl.ds(pl.multiple_of(i * tr, tr), tr)
        o = o_ref[rows, :]
        ms = _dot((o * o).astype(BF16), bo) * (1.0 / HG_DV)
        r = o * lax.rsqrt(ms + EPS) * ng * _silu(og_ref[0, rows, :])
        r_ref[0, rows, :] = r.astype(r_ref.dtype)
        return carry

    lax.fori_loop(0, n_all // tr, readout, 0)


def _hgrn_call(p_hg, lb, ng, n_lat):
    b, n_all, _ = p_hg.shape
    nc = n_all // HG_CHUNK
    col = lambda j: pl.BlockSpec((1, n_all, LANES), lambda bb, pr: (bb, 0, 2 * j + pr))
    scratch = [pltpu.VMEM((n_all, LANES), F32), pltpu.VMEM((n_all, LANES), BF16), pltpu.VMEM((n_all, LANES), BF16),
               pltpu.VMEM((nc, LANES, LANES), F32), pltpu.VMEM((nc, LANES, LANES), F32),
               pltpu.VMEM((nc, SUBLANES, LANES), F32), pltpu.VMEM((nc, SUBLANES, LANES), F32)]
    est = 2 * 6 * n_all * LANES * 4 + 2 * n_all * LANES * 4 + 2 * nc * LANES * LANES * 4 + 8 * 1024 * 1024
    return pl.pallas_call(
        functools.partial(_hgrn_kernel, n_lat=n_lat),
        grid=(b, 2),
        in_specs=[col(0), col(1), col(2), col(3), col(4),
                  pl.BlockSpec((2, LANES), lambda bb, pr: (0, pr)),
                  pl.BlockSpec((1, LANES), lambda bb, pr: (0, pr))],
        out_specs=pl.BlockSpec((1, n_all, LANES), lambda bb, pr: (bb, 0, pr)),
        out_shape=jax.ShapeDtypeStruct((b, n_all, HG_HEADS * HG_DV), BF16),
        scratch_shapes=scratch,
        compiler_params=pltpu.CompilerParams(vmem_limit_bytes=_vmem_limit(est)),
        name="hgrn2_bidir",
    )(p_hg, p_hg, p_hg, p_hg, p_hg, lb, ng)


def _ssd_intra(chunks, *, cst, dtb, a_neg):
    n = chunks[0][0].shape[0]
    lo = cst["lane"] < SSM_STATE
    hi = jnp.logical_not(lo)
    xss = [xa[:, 0:LANES] for xa, _ in chunks]
    bms, cms = [], []
    for xa, _ in chunks:
        bc = xa[:, LANES:2 * LANES]
        bms.append(jnp.where(lo, bc, 0.0).astype(BF16))
        cms.append(jnp.where(lo, pltpu.roll(bc, SSM_STATE, 1), 0.0).astype(BF16))
    gmats = [_dot_nt(cm, bm) for cm, bm in zip(cms, bms)]
    pairs = [(ci, d) for ci in range(len(chunks)) for d in range(2)]
    dts, css = [], []
    for ci, d in pairs:
        sl = slice(d * LANES, (d + 1) * LANES)
        xdt = chunks[ci][1][:, sl] + dtb[:, sl]
        dt = jnp.maximum(xdt, 0.0) + jnp.log(1.0 + jnp.exp(-jnp.abs(xdt)))
        dts.append(dt)
        css.append(_cumsum_mm(cst["triu"] if d else cst["tril"], dt * a_neg[:, sl]))
    tots = [cs[0:1, :] if d else cs[n - 1:n, :] for cs, (ci, d) in zip(css, pairs)]
    csrs = [pltpu.roll(cs, SSM_HEADDIM, 1) for cs in css]
    csts = [cs.T for cs in css]
    xds = [xss[ci] * dt for dt, (ci, d) in zip(dts, pairs)]
    lhs, rhs = [], []
    for pi, (ci, d) in enumerate(pairs):
        for j in range(2):
            colb = jnp.where(lo, css[pi], csrs[pi]) if j == 0 else jnp.where(lo, csrs[pi], css[pi])
            rowb = jnp.broadcast_to(csts[pi][j * SSM_HEADDIM:j * SSM_HEADDIM + 1, :], (n, n))
            lmat = jnp.where(cst["tri_mask_b" if d else "tri_mask_f"], jnp.exp2(jnp.minimum(colb - rowb, 0.0)), 0.0)
            lhs.append((gmats[ci] * lmat).astype(BF16))
            rhs.append(jnp.where(lo if j == 0 else hi, xds[pi], 0.0).astype(BF16))
    prods = [_dot(a, b) for a, b in zip(lhs, rhs)]
    xins = [(xd * jnp.exp2(tot - cs)).astype(BF16) for xd, tot, cs in zip(xds, tots, css)]
    upds = [_dot_tn(bms[ci], xin) for xin, (ci, d) in zip(xins, pairs)]
    res = []
    for ci in range(len(chunks)):
        y = prods[4 * ci] + prods[4 * ci + 1] + prods[4 * ci + 2] + prods[4 * ci + 3]
        outs = [(jnp.exp2(css[2 * ci + d]), upds[2 * ci + d], jnp.exp2(tots[2 * ci + d])) for d in range(2)]
        res.append((y, cms[ci], outs))
    return res


def _ssd_kernel(z_ref, xbc_ref, dt_ref, cw_ref, cb_ref, dtb_ref, alog_ref, dsk_ref, ng_ref,
                y_ref, xa_ref, ya_ref, cm_ref, dof_ref, dob_ref, updf_ref, updb_ref, decf_ref, decb_ref, *, n_lat):
    c = SSM_CHUNK
    n_all = z_ref.shape[1]
    nc = n_all // c
    ncc = (n_all - n_lat) // c
    nl = nc - ncc
    ri = lax.broadcasted_iota(jnp.int32, (c, c), 0)
    ci_ = lax.broadcasted_iota(jnp.int32, (c, c), 1)
    cst = {
        "tril": jnp.where(ci_ <= ri, 1.0, 0.0).astype(BF16),
        "triu": jnp.where(ci_ >= ri, 1.0, 0.0).astype(BF16),
        "tri_mask_f": ci_ <= ri,
        "tri_mask_b": ci_ >= ri,
        "lane": lax.broadcasted_iota(jnp.int32, (c, LANES), 1),
    }
    dtb = dtb_ref[...]
    a_neg = -jnp.exp(alog_ref[...]) * float(np.log2(np.e))
    cb = cb_ref[...]
    halo = SUBLANES

    def conv_body(ci, carry):
        r0 = pl.multiple_of(ci * c, c)
        cur = xbc_ref[0, pl.ds(r0, c), :]
        first = jnp.logical_or(ci == 0, ci == nl)
        last = jnp.logical_or(ci == nl - 1, ci == nc - 1)
        rp = pl.multiple_of(jnp.maximum(r0 - halo, 0), halo)
        rn = pl.multiple_of(jnp.minimum(r0 + c, n_all - halo), halo)
        prev = xbc_ref[0, pl.ds(rp, halo), :] * jnp.where(first, 0.0, 1.0)
        nxt = xbc_ref[0, pl.ds(rn, halo), :] * jnp.where(last, 0.0, 1.0)
        ext = jnp.concatenate([prev, cur, nxt], axis=0)
        acc = jnp.broadcast_to(cb, (c, 2 * LANES))
        for j in range(SSM_CONV):
            s = (SSM_CONV // 2 - j) % (c + 2 * halo)
            sh = ext if s == 0 else pltpu.roll(ext, s, 0)
            acc = acc + cw_ref[j:j + 1, :] * sh[halo:halo + c, :]
        xa_ref[pl.ds(r0, c), :] = _silu(acc)
        return carry

    lax.fori_loop(0, nc, conv_body, 0)

    def intra(i, carry, nb, first):
        cis = [first + i * nb + u for u in range(nb)]
        rws = [pl.ds(pl.multiple_of(ci * c, c), c) for ci in cis]
        res = _ssd_intra([(xa_ref[rows, :], dt_ref[0, rows, :]) for rows in rws], cst=cst, dtb=dtb, a_neg=a_neg)
        for ci, rows, (y, cm, outs) in zip(cis, rws, res):
            ya_ref[rows, :] = y
            cm_ref[rows, :] = cm
            for (dec_out, upd, dec), do_ref, upd_ref, dec_ref in zip(outs, (dof_ref, dob_ref), (updf_ref, updb_ref), (decf_ref, decb_ref)):
                do_ref[rows, :] = dec_out
                upd_ref[ci] = upd
                dec_ref[ci] = jnp.broadcast_to(dec, (SUBLANES, LANES))
        return carry

    nbl = SSM_BATCH if nl % SSM_BATCH == 0 else 1
    nbc = ncc if ncc <= SSM_BATCH else 1
    lax.fori_loop(0, nl // nbl, functools.partial(intra, nb=nbl, first=0), 0)
    lax.fori_loop(0, ncc // nbc, functools.partial(intra, nb=nbc, first=nl), 0)

    def scan(i, carry):
        st_f, st_b = carry
        jf = jnp.where(i < ncc, nl + i, i - ncc)
        rows = pl.ds(pl.multiple_of(jf * c, c), c)
        ya_ref[rows, :] = ya_ref[rows, :] + _dot(cm_ref[rows, :], st_f.astype(BF16)) * dof_ref[rows, :]
        st_f = st_f * decf_ref[jf][0:1, :] + updf_ref[jf]
        ib = nc - 1 - i
        rows = pl.ds(pl.multiple_of(ib * c, c), c)
        ya_ref[rows, :] = ya_ref[rows, :] + _dot(cm_ref[rows, :], st_b.astype(BF16)) * dob_ref[rows, :]
        st_b = st_b * decb_ref[ib][0:1, :] + updb_ref[ib]
        return st_f, st_b

    zst = jnp.zeros((LANES, LANES), F32)
    lax.fori_loop(0, nc, scan, (zst, zst), unroll=2)

    dsk = dsk_ref[...]
    ng = ng_ref[...]
    tr = ROW_TILE

    def readout(i, carry):
        rows = pl.ds(pl.multiple_of(i * tr, tr), tr)
        y = ya_ref[rows, :] + dsk * xa_ref[rows, 0:LANES]
        yz = y * _silu(z_ref[0, rows, :])
        y_ref[0, rows, :] = _rms(yz, ng).astype(y_ref.dtype)
        return carry

    lax.fori_loop(0, n_all // tr, readout, 0)


def _ssd_call(p_ssm, conv_w, conv_b, dtb, alog, dsk, ng, n_lat):
    b, n_all, _ = p_ssm.shape
    assert SSM_CHUNK == LANES
    nc = n_all // SSM_CHUNK
    vec = pl.BlockSpec((1, LANES), lambda bb, g: (0, g))
    vec2 = pl.BlockSpec((1, 2 * LANES), lambda bb, g: (0, g))
    scratch = [pltpu.VMEM((n_all, 2 * LANES), F32), pltpu.VMEM((n_all, LANES), F32), pltpu.VMEM((n_all, LANES), BF16),
               pltpu.VMEM((n_all, LANES), F32), pltpu.VMEM((n_all, LANES), F32),
               pltpu.VMEM((nc, LANES, LANES), F32), pltpu.VMEM((nc, LANES, LANES), F32),
               pltpu.VMEM((nc, SUBLANES, LANES), F32), pltpu.VMEM((nc, SUBLANES, LANES), F32)]
    est = 2 * (n_all * 5 * LANES * 4 + n_all * LANES * 2) + n_all * 6 * LANES * 4 + 2 * nc * LANES * LANES * 4 + 8 * 1024 * 1024
    return pl.pallas_call(
        functools.partial(_ssd_kernel, n_lat=n_lat),
        grid=(b, SSM_GROUPS),
        in_specs=[
            pl.BlockSpec((1, n_all, LANES), lambda bb, g: (bb, 0, g)),
            pl.BlockSpec((1, n_all, 2 * LANES), lambda bb, g: (bb, 0, 1 + g)),
            pl.BlockSpec((1, n_all, 2 * LANES), lambda bb, g: (bb, 0, 3 + g)),
            pl.BlockSpec((SSM_CONV, 2 * LANES), lambda bb, g: (0, g)),
            pl.BlockSpec((1, 2 * LANES), lambda bb, g: (0, g)),
            vec2, vec2, vec, vec,
        ],
        out_specs=pl.BlockSpec((1, n_all, LANES), lambda bb, g: (bb, 0, g)),
        out_shape=jax.ShapeDtypeStruct((b, n_all, SSM_INNER), BF16),
        scratch_shapes=scratch,
        compiler_params=pltpu.CompilerParams(vmem_limit_bytes=_vmem_limit(est)),
        name="ssd_bidir",
    )(p_ssm, p_ssm, p_ssm, conv_w, conv_b, dtb, alog, dsk, ng)


def _ffn_kernel(x_ref, a_ref, r_ref, s_ref, mod_ref, g2_ref, gf_ref, wa_ref, wr_ref, ws_ref, w1_ref, w2_ref,
                o_ref, *, hid_cuts, final, tail_rows):
    d = x_ref.shape[2]
    hidden = w2_ref.shape[0]
    mod = lambda j: mod_ref[0, 0, :, j * d:(j + 1) * d]

    def tile(rows):
        r = slice(0, rows)
        mix = _dot(a_ref[0, r, :], wa_ref[...]) + _dot(r_ref[0, r, :], wr_ref[...]) + _dot(s_ref[0, r, :], ws_ref[...])
        x1 = x_ref[0, r, :] + mod(2) * mix
        h2 = (_rms(x1, g2_ref[...]) * (1.0 + mod(4)) + mod(3)).astype(BF16)
        y = None
        for c0, c1 in zip(hid_cuts[:-1], hid_cuts[1:]):
            ha = _dot(h2, w1_ref[:, c0:c1])
            hb = _dot(h2, w1_ref[:, hidden + c0:hidden + c1])
            act = (_silu(ha) * hb).astype(BF16)
            yc = _dot(act, w2_ref[c0:c1, :])
            y = yc if y is None else y + yc
        x2 = x1 + mod(5) * y
        o_ref[0, r, :] = _rms(x2, gf_ref[...]) if final else x2

    _full_or_tail(tile, x_ref.shape[1], tail_rows, pl.program_id(1), pl.num_programs(1))


def _ffn_call(x_all, a, r, s, modsel, g2, gf, wa, wr, ws, w1, w2, n_lat, final):
    b, n_all, d = x_all.shape
    tm = TOKEN_TILE
    hidden = w2.shape[0]
    n_tiles = hidden // MXU_TILE
    assert hidden % MXU_TILE == 0
    hid_cuts = tuple(MXU_TILE * ((n_tiles * j + FFN_SPLIT - 1) // FFN_SPLIT) for j in range(FFN_SPLIT + 1))
    hid_chunk = max(b1 - b0 for b0, b1 in zip(hid_cuts[:-1], hid_cuts[1:]))
    nlt = n_lat // tm
    n_out = n_lat if final else n_all
    full = lambda arr: pl.BlockSpec(arr.shape, lambda bb, i: (0,) * arr.ndim, pipeline_mode=pl.Buffered(1))
    tok = lambda w: pl.BlockSpec((1, tm, w), lambda bb, i: (bb, i, 0))
    est = (wa.size + wr.size + ws.size + w1.size + w2.size) * 2 + 2 * (2 * tm * d * 4 + tm * 1024 * 2) \
        + 5 * tm * hid_chunk * 4 + 5 * tm * d * 4
    return pl.pallas_call(
        functools.partial(_ffn_kernel, hid_cuts=hid_cuts, final=final,
                          tail_rows=n_out - (pl.cdiv(n_out, tm) - 1) * tm),
        grid=(b, pl.cdiv(n_out, tm)),
        in_specs=[
            tok(d), tok(a.shape[2]), tok(r.shape[2]), tok(s.shape[2]),
            pl.BlockSpec((1, 1, 1, modsel.shape[3]), lambda bb, i: (bb, jnp.minimum(i // nlt, 1), 0, 0)),
            pl.BlockSpec((1, d), lambda bb, i: (0, 0)),
            pl.BlockSpec((1, d), lambda bb, i: (0, 0)),
            full(wa), full(wr), full(ws), full(w1), full(w2),
        ],
        out_specs=tok(d),
        out_shape=jax.ShapeDtypeStruct((b, n_out, d), F32),
        compiler_params=pltpu.CompilerParams(vmem_limit_bytes=_vmem_limit(est)),
        name="outproj_ffn",
    )(x_all, a, r, s, modsel, g2, gf, wa, wr, ws, w1, w2)


def _win_perm():
    perm = np.full((P_TOTAL,), -1, np.int64)
    perm[P_QLAT:P_QLAT + MLA_Q_RANK + MLA_KV_RANK] = np.arange(MLA_Q_RANK + MLA_KV_RANK)
    perm[P_KR + MLA_NOPE:P_KR + MLA_NOPE + ROPE_LANES] = MLA_Q_RANK + MLA_KV_RANK + ROPE_PERM
    perm[P_HG:P_HG + HG_COLS] = MLA_COLS + np.arange(HG_COLS)
    o2 = MLA_COLS + HG_COLS
    perm[P_SSM:P_SSM + SSM_INNER] = o2 + np.arange(SSM_INNER)
    xo = o2 + SSM_INNER
    bo = xo + SSM_INNER
    co = bo + SSM_GROUPS * SSM_STATE
    dto = o2 + SSM_INNER + SSM_XBC
    for g in range(SSM_GROUPS):
        base = P_SSM + SSM_INNER + g * 2 * LANES
        perm[base:base + LANES] = xo + g * LANES + np.arange(LANES)
        perm[base + LANES:base + LANES + SSM_STATE] = bo + g * SSM_STATE + np.arange(SSM_STATE)
        perm[base + LANES + SSM_STATE:base + 2 * LANES] = co + g * SSM_STATE + np.arange(SSM_STATE)
        dbase = P_SSM + SSM_INNER + 2 * 2 * LANES + g * 2 * LANES
        for d in range(2):
            for j in range(2):
                lo = dbase + d * LANES + j * SSM_HEADDIM
                perm[lo:lo + SSM_HEADDIM] = dto + d * SSM_HEADS + 2 * g + j
    return perm


def _conv_perm():
    perm = np.zeros((SSM_XBC,), np.int64)
    for g in range(SSM_GROUPS):
        base = g * 2 * LANES
        perm[base:base + LANES] = g * LANES + np.arange(LANES)
        perm[base + LANES:base + LANES + SSM_STATE] = SSM_INNER + g * SSM_STATE + np.arange(SSM_STATE)
        perm[base + LANES + SSM_STATE:base + 2 * LANES] = SSM_INNER + SSM_GROUPS * SSM_STATE + g * SSM_STATE + np.arange(SSM_STATE)
    return perm


def _gather_cols(w, perm):
    idx = jnp.asarray(np.maximum(perm, 0), jnp.int32)
    out = jnp.take(w, idx, axis=-1)
    return jnp.where(jnp.asarray(perm >= 0), out, 0.0)


def _head_vec(v):
    depth = v.shape[0]
    v5 = v.reshape(depth, 2, SSM_GROUPS, 2, 1)
    v5 = jnp.broadcast_to(v5, (depth, 2, SSM_GROUPS, 2, SSM_HEADDIM))
    return jnp.transpose(v5, (0, 2, 1, 3, 4)).reshape(depth, 1, SSM_GROUPS * 2 * LANES)


def _rope_tables(n_ctx, n_lat, scale):
    rows = n_lat // GRID_W
    row = jnp.repeat(jnp.arange(rows, dtype=F32), GRID_W)
    col = jnp.tile(jnp.arange(GRID_W, dtype=F32), rows)
    n_freq = MLA_ROPE // 4
    inv = ROPE_BASE ** (-jnp.arange(n_freq, dtype=F32) / n_freq)
    ang = jnp.stack([row[:, None] * inv, col[:, None] * inv], axis=1)
    cos, sin = jnp.cos(ang), jnp.sin(ang)
    zf = jnp.zeros((n_lat, ROPE_F), F32)
    c_r = jnp.concatenate([cos[:, 0], cos[:, 0], zf, cos[:, 1], cos[:, 1], zf], axis=-1)
    s_r = jnp.concatenate([-sin[:, 0], sin[:, 0], zf, -sin[:, 1], sin[:, 1], zf], axis=-1)
    pad = jnp.zeros((n_lat, LANES - MLA_NOPE - ROPE_LANES), F32)
    c_lat = jnp.concatenate([jnp.ones((n_lat, MLA_NOPE), F32), c_r, pad], axis=-1)
    s_lat = jnp.concatenate([jnp.zeros((n_lat, MLA_NOPE), F32), s_r, pad], axis=-1)
    keep = jnp.asarray(np.concatenate([np.ones(MLA_NOPE), np.tile(np.repeat([1.0, 1.0, 0.0], ROPE_F), 2),
                                       np.zeros(LANES - MLA_NOPE - ROPE_LANES)]), F32)[None, :]
    c_all = jnp.concatenate([c_lat, jnp.broadcast_to(keep, (n_ctx, LANES))], axis=0)
    s_all = jnp.concatenate([s_lat, jnp.zeros((n_ctx, LANES), F32)], axis=0)
    return c_all * scale, s_all * scale, c_all, s_all


def kernel(x, c, ctx, c_ctx, w_ada, b_ada, norm1_g, norm2_g, w_in, mla_qa_g, mla_wqb, mla_kva_g, mla_wkvb, hg_lb_logits, hg_norm_g, ssm_conv_w, ssm_conv_b, ssm_dt_bias, ssm_a_log, ssm_d, ssm_norm_g, w_out, w_ffn_in, w_ffn_out, final_g):
    bsz, n_lat, d = x.shape
    n_ctx = ctx.shape[1]
    depth = w_ada.shape[0]
    assert n_lat % TOKEN_TILE == 0 and n_lat % GRID_W == 0 and n_ctx <= TOKEN_TILE
    assert n_ctx % (HG_BATCH * HG_CHUNK) == 0 and n_ctx % SSM_CHUNK == 0 and n_ctx % ROW_TILE == 0

    rows = -(-(bsz + 1) // SUBLANES) * SUBLANES
    cc = jnp.zeros((rows, d), F32).at[:bsz].set(c).at[bsz].set(c_ctx)
    mods = _mod_call(cc, w_ada.astype(BF16), b_ada.reshape(depth, 1, 6 * d))
    modsel = jnp.stack([mods[:, :bsz], jnp.broadcast_to(mods[:, bsz:bsz + 1], (depth, bsz, 6 * d))], axis=2)[:, :, :, None, :]

    w_in_p = _gather_cols(w_in, _win_perm()).astype(BF16)
    head_cols = np.full((LANES,), -1, np.int64)
    head_cols[:MLA_NOPE] = np.arange(MLA_NOPE)
    head_cols[MLA_NOPE:MLA_NOPE + ROPE_LANES] = MLA_NOPE + ROPE_PERM
    wqb_perm = np.concatenate([np.where(head_cols >= 0, hh * MLA_QK + head_cols, -1) for hh in range(MLA_HEADS)])
    wqb_p = _gather_cols(mla_wqb, wqb_perm).astype(BF16)
    wkvb_b = mla_wkvb.astype(BF16)
    cperm = _conv_perm()
    conv_w_p = jnp.take(ssm_conv_w, jnp.asarray(cperm, jnp.int32), axis=-1)
    conv_b_p = jnp.take(ssm_conv_b, jnp.asarray(cperm, jnp.int32), axis=-1).reshape(depth, 1, SSM_XBC)
    dtb_p = _head_vec(ssm_dt_bias)
    alog_p = _head_vec(ssm_a_log)
    dsk_p = jnp.repeat(ssm_d, SSM_HEADDIM, axis=-1).reshape(depth, 1, SSM_INNER)
    lb_soft = jax.nn.softmax(hg_lb_logits.astype(F32), axis=0)
    lb_all = jnp.cumsum(lb_soft, axis=0) - lb_soft[0]
    w_out_b = w_out.astype(BF16)
    a_w = MLA_HEADS * MLA_V
    r_w = HG_HEADS * HG_DV
    w1_b = w_ffn_in.astype(BF16)
    w2_b = w_ffn_out.astype(BF16)
    tabs = _rope_tables(n_ctx, n_lat, MLA_QK ** -0.5 * float(np.log2(np.e)))

    x_all = jnp.concatenate([x, ctx], axis=1)
    for l in range(depth):
        q, k, v, p_hg, p_ssm = _inproj_call(
            x_all, modsel[l], norm1_g[l].reshape(1, d), w_in_p[l], mla_qa_g[l].reshape(1, -1),
            mla_kva_g[l].reshape(1, -1), wqb_p[l], wkvb_b[l], tabs, n_lat)
        a = _attn_call(q, k, v, n_lat)
        r = _hgrn_call(p_hg, lb_all[l], hg_norm_g[l].reshape(1, -1), n_lat)
        s = _ssd_call(p_ssm, conv_w_p[l], conv_b_p[l], dtb_p[l], alog_p[l], dsk_p[l],
                      ssm_norm_g[l].reshape(1, -1), n_lat)
        x_all = _ffn_call(x_all, a, r, s, modsel[l], norm2_g[l].reshape(1, d), final_g.reshape(1, d),
                          w_out_b[l, :a_w], w_out_b[l, a_w:a_w + r_w], w_out_b[l, a_w + r_w:], w1_b[l], w2_b[l],
                          n_lat, final=(l == depth - 1))
    return x_all
```

```python
import functools

import numpy as np
import jax
import jax.numpy as jnp
from jax import lax
from jax.experimental import pallas as pl
from jax.experimental.pallas import tpu as pltpu

F32 = jnp.float32
BF16 = jnp.bfloat16
EPS = 1e-6

LANES = 128
SUBLANES = 8
MXU_TILE = 256
VMEM_BYTES = 64 * 1024 * 1024

GRID_W = 64
MLA_HEADS = 8
MLA_Q_RANK = 384
MLA_KV_RANK = 256
MLA_NOPE = 64
MLA_ROPE = 32
MLA_V = 64
MLA_QK = MLA_NOPE + MLA_ROPE
ROPE_BASE = 10000.0
HG_HEADS = 4
HG_DK = 64
HG_DV = 64
HG_W = HG_HEADS * HG_DK
SSM_HEADS = 4
SSM_HEADDIM = 64
SSM_GROUPS = 2
SSM_STATE = 64
SSM_CONV = 5
SSM_INNER = SSM_HEADS * SSM_HEADDIM
SSM_XBC = SSM_INNER + 2 * SSM_GROUPS * SSM_STATE

ROPE_F = MLA_ROPE // 4
ROPE_PERM = np.concatenate([np.concatenate([a * 2 * ROPE_F + np.arange(2 * ROPE_F), a * 2 * ROPE_F + np.arange(ROPE_F)])
                            for a in range(2)])
ROPE_LANES = ROPE_PERM.size

MLA_COLS = MLA_Q_RANK + MLA_KV_RANK + MLA_ROPE
HG_COLS = 3 * HG_W + 2 * HG_HEADS * HG_DV
SSM_COLS = SSM_INNER + SSM_XBC + 2 * SSM_HEADS

P_QLAT = 0
P_KVLAT = MLA_Q_RANK
P_KR = P_KVLAT + MLA_KV_RANK
P_HG = P_KR + LANES
P_HGF = P_HG + 3 * HG_W
P_SSM = P_HG + HG_COLS
P_DT = P_SSM + SSM_INNER + 2 * 2 * LANES
SSM_BLOCK = SSM_INNER + 2 * 2 * LANES + 2 * 2 * LANES
P_TOTAL = P_SSM + SSM_BLOCK

TOKEN_TILE = 512
ROW_TILE = 256
HG_CHUNK = 128
HG_BATCH = 2
SSM_CHUNK = 128
SSM_BATCH = 4
FFN_SPLIT = 2

NT_DIMS = (((1,), (1,)), ((), ()))
TN_DIMS = (((0,), (0,)), ((), ()))


def _vmem_limit(nbytes):
    return int(min(VMEM_BYTES - 8 * 1024 * 1024, max(nbytes, 16 * 1024 * 1024)))


def _sigmoid_pair(x):
    e = jnp.exp(-jnp.abs(x))
    d = 1.0 / (1.0 + e)
    ed = e * d
    pos = x >= 0
    return jnp.where(pos, d, ed), jnp.where(pos, ed, d)


def _silu(x):
    return x / (1.0 + jnp.exp(-x))


def _rms(x, g):
    ms = jnp.mean(x * x, axis=-1, keepdims=True)
    return x * lax.rsqrt(ms + EPS) * g


def _dot(a, b):
    return jnp.dot(a, b, preferred_element_type=F32)


def _dot_nt(a, b):
    return lax.dot_general(a, b, NT_DIMS, preferred_element_type=F32)


def _dot_tn(a, b):
    return lax.dot_general(a, b, TN_DIMS, preferred_element_type=F32)


def _full_or_tail(tile, tm, tail_rows, i, n):
    if tail_rows == tm:
        tile(tm)
        return
    last = i == n - 1
    pl.when(last)(lambda: tile(tail_rows))
    pl.when(jnp.logical_not(last))(lambda: tile(tm))


def _split3(x):
    x1 = x.astype(BF16)
    r1 = x - x1.astype(F32)
    x2 = r1.astype(BF16)
    x3 = (r1 - x2.astype(F32)).astype(BF16)
    return x1, x2, x3


def _cumsum_mm(tri, x):
    x1, x2, x3 = _split3(x)
    return _dot(tri, x1) + _dot(tri, x2) + _dot(tri, x3)


def _mod_kernel(c_ref, w_ref, b_ref, o_ref):
    s = _silu(c_ref[...]).astype(BF16)
    o_ref[0] = _dot(s, w_ref[0]) + b_ref[0]


def _mod_call(cc, w_ada, b_ada):
    depth, d, six_d = w_ada.shape
    rows = cc.shape[0]
    tn = 1536
    return pl.pallas_call(
        _mod_kernel,
        grid=(depth, six_d // tn),
        in_specs=[
            pl.BlockSpec((rows, d), lambda l, j: (0, 0)),
            pl.BlockSpec((1, d, tn), lambda l, j: (l, 0, j)),
            pl.BlockSpec((1, 1, tn), lambda l, j: (l, 0, j)),
        ],
        out_specs=pl.BlockSpec((1, rows, tn), lambda l, j: (l, 0, j)),
        out_shape=jax.ShapeDtypeStruct((depth, rows, six_d), F32),
        name="adaln_mod",
    )(cc, w_ada, b_ada)


def _rope(xh, c, s):
    return xh * c + pltpu.roll(xh, LANES - ROPE_F, 1) * s


def _inproj_kernel(x_ref, mod_ref, g1_ref, win_ref, qag_ref, kvag_ref, wqb_ref, wkvb_ref,
                   cq_ref, sq_ref, ck_ref, sk_ref,
                   q_ref, k_ref, v_ref, hgb_ref, hgf_ref, ssb_ref, ssf_ref, *, tail_rows):
    d = x_ref.shape[2]

    def tile(rows):
        r = slice(0, rows)
        x = x_ref[0, r, :]
        sh = mod_ref[0, 0, :, 0:d]
        sc = mod_ref[0, 0, :, d:2 * d]
        h = _rms(x, g1_ref[...]) * (1.0 + sc) + sh
        p = _dot(h.astype(BF16), win_ref[...])
        hgb_ref[0, r, :] = p[:, P_HG:P_HGF].astype(hgb_ref.dtype)
        hgf_ref[0, r, :] = p[:, P_HGF:P_SSM]
        ssb_ref[0, r, :] = p[:, P_SSM:P_DT].astype(ssb_ref.dtype)
        ssf_ref[0, r, :] = p[:, P_DT:P_TOTAL]
        q_lat = _rms(p[:, P_QLAT:P_KVLAT], qag_ref[...])
        kv_lat = _rms(p[:, P_KVLAT:P_KR], kvag_ref[...])
        q = _dot(q_lat.astype(BF16), wqb_ref[...])
        kv = _dot(kv_lat.astype(BF16), wkvb_ref[...])
        lane = lax.broadcasted_iota(jnp.int32, (rows, LANES), 1)
        cq, sq, ck, sk = cq_ref[r, :], sq_ref[r, :], ck_ref[r, :], sk_ref[r, :]
        kr = _rope(p[:, P_KR:P_HG], ck, sk)
        nope = lane < MLA_NOPE
        for hh in range(MLA_HEADS):
            sl = slice(hh * LANES, (hh + 1) * LANES)
            q_ref[0, hh, r, :] = _rope(q[:, sl], cq, sq).astype(q_ref.dtype)
            kvh = kv[:, sl]
            k_ref[0, hh, r, :] = jnp.where(nope, kvh, kr).astype(k_ref.dtype)
            v_ref[0, hh, r, :] = jnp.where(nope, 1.0, kvh).astype(v_ref.dtype)

    _full_or_tail(tile, x_ref.shape[1], tail_rows, pl.program_id(1), pl.num_programs(1))


def _inproj_call(x_all, modsel, g1, w_in, qa_g, kva_g, wqb, wkvb, tabs, n_lat):
    b, n_all, d = x_all.shape
    tm = TOKEN_TILE
    nt = pl.cdiv(n_all, tm)
    nlt = n_lat // tm
    full = lambda shape: pl.BlockSpec(shape, lambda bb, i: (0,) * len(shape), pipeline_mode=pl.Buffered(1))
    tab = pl.BlockSpec((tm, LANES), lambda bb, i: (i, 0))
    head_out = pl.BlockSpec((1, MLA_HEADS, tm, LANES), lambda bb, i: (bb, 0, i, 0))
    est = (w_in.size + wqb.size + wkvb.size) * 2 + 2 * (tm * d * 4 + 3 * MLA_HEADS * tm * LANES * 2
                                                        + tm * (HG_COLS + SSM_BLOCK) * 4) + 4 * tm * P_TOTAL * 4
    return pl.pallas_call(
        functools.partial(_inproj_kernel, tail_rows=n_all - (nt - 1) * tm),
        grid=(b, nt),
        in_specs=[
            pl.BlockSpec((1, tm, d), lambda bb, i: (bb, i, 0)),
            pl.BlockSpec((1, 1, 1, modsel.shape[3]), lambda bb, i: (bb, jnp.minimum(i // nlt, 1), 0, 0)),
            full((1, d)),
            full(w_in.shape),
            full((1, MLA_Q_RANK)),
            full((1, MLA_KV_RANK)),
            full(wqb.shape),
            full(wkvb.shape),
            tab, tab, tab, tab,
        ],
        out_specs=[
            head_out, head_out, head_out,
            pl.BlockSpec((1, tm, P_HGF - P_HG), lambda bb, i: (bb, i, 0)),
            pl.BlockSpec((1, tm, P_SSM - P_HGF), lambda bb, i: (bb, i, 0)),
            pl.BlockSpec((1, tm, P_DT - P_SSM), lambda bb, i: (bb, i, 0)),
            pl.BlockSpec((1, tm, P_TOTAL - P_DT), lambda bb, i: (bb, i, 0)),
        ],
        out_shape=[
            jax.ShapeDtypeStruct((b, MLA_HEADS, n_all, LANES), BF16),
            jax.ShapeDtypeStruct((b, MLA_HEADS, n_all, LANES), BF16),
            jax.ShapeDtypeStruct((b, MLA_HEADS, n_all, LANES), BF16),
            jax.ShapeDtypeStruct((b, n_all, P_HGF - P_HG), BF16),
            jax.ShapeDtypeStruct((b, n_all, P_SSM - P_HGF), F32),
            jax.ShapeDtypeStruct((b, n_all, P_DT - P_SSM), BF16),
            jax.ShapeDtypeStruct((b, n_all, P_TOTAL - P_DT), F32),
        ],
        compiler_params=pltpu.CompilerParams(vmem_limit_bytes=_vmem_limit(est)),
        name="inproj_mla",
    )(x_all, modsel, g1, w_in, qa_g, kva_g, wqb, wkvb, *tabs)


def _attn_kernel(q_ref, k_ref, v_ref, o_ref, *, n_lat):
    n_all = k_ref.shape[2]
    tq = q_ref.shape[2]

    def run(k0, k1):
        ss = [_dot_nt(q_ref[0, j], k_ref[0, j, k0:k1, :]) for j in range(2)]
        ps = [jnp.exp2(s - jnp.max(s, axis=-1, keepdims=True)).astype(BF16) for s in ss]
        outs = []
        for j in range(2):
            o = _dot(ps[j], v_ref[0, j, k0:k1, :])
            outs.append(o / pltpu.roll(o, MLA_V, 1))
        lane = lax.broadcasted_iota(jnp.int32, (tq, LANES), 1)
        o_ref[0] = jnp.where(lane < MLA_V, pltpu.roll(outs[0], MLA_V, 1), outs[1]).astype(o_ref.dtype)

    is_ctx = pl.program_id(2) >= n_lat // tq

    @pl.when(is_ctx)
    def _():
        run(n_lat, n_all)

    @pl.when(jnp.logical_not(is_ctx))
    def _():
        run(0, n_all)


def _attn_call(q, k, v, n_lat):
    b, h, n_all, _ = q.shape
    tq = TOKEN_TILE
    kv_spec = pl.BlockSpec((1, 2, n_all, LANES), lambda bb, hp, i: (bb, hp, 0, 0))
    est = 2 * (2 * tq * LANES * 2 + 2 * 2 * n_all * LANES * 2 + tq * LANES * 2) + 7 * tq * n_all * 4
    return pl.pallas_call(
        functools.partial(_attn_kernel, n_lat=n_lat),
        grid=(b, h // 2, pl.cdiv(n_all, tq)),
        in_specs=[pl.BlockSpec((1, 2, tq, LANES), lambda bb, hp, i: (bb, hp, i, 0)), kv_spec, kv_spec],
        out_specs=pl.BlockSpec((1, tq, LANES), lambda bb, hp, i: (bb, i, hp)),
        out_shape=jax.ShapeDtypeStruct((b, n_all, h * MLA_V), BF16),
        compiler_params=pltpu.CompilerParams(vmem_limit_bytes=_vmem_limit(est)),
        name="mla_attention",
    )(q, k, v)


def _hg_consts(c):
    ri = lax.broadcasted_iota(jnp.int32, (c, c), 0)
    ci = lax.broadcasted_iota(jnp.int32, (c, c), 1)
    cst = {
        "tril": jnp.where(ci <= ri, 1.0, 0.0).astype(BF16),
        "triu": jnp.where(ci >= ri, 1.0, 0.0).astype(BF16),
        "lvl": {},
    }
    h = SUBLANES
    while 2 * h <= c:
        same =jnp.right_shift(ri, int(np.log2(2 * h))) == jnp.right_shift(ci, int(np.log2(2 * h)))
        r_lo = jnp.bitwise_and(ri, 2 * h - 1) < h
        c_lo = jnp.bitwise_and(ci, 2 * h - 1) < h
        fwd = jnp.where(same, jnp.where(r_lo, 0.0, jnp.where(c_lo, 1.0, 0.0)), 0.0)
        bwd = jnp.where(same, jnp.where(r_lo, jnp.where(c_lo, 0.0, 1.0), 0.0), 0.0)
        cst["lvl"][(h, False)] = fwd
        cst["lvl"][(h, True)] = bwd
        h *= 2
    r2 = lax.broadcasted_iota(jnp.int32, (LANES, LANES), 0)
    c2 = lax.broadcasted_iota(jnp.int32, (LANES, LANES), 1)
    bd = (r2 < HG_DK) == (c2 < HG_DK)
    cst["bd"] = bd
    cst["bo"] = jnp.where(bd, 1.0, 0.0).astype(BF16)
    lane = lax.broadcasted_iota(jnp.int32, (c, LANES), 1)
    cst["hm"] = [lane < HG_DK, lane >= HG_DK]
    cst["rit"] = jnp.bitwise_and(lax.broadcasted_iota(jnp.int32, (c, LANES), 0), SUBLANES - 1)
    return cst


def _hg_level_ref(b, h, rev):
    c = b.shape[0]
    off = h if rev else h - 1
    pieces = [jnp.broadcast_to(b[blk * 2 * h + off:blk * 2 * h + off + 1, :], (2 * h, LANES))
              for blk in range(c // (2 * h))]
    return pieces[0] if len(pieces) == 1 else jnp.concatenate(pieces, axis=0)


def _hg_intra(streams, *, cst):
    c = streams[0][0].shape[0]
    ns = len(streams)
    bs = [_cumsum_mm(cst["triu"] if s[5] else cst["tril"], s[3]) for s in streams]
    tots = [b[0:1, :] if s[5] else b[c - 1:c, :] for b, s in zip(bs, streams)]
    qes = [(s[0] * jnp.exp2(b)).astype(BF16) for b, s in zip(bs, streams)]
    atts = [[None, None] for _ in range(ns)]
    sign = jnp.uint32(0x80000000)
    h = SUBLANES
    while 2 * h <= c:
        ops = []
        for si, (b, (q, k, f, g, v, rev)) in enumerate(zip(bs, streams)):
            ref = _hg_level_ref(b, h, rev)
            nabs = pltpu.bitcast(pltpu.bitcast(b - ref, jnp.uint32) | sign, F32)
            e = jnp.exp2(nabs)
            qh = q * e
            ops.append(([jnp.where(cst["hm"][j], qh, 0.0).astype(BF16) for j in range(2)], (k * e).astype(BF16)))
        prods = [[_dot_nt(qj, kh) for qj in qjs] for qjs, kh in ops]
        for si in range(ns):
            msk = cst["lvl"][(h, streams[si][5])]
            for j in range(2):
                t = prods[si][j] * msk
                atts[si][j] = t if atts[si][j] is None else atts[si][j] + t
        h *= 2
    outs = []
    for si, (q, k, f, g, v, rev) in enumerate(streams):
        o = None
        for j in range(2):
            vj = jnp.where(cst["hm"][j], v, 0.0).astype(BF16)
            oj = _dot(atts[si][j].astype(BF16), vj)
            o = oj if o is None else o + oj
        outs.append(o)

    def shift(x, jj, rev):
        if jj == 0:
            return x
        x3 = x.reshape(c // SUBLANES, SUBLANES, LANES)
        return pltpu.roll(x3, (SUBLANES - jj) if rev else jj, 1).reshape(c, LANES)

    es = [None] * ns
    for dd in range(SUBLANES):
        ws = []
        for si, (q, k, f, g, v, rev) in enumerate(streams):
            if dd == 0:
                w = q * k
            else:
                fd = shift(f, dd - 1, rev)
                es[si] = fd if es[si] is None else es[si] * fd
                valid = (cst["rit"] <= SUBLANES - 1 - dd) if rev else (cst["rit"] >= dd)
                w = jnp.where(valid, q * shift(k, dd, rev) * es[si], 0.0)
            ws.append(w.astype(BF16))
        sums = [_dot(w, cst["bo"]) for w in ws]
        for si, (q, k, f, g, v, rev) in enumerate(streams):
            outs[si] = outs[si] + sums[si] * shift(v, dd, rev)
    res = []
    for si, (q, k, f, g, v, rev) in enumerate(streams):
        ke = (k * jnp.exp2(tots[si] - bs[si])).astype(BF16)
        upd = jnp.where(cst["bd"], _dot_tn(v.astype(BF16), ke), 0.0)
        res.append((outs[si], qes[si], upd, jnp.exp2(tots[si])))
    return res


def _hgrn_kernel(q_ref, fff_ref, ffb_ref, iv_ref, og_ref, lb_ref, ng_ref, r_ref,
                 o_ref, qef_ref, qeb_ref, updf_ref, updb_ref, decf_ref, decb_ref, *, n_lat):
    c = HG_CHUNK
    n_all = q_ref.shape[1]
    nc = n_all // c
    ncc = (n_all - n_lat) // c
    cst = _hg_consts(c)
    bo = cst["bo"]
    ng = ng_ref[...]
    tr = ROW_TILE
    dirs = ((False, fff_ref, lb_ref[0:1, :], qef_ref, updf_ref, decf_ref),
            (True, ffb_ref, lb_ref[1:2, :], qeb_ref, updb_ref, decb_ref))
    nb = HG_BATCH
    assert nc % nb == 0

    def intra(i, carry):
        streams, where = [], []
        for u in range(nb):
            ci = i * nb + u
            rows = pl.ds(pl.multiple_of(ci * c, c), c)
            q = _silu(q_ref[0, rows, :].astype(F32))
            v = iv_ref[0, rows, :].astype(F32)
            for rev, ff_ref, lb, qe_ref, upd_ref, dec_ref in dirs:
                sig, sigm = _sigmoid_pair(ff_ref[0, rows, :])
                f = lb + (1.0 - lb) * sig
                streams.append((q, (1.0 - lb) * sigm, f, jnp.log2(f), v, rev))
                where.append((ci, rows, qe_ref, upd_ref, dec_ref))
        res = _hg_intra(streams, cst=cst)
        for u in range(nb):
            o_ref[where[2 * u][1], :] = res[2 * u][0] + res[2 * u + 1][0]
        for (_, qe, upd, dec), (ci, rows, qe_ref, upd_ref, dec_ref) in zip(res, where):
            qe_ref[rows, :] = qe
            upd_ref[ci] = upd
            dec_ref[ci] = jnp.broadcast_to(dec, (SUBLANES, LANES))
        return carry

    lax.fori_loop(0, nc // nb, intra, 0)

    def scan(i, carry):
        st_f, st_b = carry
        jf = jnp.where(i < ncc, nc - ncc + i, i - ncc)
        rows = pl.ds(pl.multiple_of(jf * c, c), c)
        o_ref[rows, :] = o_ref[rows, :] + _dot_nt(qef_ref[rows, :], st_f.astype(BF16))
        st_f = st_f * decf_ref[jf][0:1, :] + updf_ref[jf]
        ib = nc - 1 - i
        rows = pl.ds(pl.multiple_of(ib * c, c), c)
        o_ref[rows, :] = o_ref[rows, :] + _dot_nt(qeb_ref[rows, :], st_b.astype(BF16))
        st_b = st_b * decb_ref[ib][0:1, :] + updb_ref[ib]
        return st_f, st_b

    zero = jnp.zeros((LANES, LANES), F32)
    lax.fori_loop(0, nc, scan, (zero, zero), unroll=2)

    def readout(i, carry):
        rows = pl.ds(pl.multiple_of(i * tr, tr), tr)
        o = o_ref[rows, :]
        ms = _dot((o * o).astype(BF16), bo) * (1.0 / HG_DV)
        r = o * lax.rsqrt(ms + EPS) * ng * _silu(og_ref[0, rows, :].astype(F32))
        r_ref[0, rows, :] = r.astype(r_ref.dtype)
        return carry

    lax.fori_loop(0, n_all // tr, readout, 0)


def _hgrn_call(hg_b, hg_f, lb, ng, n_lat):
    b, n_all, _ = hg_b.shape
    nc = n_all // HG_CHUNK
    col = lambda j: pl.BlockSpec((1, n_all, LANES), lambda bb, pr: (bb, 0, 2 * j + pr))
    scratch = [pltpu.VMEM((n_all, LANES), F32), pltpu.VMEM((n_all, LANES), BF16), pltpu.VMEM((n_all, LANES), BF16),
               pltpu.VMEM((nc, LANES, LANES), F32), pltpu.VMEM((nc, LANES, LANES), F32),
               pltpu.VMEM((nc, SUBLANES, LANES), F32), pltpu.VMEM((nc, SUBLANES, LANES), F32)]
    est = 2 * 6 * n_all * LANES * 4 + 2 * n_all * LANES * 4 + 2 * nc * LANES * LANES * 4 + 8 * 1024 * 1024
    return pl.pallas_call(
        functools.partial(_hgrn_kernel, n_lat=n_lat),
        grid=(b, 2),
        in_specs=[col(0), col(0), col(1), col(1), col(2),
                  pl.BlockSpec((2, LANES), lambda bb, pr: (0, pr)),
                  pl.BlockSpec((1, LANES), lambda bb, pr: (0, pr))],
        out_specs=pl.BlockSpec((1, n_all, LANES), lambda bb, pr: (bb, 0, pr)),
        out_shape=jax.ShapeDtypeStruct((b, n_all, HG_HEADS * HG_DV), BF16),
        scratch_shapes=scratch,
        compiler_params=pltpu.CompilerParams(vmem_limit_bytes=_vmem_limit(est)),
        name="hgrn2_bidir",
    )(hg_b, hg_f, hg_f, hg_b, hg_b, lb, ng)


def _ssd_intra(chunks, *, cst, dtb, a_neg):
    n = chunks[0][0].shape[0]
    lo = cst["lane"] < SSM_STATE
    hi = jnp.logical_not(lo)
    xss = [xa[:, 0:LANES] for xa, _ in chunks]
    bms, cms = [], []
    for xa, _ in chunks:
        bc = xa[:, LANES:2 * LANES]
        bms.append(jnp.where(lo, bc, 0.0).astype(BF16))
        cms.append(jnp.where(lo, pltpu.roll(bc, SSM_STATE, 1), 0.0).astype(BF16))
    gmats = [_dot_nt(cm, bm) for cm, bm in zip(cms, bms)]
    pairs = [(ci, d) for ci in range(len(chunks)) for d in range(2)]
    dts, css = [], []
    for ci, d in pairs:
        sl = slice(d * LANES, (d + 1) * LANES)
        xdt = chunks[ci][1][:, sl] + dtb[:, sl]
        dt = jnp.maximum(xdt, 0.0) + jnp.log(1.0 + jnp.exp(-jnp.abs(xdt)))
        dts.append(dt)
        css.append(_cumsum_mm(cst["triu"] if d else cst["tril"], dt * a_neg[:, sl]))
    tots = [cs[0:1, :] if d else cs[n - 1:n, :] for cs, (ci, d) in zip(css, pairs)]
    csrs = [pltpu.roll(cs, SSM_HEADDIM, 1) for cs in css]
    csts = [cs.T for cs in css]
    xds = [xss[ci] * dt for dt, (ci, d) in zip(dts, pairs)]
    lhs, rhs = [], []
    for pi, (ci, d) in enumerate(pairs):
        for j in range(2):
            colb = jnp.where(lo, css[pi], csrs[pi]) if j == 0 else jnp.where(lo, csrs[pi], css[pi])
            rowb = jnp.broadcast_to(csts[pi][j * SSM_HEADDIM:j * SSM_HEADDIM + 1, :], (n, n))
            lmat = jnp.where(cst["tri_mask_b" if d else "tri_mask_f"], jnp.exp2(jnp.minimum(colb - rowb, 0.0)), 0.0)
            lhs.append((gmats[ci] * lmat).astype(BF16))
            rhs.append(jnp.where(lo if j == 0 else hi, xds[pi], 0.0).astype(BF16))
    prods = [_dot(a, b) for a, b in zip(lhs, rhs)]
    xins = [(xd * jnp.exp2(tot - cs)).astype(BF16) for xd, tot, cs in zip(xds, tots, css)]
    upds = [_dot_tn(bms[ci], xin) for xin, (ci, d) in zip(xins, pairs)]
    res = []
    for ci in range(len(chunks)):
        y = prods[4 * ci] + prods[4 * ci + 1] + prods[4 * ci + 2] + prods[4 * ci + 3]
        outs = [(jnp.exp2(css[2 * ci + d]), upds[2 * ci + d], jnp.exp2(tots[2 * ci + d])) for d in range(2)]
        res.append((y, cms[ci], outs))
    return res


def _ssd_kernel(z_ref, xbc_ref, dt_ref, cw_ref, cb_ref, dtb_ref, alog_ref, dsk_ref, ng_ref,
                y_ref, xa_ref, ya_ref, cm_ref, dof_ref, dob_ref, updf_ref, updb_ref, decf_ref, decb_ref, *, n_lat):
    c = SSM_CHUNK
    n_all = z_ref.shape[1]
    nc = n_all // c
    ncc = (n_all - n_lat) // c
    nl = nc - ncc
    ri = lax.broadcasted_iota(jnp.int32, (c, c), 0)
    ci_ = lax.broadcasted_iota(jnp.int32, (c, c), 1)
    cst = {
        "tril": jnp.where(ci_ <= ri, 1.0, 0.0).astype(BF16),
        "triu": jnp.where(ci_ >= ri, 1.0, 0.0).astype(BF16),
        "tri_mask_f": ci_ <= ri,
        "tri_mask_b": ci_ >= ri,
        "lane": lax.broadcasted_iota(jnp.int32, (c, LANES), 1),
    }
    dtb = dtb_ref[...]
    a_neg = -jnp.exp(alog_ref[...]) * float(np.log2(np.e))
    cb = cb_ref[...]
    halo = 2 * SUBLANES

    def conv_body(ci, carry):
        r0 = pl.multiple_of(ci * c, c)
        cur = xbc_ref[0, pl.ds(r0, c), :].astype(F32)
        first = jnp.logical_or(ci == 0, ci == nl)
        last = jnp.logical_or(ci == nl - 1, ci == nc - 1)
        rp = pl.multiple_of(jnp.maximum(r0 - halo, 0), halo)
        rn = pl.multiple_of(jnp.minimum(r0 + c, n_all - halo), halo)
        prev = xbc_ref[0, pl.ds(rp, halo), :].astype(F32) * jnp.where(first, 0.0, 1.0)
        nxt = xbc_ref[0, pl.ds(rn, halo), :].astype(F32) * jnp.where(last, 0.0, 1.0)
        ext = jnp.concatenate([prev, cur, nxt], axis=0)
        acc = jnp.broadcast_to(cb, (c, 2 * LANES))
        for j in range(SSM_CONV):
            s = (SSM_CONV // 2 - j) % (c + 2 * halo)
            sh = ext if s == 0 else pltpu.roll(ext, s, 0)
            acc = acc + cw_ref[j:j + 1, :] * sh[halo:halo + c, :]
        xa_ref[pl.ds(r0, c), :] = _silu(acc)
        return carry

    lax.fori_loop(0, nc, conv_body, 0)

    def intra(i, carry, nb, first):
        cis = [first + i * nb + u for u in range(nb)]
        rws = [pl.ds(pl.multiple_of(ci * c, c), c) for ci in cis]
        res = _ssd_intra([(xa_ref[rows, :], dt_ref[0, rows, :]) for rows in rws], cst=cst, dtb=dtb, a_neg=a_neg)
        for ci, rows, (y, cm, outs) in zip(cis, rws, res):
            ya_ref[rows, :] = y
            cm_ref[rows, :] = cm
            for (dec_out, upd, dec), do_ref, upd_ref, dec_ref in zip(outs, (dof_ref, dob_ref), (updf_ref, updb_ref), (decf_ref, decb_ref)):
                do_ref[rows, :] = dec_out
                upd_ref[ci] = upd
                dec_ref[ci] = jnp.broadcast_to(dec, (SUBLANES, LANES))
        return carry

    nbl = SSM_BATCH if nl % SSM_BATCH == 0 else 1
    nbc = ncc if ncc <= SSM_BATCH else 1
    lax.fori_loop(0, nl // nbl, functools.partial(intra, nb=nbl, first=0), 0)
    lax.fori_loop(0, ncc // nbc, functools.partial(intra, nb=nbc, first=nl), 0)

    def scan(i, carry):
        st_f, st_b = carry
        jf = jnp.where(i < ncc, nl + i, i - ncc)
        rows = pl.ds(pl.multiple_of(jf * c, c), c)
        ya_ref[rows, :] = ya_ref[rows, :] + _dot(cm_ref[rows, :], st_f.astype(BF16)) * dof_ref[rows, :]
        st_f = st_f * decf_ref[jf][0:1, :] + updf_ref[jf]
        ib = nc - 1 - i
        rows = pl.ds(pl.multiple_of(ib * c, c), c)
        ya_ref[rows, :] = ya_ref[rows, :] + _dot(cm_ref[rows, :], st_b.astype(BF16)) * dob_ref[rows, :]
        st_b = st_b * decb_ref[ib][0:1, :] + updb_ref[ib]
        return st_f, st_b

    zst = jnp.zeros((LANES, LANES), F32)
    lax.fori_loop(0, nc, scan, (zst, zst), unroll=2)

    dsk = dsk_ref[...]
    ng = ng_ref[...]
    tr = ROW_TILE

    def readout(i, carry):
        rows = pl.ds(pl.multiple_of(i * tr, tr), tr)
        y = ya_ref[rows, :] + dsk * xa_ref[rows, 0:LANES]
        yz = y * _silu(z_ref[0, rows, :].astype(F32))
        y_ref[0, rows, :] = _rms(yz, ng).astype(y_ref.dtype)
        return carry

    lax.fori_loop(0, n_all // tr, readout, 0)


def _ssd_call(ss_b, ss_f, conv_w, conv_b, dtb, alog, dsk, ng, n_lat):
    b, n_all, _ = ss_b.shape
    assert SSM_CHUNK == LANES
    nc = n_all // SSM_CHUNK
    vec = pl.BlockSpec((1, LANES), lambda bb, g: (0, g))
    vec2 = pl.BlockSpec((1, 2 * LANES), lambda bb, g: (0, g))
    scratch = [pltpu.VMEM((n_all, 2 * LANES), F32), pltpu.VMEM((n_all, LANES), F32), pltpu.VMEM((n_all, LANES), BF16),
               pltpu.VMEM((n_all, LANES), F32), pltpu.VMEM((n_all, LANES), F32),
               pltpu.VMEM((nc, LANES, LANES), F32), pltpu.VMEM((nc, LANES, LANES), F32),
               pltpu.VMEM((nc, SUBLANES, LANES), F32), pltpu.VMEM((nc, SUBLANES, LANES), F32)]
    est = 2 * (n_all * 5 * LANES * 4 + n_all * LANES * 2) + n_all * 6 * LANES * 4 + 2 * nc * LANES * LANES * 4 + 8 * 1024 * 1024
    return pl.pallas_call(
        functools.partial(_ssd_kernel, n_lat=n_lat),
        grid=(b, SSM_GROUPS),
        in_specs=[
            pl.BlockSpec((1, n_all, LANES), lambda bb, g: (bb, 0, g)),
            pl.BlockSpec((1, n_all, 2 * LANES), lambda bb, g: (bb, 0, 1 + g)),
            pl.BlockSpec((1, n_all, 2 * LANES), lambda bb, g: (bb, 0, g)),
            pl.BlockSpec((SSM_CONV, 2 * LANES), lambda bb, g: (0, g)),
            pl.BlockSpec((1, 2 * LANES), lambda bb, g: (0, g)),
            vec2, vec2, vec, vec,
        ],
        out_specs=pl.BlockSpec((1, n_all, LANES), lambda bb, g: (bb, 0, g)),
        out_shape=jax.ShapeDtypeStruct((b, n_all, SSM_INNER), BF16),
        scratch_shapes=scratch,
        compiler_params=pltpu.CompilerParams(vmem_limit_bytes=_vmem_limit(est)),
        name="ssd_bidir",
    )(ss_b, ss_b, ss_f, conv_w, conv_b, dtb, alog, dsk, ng)


def _ffn_kernel(x_ref, a_ref, r_ref, s_ref, mod_ref, g2_ref, gf_ref, wa_ref, wr_ref, ws_ref, w1_ref, w2_ref,
                o_ref, *, hid_cuts, final, tail_rows):
    d = x_ref.shape[2]
    hidden = w2_ref.shape[0]
    mod = lambda j: mod_ref[0, 0, :, j * d:(j + 1) * d]

    def tile(rows):
        r = slice(0, rows)
        mix = _dot(a_ref[0, r, :], wa_ref[...]) + _dot(r_ref[0, r, :], wr_ref[...]) + _dot(s_ref[0, r, :], ws_ref[...])
        x1 = x_ref[0, r, :] + mod(2) * mix
        h2 = (_rms(x1, g2_ref[...]) * (1.0 + mod(4)) + mod(3)).astype(BF16)
        y = None
        for c0, c1 in zip(hid_cuts[:-1], hid_cuts[1:]):
            ha = _dot(h2, w1_ref[:, c0:c1])
            hb = _dot(h2, w1_ref[:, hidden + c0:hidden + c1])
            act = (_silu(ha) * hb).astype(BF16)
            yc = _dot(act, w2_ref[c0:c1, :])
            y = yc if y is None else y + yc
        x2 = x1 + mod(5) * y
        o_ref[0, r, :] = _rms(x2, gf_ref[...]) if final else x2

    _full_or_tail(tile, x_ref.shape[1], tail_rows, pl.program_id(1), pl.num_programs(1))


def _ffn_call(x_all, a, r, s, modsel, g2, gf, wa, wr, ws, w1, w2, n_lat, final):
    b, n_all, d = x_all.shape
    tm = TOKEN_TILE
    hidden = w2.shape[0]
    n_tiles = hidden // MXU_TILE
    assert hidden % MXU_TILE == 0
    hid_cuts = tuple(MXU_TILE * ((n_tiles * j + FFN_SPLIT - 1) // FFN_SPLIT) for j in range(FFN_SPLIT + 1))
    hid_chunk = max(b1 - b0 for b0, b1 in zip(hid_cuts[:-1], hid_cuts[1:]))
    nlt = n_lat // tm
    n_out = n_lat if final else n_all
    full = lambda arr: pl.BlockSpec(arr.shape, lambda bb, i: (0,) * arr.ndim, pipeline_mode=pl.Buffered(1))
    tok = lambda w: pl.BlockSpec((1, tm, w), lambda bb, i: (bb, i, 0))
    est = (wa.size + wr.size + ws.size + w1.size + w2.size) * 2 + 2 * (2 * tm * d * 4 + tm * 1024 * 2) \
        + 5 * tm * hid_chunk * 4 + 5 * tm * d * 4
    return pl.pallas_call(
        functools.partial(_ffn_kernel, hid_cuts=hid_cuts, final=final,
                          tail_rows=n_out - (pl.cdiv(n_out, tm) - 1) * tm),
        grid=(b, pl.cdiv(n_out, tm)),
        in_specs=[
            tok(d), tok(a.shape[2]), tok(r.shape[2]), tok(s.shape[2]),
            pl.BlockSpec((1, 1, 1, modsel.shape[3]), lambda bb, i: (bb, jnp.minimum(i // nlt, 1), 0, 0)),
            pl.BlockSpec((1, d), lambda bb, i: (0, 0)),
            pl.BlockSpec((1, d), lambda bb, i: (0, 0)),
            full(wa), full(wr), full(ws), full(w1), full(w2),
        ],
        out_specs=tok(d),
        out_shape=jax.ShapeDtypeStruct((b, n_out, d), F32),
        compiler_params=pltpu.CompilerParams(vmem_limit_bytes=_vmem_limit(est)),
        name="outproj_ffn",
    )(x_all, a, r, s, modsel, g2, gf, wa, wr, ws, w1, w2)


def _win_perm():
    perm = np.full((P_TOTAL,), -1, np.int64)
    perm[P_QLAT:P_QLAT + MLA_Q_RANK + MLA_KV_RANK] = np.arange(MLA_Q_RANK + MLA_KV_RANK)
    perm[P_KR + MLA_NOPE:P_KR + MLA_NOPE + ROPE_LANES] = MLA_Q_RANK + MLA_KV_RANK + ROPE_PERM
    for dst, src in zip((0, 1, 2, 3, 4), (0, 3, 4, 1, 2)):
        perm[P_HG + dst * HG_W:P_HG + (dst + 1) * HG_W] = MLA_COLS + src * HG_W + np.arange(HG_W)
    o2 = MLA_COLS + HG_COLS
    perm[P_SSM:P_SSM + SSM_INNER] = o2 + np.arange(SSM_INNER)
    xo = o2 + SSM_INNER
    bo = xo + SSM_INNER
    co = bo + SSM_GROUPS * SSM_STATE
    dto = o2 + SSM_INNER + SSM_XBC
    for g in range(SSM_GROUPS):
        base = P_SSM + SSM_INNER + g * 2 * LANES
        perm[base:base + LANES] = xo + g * LANES + np.arange(LANES)
        perm[base + LANES:base + LANES + SSM_STATE] = bo + g * SSM_STATE + np.arange(SSM_STATE)
        perm[base + LANES + SSM_STATE:base + 2 * LANES] = co + g * SSM_STATE + np.arange(SSM_STATE)
        dbase = P_SSM + SSM_INNER + 2 * 2 * LANES + g * 2 * LANES
        for d in range(2):
            for j in range(2):
                lo = dbase + d * LANES + j * SSM_HEADDIM
                perm[lo:lo + SSM_HEADDIM] = dto + d * SSM_HEADS + 2 * g + j
    return perm


def _conv_perm():
    perm = np.zeros((SSM_XBC,), np.int64)
    for g in range(SSM_GROUPS):
        base = g * 2 * LANES
        perm[base:base + LANES] = g * LANES + np.arange(LANES)
        perm[base + LANES:base + LANES + SSM_STATE] = SSM_INNER + g * SSM_STATE + np.arange(SSM_STATE)
        perm[base + LANES + SSM_STATE:base + 2 * LANES] = SSM_INNER + SSM_GROUPS * SSM_STATE + g * SSM_STATE + np.arange(SSM_STATE)
    return perm


def _gather_cols(w, perm):
    idx = jnp.asarray(np.maximum(perm, 0), jnp.int32)
    out = jnp.take(w, idx, axis=-1)
    return jnp.where(jnp.asarray(perm >= 0), out, 0.0)


def _head_vec(v):
    depth = v.shape[0]
    v5 = v.reshape(depth, 2, SSM_GROUPS, 2, 1)
    v5 = jnp.broadcast_to(v5, (depth, 2, SSM_GROUPS, 2, SSM_HEADDIM))
    return jnp.transpose(v5, (0, 2, 1, 3, 4)).reshape(depth, 1, SSM_GROUPS * 2 * LANES)


def _rope_tables(n_ctx, n_lat, scale):
    rows = n_lat // GRID_W
    row = jnp.repeat(jnp.arange(rows, dtype=F32), GRID_W)
    col = jnp.tile(jnp.arange(GRID_W, dtype=F32), rows)
    n_freq = MLA_ROPE // 4
    inv = ROPE_BASE ** (-jnp.arange(n_freq, dtype=F32) / n_freq)
    ang = jnp.stack([row[:, None] * inv, col[:, None] * inv], axis=1)
    cos, sin = jnp.cos(ang), jnp.sin(ang)
    zf = jnp.zeros((n_lat, ROPE_F), F32)
    c_r = jnp.concatenate([cos[:, 0], cos[:, 0], zf, cos[:, 1], cos[:, 1], zf], axis=-1)
    s_r = jnp.concatenate([-sin[:, 0], sin[:, 0], zf, -sin[:, 1], sin[:, 1], zf], axis=-1)
    pad = jnp.zeros((n_lat, LANES - MLA_NOPE - ROPE_LANES), F32)
    c_lat = jnp.concatenate([jnp.ones((n_lat, MLA_NOPE), F32), c_r, pad], axis=-1)
    s_lat = jnp.concatenate([jnp.zeros((n_lat, MLA_NOPE), F32), s_r, pad], axis=-1)
    keep = jnp.asarray(np.concatenate([np.ones(MLA_NOPE), np.tile(np.repeat([1.0, 1.0, 0.0], ROPE_F), 2),
                                       np.zeros(LANES - MLA_NOPE - ROPE_LANES)]), F32)[None, :]
    c_all = jnp.concatenate([c_lat, jnp.broadcast_to(keep, (n_ctx, LANES))], axis=0)
    s_all = jnp.concatenate([s_lat, jnp.zeros((n_ctx, LANES), F32)], axis=0)
    return c_all * scale, s_all * scale, c_all, s_all


def kernel(x, c, ctx, c_ctx, w_ada, b_ada, norm1_g, norm2_g, w_in, mla_qa_g, mla_wqb, mla_kva_g, mla_wkvb, hg_lb_logits, hg_norm_g, ssm_conv_w, ssm_conv_b, ssm_dt_bias, ssm_a_log, ssm_d, ssm_norm_g, w_out, w_ffn_in, w_ffn_out, final_g):
    bsz, n_lat, d = x.shape
    n_ctx = ctx.shape[1]
    depth = w_ada.shape[0]
    assert n_lat % TOKEN_TILE == 0 and n_lat % GRID_W == 0 and n_ctx <= TOKEN_TILE
    assert n_ctx % (HG_BATCH * HG_CHUNK) == 0 and n_ctx % SSM_CHUNK == 0 and n_ctx % ROW_TILE == 0

    rows = -(-(bsz + 1) // SUBLANES) * SUBLANES
    cc = jnp.zeros((rows, d), F32).at[:bsz].set(c).at[bsz].set(c_ctx)
    mods = _mod_call(cc, w_ada.astype(BF16), b_ada.reshape(depth, 1, 6 * d))
    modsel = jnp.stack([mods[:, :bsz], jnp.broadcast_to(mods[:, bsz:bsz + 1], (depth, bsz, 6 * d))], axis=2)[:, :, :, None, :]

    w_in_p = _gather_cols(w_in, _win_perm()).astype(BF16)
    head_cols = np.full((LANES,), -1, np.int64)
    head_cols[:MLA_NOPE] = np.arange(MLA_NOPE)
    head_cols[MLA_NOPE:MLA_NOPE + ROPE_LANES] = MLA_NOPE + ROPE_PERM
    wqb_perm = np.concatenate([np.where(head_cols >= 0, hh * MLA_QK + head_cols, -1) for hh in range(MLA_HEADS)])
    wqb_p = _gather_cols(mla_wqb, wqb_perm).astype(BF16)
    wkvb_b = mla_wkvb.astype(BF16)
    cperm = _conv_perm()
    conv_w_p = jnp.take(ssm_conv_w, jnp.asarray(cperm, jnp.int32), axis=-1)
    conv_b_p = jnp.take(ssm_conv_b, jnp.asarray(cperm, jnp.int32), axis=-1).reshape(depth, 1, SSM_XBC)
    dtb_p = _head_vec(ssm_dt_bias)
    alog_p = _head_vec(ssm_a_log)
    dsk_p = jnp.repeat(ssm_d, SSM_HEADDIM, axis=-1).reshape(depth, 1, SSM_INNER)
    lb_soft = jax.nn.softmax(hg_lb_logits.astype(F32), axis=0)
    lb_all = jnp.cumsum(lb_soft, axis=0) - lb_soft[0]
    w_out_b = w_out.astype(BF16)
    a_w = MLA_HEADS * MLA_V
    r_w = HG_HEADS * HG_DV
    w1_b = w_ffn_in.astype(BF16)
    w2_b = w_ffn_out.astype(BF16)
    tabs = _rope_tables(n_ctx, n_lat, MLA_QK ** -0.5 * float(np.log2(np.e)))

    x_all = jnp.concatenate([x, ctx], axis=1)
    for l in range(depth):
        q, k, v, hg_b, hg_f, ss_b, ss_f = _inproj_call(
            x_all, modsel[l], norm1_g[l].reshape(1, d), w_in_p[l], mla_qa_g[l].reshape(1, -1),
            mla_kva_g[l].reshape(1, -1), wqb_p[l], wkvb_b[l], tabs, n_lat)
        a = _attn_call(q, k, v, n_lat)
        r = _hgrn_call(hg_b, hg_f, lb_all[l], hg_norm_g[l].reshape(1, -1), n_lat)
        s = _ssd_call(ss_b, ss_f, conv_w_p[l], conv_b_p[l], dtb_p[l], alog_p[l], dsk_p[l],
                      ssm_norm_g[l].reshape(1, -1), n_lat)
        x_all = _ffn_call(x_all, a, r, s, modsel[l], norm2_g[l].reshape(1, d), final_g.reshape(1, d),
                          w_out_b[l, :a_w], w_out_b[l, a_w:a_w + r_w], w_out_b[l, a_w + r_w:], w1_b[l], w2_b[l],
                          n_lat, final=(l == depth - 1))
    return x_all
```

```python
import functools

import numpy as np
import jax
import jax.numpy as jnp
from jax import lax
from jax.experimental import pallas as pl
from jax.experimental.pallas import tpu as pltpu

F32 = jnp.float32
BF16 = jnp.bfloat16
EPS = 1e-6

LANES = 128
SUBLANES = 8
MXU_TILE = 256
VMEM_BYTES = 64 * 1024 * 1024

GRID_W = 64
MLA_HEADS = 8
MLA_Q_RANK = 384
MLA_KV_RANK = 256
MLA_NOPE = 64
MLA_ROPE = 32
MLA_V = 64
MLA_QK = MLA_NOPE + MLA_ROPE
ROPE_BASE = 10000.0
HG_HEADS = 4
HG_DK = 64
HG_DV = 64
HG_W = HG_HEADS * HG_DK
SSM_HEADS = 4
SSM_HEADDIM = 64
SSM_GROUPS = 2
SSM_STATE = 64
SSM_CONV = 5
SSM_INNER = SSM_HEADS * SSM_HEADDIM
SSM_XBC = SSM_INNER + 2 * SSM_GROUPS * SSM_STATE

ROPE_F = MLA_ROPE // 4
ROPE_PERM = np.concatenate([np.concatenate([a * 2 * ROPE_F + np.arange(2 * ROPE_F), a * 2 * ROPE_F + np.arange(ROPE_F)])
                            for a in range(2)])
ROPE_LANES = ROPE_PERM.size

MLA_COLS = MLA_Q_RANK + MLA_KV_RANK + MLA_ROPE
HG_COLS = 3 * HG_W + 2 * HG_HEADS * HG_DV
SSM_COLS = SSM_INNER + SSM_XBC + 2 * SSM_HEADS

P_QLAT = 0
P_KVLAT = MLA_Q_RANK
P_KR = P_KVLAT + MLA_KV_RANK
P_HG = P_KR + LANES
P_SSM = P_HG + HG_COLS
SSM_BLOCK = SSM_INNER + 2 * 2 * LANES + 2 * 2 * LANES
P_TOTAL = P_SSM + SSM_BLOCK

TOKEN_TILE = 512
ROW_TILE = 256
HG_CHUNK = 128
HG_BATCH = 2
SSM_CHUNK = 128
SSM_BATCH = 8
FFN_SPLIT = 2

NT_DIMS = (((1,), (1,)), ((), ()))
TN_DIMS = (((0,), (0,)), ((), ()))


def _vmem_limit(nbytes):
    return int(min(VMEM_BYTES - 8 * 1024 * 1024, max(nbytes, 16 * 1024 * 1024)))


def _sigmoid_pair(x):
    e = jnp.exp(-jnp.abs(x))
    d = 1.0 / (1.0 + e)
    ed = e * d
    pos = x >= 0
    return jnp.where(pos, d, ed), jnp.where(pos, ed, d)


def _silu(x):
    return x / (1.0 + jnp.exp(-x))


def _rms(x, g):
    ms = jnp.mean(x * x, axis=-1, keepdims=True)
    return x * lax.rsqrt(ms + EPS) * g


def _dot(a, b):
    return jnp.dot(a, b, preferred_element_type=F32)


def _dot_nt(a, b):
    return lax.dot_general(a, b, NT_DIMS, preferred_element_type=F32)


def _dot_tn(a, b):
    return lax.dot_general(a, b, TN_DIMS, preferred_element_type=F32)


def _lat_or_ctx(tile, tm, ctx_rows, i, n):
    if ctx_rows is None:
        tile(tm, False)
        return
    last = i == n - 1
    pl.when(last)(lambda: tile(ctx_rows, True))
    pl.when(jnp.logical_not(last))(lambda: tile(tm, False))


def _split3(x):
    x1 = x.astype(BF16)
    r1 = x - x1.astype(F32)
    x2 = r1.astype(BF16)
    x3 = (r1 - x2.astype(F32)).astype(BF16)
    return x1, x2, x3


def _cumsum_mm(tri, x):
    x1, x2, x3 = _split3(x)
    return _dot(tri, x1) + _dot(tri, x2) + _dot(tri, x3)


def _mod_kernel(c_ref, w_ref, b_ref, o_ref):
    s = _silu(c_ref[...]).astype(BF16)
    o_ref[0] = _dot(s, w_ref[0]) + b_ref[0]


def _mod_call(cc, w_ada, b_ada):
    depth, d, six_d = w_ada.shape
    rows = cc.shape[0]
    tn = 1536
    return pl.pallas_call(
        _mod_kernel,
        grid=(depth, six_d // tn),
        in_specs=[
            pl.BlockSpec((rows, d), lambda l, j: (0, 0)),
            pl.BlockSpec((1, d, tn), lambda l, j: (l, 0, j)),
            pl.BlockSpec((1, 1, tn), lambda l, j: (l, 0, j)),
        ],
        out_specs=pl.BlockSpec((1, rows, tn), lambda l, j: (l, 0, j)),
        out_shape=jax.ShapeDtypeStruct((depth, rows, six_d), F32),
        name="adaln_mod",
    )(cc, w_ada, b_ada)


def _rope(xh, c, s):
    return xh * c + pltpu.roll(xh, LANES - ROPE_F, 1) * s


def _inproj_kernel(xl_ref, xc_ref, mod_ref, g1_ref, win_ref, qag_ref, kvag_ref, wqb_ref, wkvb_ref,
                   cq_ref, sq_ref, ck_ref, sk_ref,
                   q_ref, k_ref, v_ref, hg_ref, ssm_ref):
    d = xl_ref.shape[2]

    def tile(rows, is_ctx):
        r = slice(0, rows)
        x = xc_ref[0] if is_ctx else xl_ref[0]
        sh = mod_ref[0, 0, :, 0:d]
        sc = mod_ref[0, 0, :, d:2 * d]
        h = _rms(x, g1_ref[...]) * (1.0 + sc) + sh
        p = _dot(h.astype(BF16), win_ref[...])
        hg_ref[0, r, :] = p[:, P_HG:P_SSM]
        ssm_ref[0, r, :] = p[:, P_SSM:P_TOTAL]
        q_lat = _rms(p[:, P_QLAT:P_KVLAT], qag_ref[...])
        kv_lat = _rms(p[:, P_KVLAT:P_KR], kvag_ref[...])
        q = _dot(q_lat.astype(BF16), wqb_ref[...])
        kv = _dot(kv_lat.astype(BF16), wkvb_ref[...])
        lane = lax.broadcasted_iota(jnp.int32, (rows, LANES), 1)
        cq, sq, ck, sk = cq_ref[r, :], sq_ref[r, :], ck_ref[r, :], sk_ref[r, :]
        kr = _rope(p[:, P_KR:P_HG], ck, sk)
        nope = lane < MLA_NOPE
        for hh in range(MLA_HEADS):
            sl = slice(hh * LANES, (hh + 1) * LANES)
            q_ref[0, hh, r, :] = _rope(q[:, sl], cq, sq).astype(q_ref.dtype)
            kvh = kv[:, sl]
            k_ref[0, hh, r, :] = jnp.where(nope, kvh, kr).astype(k_ref.dtype)
            v_ref[0, hh, r, :] = jnp.where(nope, 1.0, kvh).astype(v_ref.dtype)

    _lat_or_ctx(tile, xl_ref.shape[1], xc_ref.shape[1], pl.program_id(1), pl.num_programs(1))


def _inproj_call(x_lat, x_ctx, modsel, g1, w_in, qa_g, kva_g, wqb, wkvb, tabs):
    b, n_lat, d = x_lat.shape
    n_ctx = x_ctx.shape[1]
    n_all = n_lat + n_ctx
    tm = TOKEN_TILE
    nlt = n_lat // tm
    nt = nlt + 1
    full = lambda shape: pl.BlockSpec(shape, lambda bb, i: (0,) * len(shape), pipeline_mode=pl.Buffered(1))
    tab = pl.BlockSpec((tm, LANES), lambda bb, i: (i, 0))
    head_out = pl.BlockSpec((1, MLA_HEADS, tm, LANES), lambda bb, i: (bb, 0, i, 0))
    est = (w_in.size + wqb.size + wkvb.size) * 2 + 2 * (tm * d * 4 + 3 * MLA_HEADS * tm * LANES * 2
                                                        + tm * (HG_COLS + SSM_BLOCK) * 4) + 4 * tm * P_TOTAL * 4
    return pl.pallas_call(
        _inproj_kernel,
        grid=(b, nt),
        in_specs=[
            pl.BlockSpec((1, tm, d), lambda bb, i: (bb, jnp.minimum(i, nlt - 1), 0)),
            pl.BlockSpec((1, n_ctx, d), lambda bb, i: (bb, 0, 0)),
            pl.BlockSpec((1, 1, 1, modsel.shape[3]), lambda bb, i: (bb, i // nlt, 0, 0)),
            full((1, d)),
            full(w_in.shape),
            full((1, MLA_Q_RANK)),
            full((1, MLA_KV_RANK)),
            full(wqb.shape),
            full(wkvb.shape),
            tab, tab, tab, tab,
        ],
        out_specs=[
            head_out, head_out, head_out,
            pl.BlockSpec((1, tm, HG_COLS), lambda bb, i: (bb, i, 0)),
            pl.BlockSpec((1, tm, SSM_BLOCK), lambda bb, i: (bb, i, 0)),
        ],
        out_shape=[
            jax.ShapeDtypeStruct((b, MLA_HEADS, n_all, LANES), BF16),
            jax.ShapeDtypeStruct((b, MLA_HEADS, n_all, LANES), BF16),
            jax.ShapeDtypeStruct((b, MLA_HEADS, n_all, LANES), BF16),
            jax.ShapeDtypeStruct((b, n_all, HG_COLS), F32),
            jax.ShapeDtypeStruct((b, n_all, SSM_BLOCK), F32),
        ],
        compiler_params=pltpu.CompilerParams(vmem_limit_bytes=_vmem_limit(est)),
        name="inproj_mla",
    )(x_lat, x_ctx, modsel, g1, w_in, qa_g, kva_g, wqb, wkvb, *tabs)


def _attn_kernel(q_ref, k_ref, v_ref, o_ref, *, n_lat):
    n_all = k_ref.shape[2]
    tq = q_ref.shape[2]

    def run(k0, k1):
        ss = [_dot_nt(q_ref[0, j], k_ref[0, j, k0:k1, :]) for j in range(2)]
        ps = [jnp.exp2(s - jnp.max(s, axis=-1, keepdims=True)).astype(BF16) for s in ss]
        outs = []
        for j in range(2):
            o = _dot(ps[j], v_ref[0, j, k0:k1, :])
            outs.append(o / pltpu.roll(o, MLA_V, 1))
        lane = lax.broadcasted_iota(jnp.int32, (tq, LANES), 1)
        o_ref[0] = jnp.where(lane < MLA_V, pltpu.roll(outs[0], MLA_V, 1), outs[1]).astype(o_ref.dtype)

    is_ctx = pl.program_id(2) >= n_lat // tq

    @pl.when(is_ctx)
    def _():
        run(n_lat, n_all)

    @pl.when(jnp.logical_not(is_ctx))
    def _():
        run(0, n_all)


def _attn_call(q, k, v, n_lat):
    b, h, n_all, _ = q.shape
    tq = TOKEN_TILE
    kv_spec = pl.BlockSpec((1, 2, n_all, LANES), lambda bb, hp, i: (bb, hp, 0, 0))
    est = 2 * (2 * tq * LANES * 2 + 2 * 2 * n_all * LANES * 2 + tq * LANES * 2) + 7 * tq * n_all * 4
    return pl.pallas_call(
        functools.partial(_attn_kernel, n_lat=n_lat),
        grid=(b, h // 2, pl.cdiv(n_all, tq)),
        in_specs=[pl.BlockSpec((1, 2, tq, LANES), lambda bb, hp, i: (bb, hp, i, 0)), kv_spec, kv_spec],
        out_specs=pl.BlockSpec((1, tq, LANES), lambda bb, hp, i: (bb, i, hp)),
        out_shape=jax.ShapeDtypeStruct((b, n_all, h * MLA_V), BF16),
        compiler_params=pltpu.CompilerParams(vmem_limit_bytes=_vmem_limit(est)),
        name="mla_attention",
    )(q, k, v)


def _hg_consts(c):
    ri = lax.broadcasted_iota(jnp.int32, (c, c), 0)
    ci = lax.broadcasted_iota(jnp.int32, (c, c), 1)
    cst = {
        "tril": jnp.where(ci <= ri, 1.0, 0.0).astype(BF16),
        "triu": jnp.where(ci >= ri, 1.0, 0.0).astype(BF16),
        "lvl": {},
    }
    h = SUBLANES
    while 2 * h <= c:
        same =jnp.right_shift(ri, int(np.log2(2 * h))) == jnp.right_shift(ci, int(np.log2(2 * h)))
        r_lo = jnp.bitwise_and(ri, 2 * h - 1) < h
        c_lo = jnp.bitwise_and(ci, 2 * h - 1) < h
        fwd = jnp.where(same, jnp.where(r_lo, 0.0, jnp.where(c_lo, 1.0, 0.0)), 0.0)
        bwd = jnp.where(same, jnp.where(r_lo, jnp.where(c_lo, 0.0, 1.0), 0.0), 0.0)
        cst["lvl"][(h, False)] = fwd
        cst["lvl"][(h, True)] = bwd
        h *= 2
    r2 = lax.broadcasted_iota(jnp.int32, (LANES, LANES), 0)
    c2 = lax.broadcasted_iota(jnp.int32, (LANES, LANES), 1)
    bd = (r2 < HG_DK) == (c2 < HG_DK)
    cst["bd"] = bd
    cst["bo"] = jnp.where(bd, 1.0, 0.0).astype(BF16)
    lane = lax.broadcasted_iota(jnp.int32, (c, LANES), 1)
    cst["hm"] = [lane < HG_DK, lane >= HG_DK]
    cst["rit"] = jnp.bitwise_and(lax.broadcasted_iota(jnp.int32, (c, LANES), 0), SUBLANES - 1)
    return cst


def _hg_level_ref(b, h, rev):
    c = b.shape[0]
    off = h if rev else h - 1
    pieces = [jnp.broadcast_to(b[blk * 2 * h + off:blk * 2 * h + off + 1, :], (2 * h, LANES))
              for blk in range(c // (2 * h))]
    return pieces[0] if len(pieces) == 1 else jnp.concatenate(pieces, axis=0)


def _hg_intra(streams, *, cst):
    c = streams[0][0].shape[0]
    ns = len(streams)
    bs = [_cumsum_mm(cst["triu"] if s[5] else cst["tril"], s[3]) for s in streams]
    tots = [b[0:1, :] if s[5] else b[c - 1:c, :] for b, s in zip(bs, streams)]
    qes = [(s[0] * jnp.exp2(b)).astype(BF16) for b, s in zip(bs, streams)]
    atts = [[None, None] for _ in range(ns)]
    sign = jnp.uint32(0x80000000)
    h = SUBLANES
    while 2 * h <= c:
        ops = []
        for si, (b, (q, k, f, g, v, rev)) in enumerate(zip(bs, streams)):
            ref = _hg_level_ref(b, h, rev)
            nabs = pltpu.bitcast(pltpu.bitcast(b - ref, jnp.uint32) | sign, F32)
            e = jnp.exp2(nabs)
            qh = q * e
            ops.append(([jnp.where(cst["hm"][j], qh, 0.0).astype(BF16) for j in range(2)], (k * e).astype(BF16)))
        prods = [[_dot_nt(qj, kh) for qj in qjs] for qjs, kh in ops]
        for si in range(ns):
            msk = cst["lvl"][(h, streams[si][5])]
            for j in range(2):
                t = prods[si][j] * msk
                atts[si][j] = t if atts[si][j] is None else atts[si][j] + t
        h *= 2
    outs = []
    for si, (q, k, f, g, v, rev) in enumerate(streams):
        o = None
        for j in range(2):
            vj = jnp.where(cst["hm"][j], v, 0.0).astype(BF16)
            oj = _dot(atts[si][j].astype(BF16), vj)
            o = oj if o is None else o + oj
        outs.append(o)

    def shift(x, jj, rev):
        if jj == 0:
            return x
        x3 = x.reshape(c // SUBLANES, SUBLANES, LANES)
        return pltpu.roll(x3, (SUBLANES - jj) if rev else jj, 1).reshape(c, LANES)

    es = [None] * ns
    for dd in range(SUBLANES):
        ws = []
        for si, (q, k, f, g, v, rev) in enumerate(streams):
            if dd == 0:
                w = q * k
            else:
                fd = shift(f, dd - 1, rev)
                es[si] = fd if es[si] is None else es[si] * fd
                valid = (cst["rit"] <= SUBLANES - 1 - dd) if rev else (cst["rit"] >= dd)
                w = jnp.where(valid, q * shift(k, dd, rev) * es[si], 0.0)
            ws.append(w.astype(BF16))
        sums = [_dot(w, cst["bo"]) for w in ws]
        for si, (q, k, f, g, v, rev) in enumerate(streams):
            outs[si] = outs[si] + sums[si] * shift(v, dd, rev)
    res = []
    for si, (q, k, f, g, v, rev) in enumerate(streams):
        ke = (k * jnp.exp2(tots[si] - bs[si])).astype(BF16)
        upd = jnp.where(cst["bd"], _dot_tn(v.astype(BF16), ke), 0.0)
        res.append((outs[si], qes[si], upd, jnp.exp2(tots[si])))
    return res


def _hgrn_kernel(q_ref, fff_ref, ffb_ref, iv_ref, og_ref, lb_ref, ng_ref, r_ref,
                 o_ref, qef_ref, qeb_ref, updf_ref, updb_ref, decf_ref, decb_ref, *, n_lat):
    c = HG_CHUNK
    n_all = q_ref.shape[1]
    nc = n_all // c
    ncc = (n_all - n_lat) // c
    cst = _hg_consts(c)
    bo = cst["bo"]
    ng = ng_ref[...]
    tr = ROW_TILE
    dirs = ((False, fff_ref, lb_ref[0:1, :], qef_ref, updf_ref, decf_ref),
            (True, ffb_ref, lb_ref[1:2, :], qeb_ref, updb_ref, decb_ref))
    nb = HG_BATCH
    assert nc % nb == 0

    def intra(i, carry):
        streams, where = [], []
        for u in range(nb):
            ci = i * nb + u
            rows = pl.ds(pl.multiple_of(ci * c, c), c)
            q = _silu(q_ref[0, rows, :])
            v = iv_ref[0, rows, :]
            for rev, ff_ref, lb, qe_ref, upd_ref, dec_ref in dirs:
                sig, sigm = _sigmoid_pair(ff_ref[0, rows, :])
                f = lb + (1.0 - lb) * sig
                streams.append((q, (1.0 - lb) * sigm, f, jnp.log2(f), v, rev))
                where.append((ci, rows, qe_ref, upd_ref, dec_ref))
        res = _hg_intra(streams, cst=cst)
        for u in range(nb):
            o_ref[where[2 * u][1], :] = res[2 * u][0] + res[2 * u + 1][0]
        for (_, qe, upd, dec), (ci, rows, qe_ref, upd_ref, dec_ref) in zip(res, where):
            qe_ref[rows, :] = qe
            upd_ref[ci] = upd
            dec_ref[ci] = jnp.broadcast_to(dec, (SUBLANES, LANES))
        return carry

    lax.fori_loop(0, nc // nb, intra, 0)

    def scan(i, carry):
        st_f, st_b = carry
        jf = jnp.where(i < ncc, nc - ncc + i, i - ncc)
        rows = pl.ds(pl.multiple_of(jf * c, c), c)
        o_ref[rows, :] = o_ref[rows, :] + _dot_nt(qef_ref[rows, :], st_f.astype(BF16))
        st_f = st_f * decf_ref[jf][0:1, :] + updf_ref[jf]
        ib = nc - 1 - i
        rows = pl.ds(pl.multiple_of(ib * c, c), c)
        o_ref[rows, :] = o_ref[rows, :] + _dot_nt(qeb_ref[rows, :], st_b.astype(BF16))
        st_b = st_b * decb_ref[ib][0:1, :] + updb_ref[ib]
        return st_f, st_b

    zero = jnp.zeros((LANES, LANES), F32)
    lax.fori_loop(0, nc, scan, (zero, zero), unroll=2)

    def readout(i, carry):
        rows = pl.ds(pl.multiple_of(i * tr, tr), tr)
        o = o_ref[rows, :]
        ms = _dot((o * o).astype(BF16), bo) * (1.0 / HG_DV)
        r = o * lax.rsqrt(ms + EPS) * ng * _silu(og_ref[0, rows, :])
        r_ref[0, rows, :] = r.astype(r_ref.dtype)
        return carry

    lax.fori_loop(0, n_all // tr, readout, 0)


def _hgrn_call(p_hg, lb, ng, n_lat):
    b, n_all, _ = p_hg.shape
    nc = n_all // HG_CHUNK
    col = lambda j: pl.BlockSpec((1, n_all, LANES), lambda bb, pr: (bb, 0, 2 * j + pr))
    scratch = [pltpu.VMEM((n_all, LANES), F32), pltpu.VMEM((n_all, LANES), BF16), pltpu.VMEM((n_all, LANES), BF16),
               pltpu.VMEM((nc, LANES, LANES), F32), pltpu.VMEM((nc, LANES, LANES), F32),
               pltpu.VMEM((nc, SUBLANES, LANES), F32), pltpu.VMEM((nc, SUBLANES, LANES), F32)]
    est = 2 * 6 * n_all * LANES * 4 + 2 * n_all * LANES * 4 + 2 * nc * LANES * LANES * 4 + 8 * 1024 * 1024
    return pl.pallas_call(
        functools.partial(_hgrn_kernel, n_lat=n_lat),
        grid=(b, 2),
        in_specs=[col(0), col(1), col(2), col(3), col(4),
                  pl.BlockSpec((2, LANES), lambda bb, pr: (0, pr)),
                  pl.BlockSpec((1, LANES), lambda bb, pr: (0, pr))],
        out_specs=pl.BlockSpec((1, n_all, LANES), lambda bb, pr: (bb, 0, pr)),
        out_shape=jax.ShapeDtypeStruct((b, n_all, HG_HEADS * HG_DV), BF16),
        scratch_shapes=scratch,
        compiler_params=pltpu.CompilerParams(vmem_limit_bytes=_vmem_limit(est)),
        name="hgrn2_bidir",
    )(p_hg, p_hg, p_hg, p_hg, p_hg, lb, ng)


def _ssd_intra(chunks, *, cst, dtb, a_neg):
    n = chunks[0][0].shape[0]
    lo = cst["lane"] < SSM_STATE
    hi = jnp.logical_not(lo)
    xss = [xa[:, 0:LANES] for xa, _ in chunks]
    bms, cms = [], []
    for xa, _ in chunks:
        bc = xa[:, LANES:2 * LANES]
        bms.append(jnp.where(lo, bc, 0.0).astype(BF16))
        cms.append(jnp.where(lo, pltpu.roll(bc, SSM_STATE, 1), 0.0).astype(BF16))
    gmats = [_dot_nt(cm, bm) for cm, bm in zip(cms, bms)]
    pairs = [(ci, d) for ci in range(len(chunks)) for d in range(2)]
    dts, css = [], []
    for ci, d in pairs:
        sl = slice(d * LANES, (d + 1) * LANES)
        xdt = chunks[ci][1][:, sl] + dtb[:, sl]
        dt = jnp.maximum(xdt, 0.0) + jnp.log(1.0 + jnp.exp(-jnp.abs(xdt)))
        dts.append(dt)
        css.append(_cumsum_mm(cst["triu"] if d else cst["tril"], dt * a_neg[:, sl]))
    tots = [cs[0:1, :] if d else cs[n - 1:n, :] for cs, (ci, d) in zip(css, pairs)]
    csrs = [pltpu.roll(cs, SSM_HEADDIM, 1) for cs in css]
    csts = [cs.T for cs in css]
    xds = [xss[ci] * dt for dt, (ci, d) in zip(dts, pairs)]
    lhs, rhs = [], []
    for pi, (ci, d) in enumerate(pairs):
        for j in range(2):
            colb = jnp.where(lo, css[pi], csrs[pi]) if j == 0 else jnp.where(lo, csrs[pi], css[pi])
            rowb = jnp.broadcast_to(csts[pi][j * SSM_HEADDIM:j * SSM_HEADDIM + 1, :], (n, n))
            lmat = jnp.where(cst["tri_mask_b" if d else "tri_mask_f"], jnp.exp2(jnp.minimum(colb - rowb, 0.0)), 0.0)
            lhs.append((gmats[ci] * lmat).astype(BF16))
            rhs.append(jnp.where(lo if j == 0 else hi, xds[pi], 0.0).astype(BF16))
    prods = [_dot(a, b) for a, b in zip(lhs, rhs)]
    xins = [(xd * jnp.exp2(tot - cs)).astype(BF16) for xd, tot, cs in zip(xds, tots, css)]
    upds = [_dot_tn(bms[ci], xin) for xin, (ci, d) in zip(xins, pairs)]
    res = []
    for ci in range(len(chunks)):
        y = prods[4 * ci] + prods[4 * ci + 1] + prods[4 * ci + 2] + prods[4 * ci + 3]
        outs = [(jnp.exp2(css[2 * ci + d]), upds[2 * ci + d], jnp.exp2(tots[2 * ci + d])) for d in range(2)]
        res.append((y, cms[ci], outs))
    return res


def _ssd_kernel(z_ref, xbc_ref, dt_ref, cw_ref, cb_ref, dtb_ref, alog_ref, dsk_ref, ng_ref,
                y_ref, xa_ref, ya_ref, cm_ref, dof_ref, dob_ref, updf_ref, updb_ref, decf_ref, decb_ref, *, n_lat):
    c = SSM_CHUNK
    n_all = z_ref.shape[1]
    nc = n_all // c
    ncc = (n_all - n_lat) // c
    nl = nc - ncc
    ri = lax.broadcasted_iota(jnp.int32, (c, c), 0)
    ci_ = lax.broadcasted_iota(jnp.int32, (c, c), 1)
    cst = {
        "tril": jnp.where(ci_ <= ri, 1.0, 0.0).astype(BF16),
        "triu": jnp.where(ci_ >= ri, 1.0, 0.0).astype(BF16),
        "tri_mask_f": ci_ <= ri,
        "tri_mask_b": ci_ >= ri,
        "lane": lax.broadcasted_iota(jnp.int32, (c, LANES), 1),
    }
    dtb = dtb_ref[...]
    a_neg = -jnp.exp(alog_ref[...]) * float(np.log2(np.e))
    cb = cb_ref[...]
    halo = SUBLANES

    def conv_body(ci, carry):
        r0 = pl.multiple_of(ci * c, c)
        cur = xbc_ref[0, pl.ds(r0, c), :]
        first = jnp.logical_or(ci == 0, ci == nl)
        last = jnp.logical_or(ci == nl - 1, ci == nc - 1)
        rp = pl.multiple_of(jnp.maximum(r0 - halo, 0), halo)
        rn = pl.multiple_of(jnp.minimum(r0 + c, n_all - halo), halo)
        prev = xbc_ref[0, pl.ds(rp, halo), :] * jnp.where(first, 0.0, 1.0)
        nxt = xbc_ref[0, pl.ds(rn, halo), :] * jnp.where(last, 0.0, 1.0)
        ext = jnp.concatenate([prev, cur, nxt], axis=0)
        acc = jnp.broadcast_to(cb, (c, 2 * LANES))
        for j in range(SSM_CONV):
            s = (SSM_CONV // 2 - j) % (c + 2 * halo)
            sh = ext if s == 0 else pltpu.roll(ext, s, 0)
            acc = acc + cw_ref[j:j + 1, :] * sh[halo:halo + c, :]
        xa_ref[pl.ds(r0, c), :] = _silu(acc)
        return carry

    lax.fori_loop(0, nc, conv_body, 0)

    def intra(i, carry, nb, first):
        cis = [first + i * nb + u for u in range(nb)]
        rws = [pl.ds(pl.multiple_of(ci * c, c), c) for ci in cis]
        res = _ssd_intra([(xa_ref[rows, :], dt_ref[0, rows, :]) for rows in rws], cst=cst, dtb=dtb, a_neg=a_neg)
        for ci, rows, (y, cm, outs) in zip(cis, rws, res):
            ya_ref[rows, :] = y
            cm_ref[rows, :] = cm
            for (dec_out, upd, dec), do_ref, upd_ref, dec_ref in zip(outs, (dof_ref, dob_ref), (updf_ref, updb_ref), (decf_ref, decb_ref)):
                do_ref[rows, :] = dec_out
                upd_ref[ci] = upd
                dec_ref[ci] = jnp.broadcast_to(dec, (SUBLANES, LANES))
        return carry

    nbl = SSM_BATCH if nl % SSM_BATCH == 0 else 1
    nbc = ncc if ncc <= SSM_BATCH else 1
    lax.fori_loop(0, nl // nbl, functools.partial(intra, nb=nbl, first=0), 0)
    lax.fori_loop(0, ncc // nbc, functools.partial(intra, nb=nbc, first=nl), 0)

    def scan(i, carry):
        st_f, st_b = carry
        jf = jnp.where(i < ncc, nl + i, i - ncc)
        rows = pl.ds(pl.multiple_of(jf * c, c), c)
        ya_ref[rows, :] = ya_ref[rows, :] + _dot(cm_ref[rows, :], st_f.astype(BF16)) * dof_ref[rows, :]
        st_f = st_f * decf_ref[jf][0:1, :] + updf_ref[jf]
        ib = nc - 1 - i
        rows = pl.ds(pl.multiple_of(ib * c, c), c)
        ya_ref[rows, :] = ya_ref[rows, :] + _dot(cm_ref[rows, :], st_b.astype(BF16)) * dob_ref[rows, :]
        st_b = st_b * decb_ref[ib][0:1, :] + updb_ref[ib]
        return st_f, st_b

    zst = jnp.zeros((LANES, LANES), F32)
    lax.fori_loop(0, nc, scan, (zst, zst), unroll=2)

    dsk = dsk_ref[...]
    ng = ng_ref[...]
    tr = ROW_TILE

    def readout(i, carry):
        rows = pl.ds(pl.multiple_of(i * tr, tr), tr)
        y = ya_ref[rows, :] + dsk * xa_ref[rows, 0:LANES]
        yz = y * _silu(z_ref[0, rows, :])
        y_ref[0, rows, :] = _rms(yz, ng).astype(y_ref.dtype)
        return carry

    lax.fori_loop(0, n_all // tr, readout, 0)


def _ssd_call(p_ssm, conv_w, conv_b, dtb, alog, dsk, ng, n_lat):
    b, n_all, _ = p_ssm.shape
    assert SSM_CHUNK == LANES
    nc = n_all // SSM_CHUNK
    vec = pl.BlockSpec((1, LANES), lambda bb, g: (0, g))
    vec2 = pl.BlockSpec((1, 2 * LANES), lambda bb, g: (0, g))
    scratch = [pltpu.VMEM((n_all, 2 * LANES), F32), pltpu.VMEM((n_all, LANES), F32), pltpu.VMEM((n_all, LANES), BF16),
               pltpu.VMEM((n_all, LANES), F32), pltpu.VMEM((n_all, LANES), F32),
               pltpu.VMEM((nc, LANES, LANES), F32), pltpu.VMEM((nc, LANES, LANES), F32),
               pltpu.VMEM((nc, SUBLANES, LANES), F32), pltpu.VMEM((nc, SUBLANES, LANES), F32)]
    est = 2 * (n_all * 5 * LANES * 4 + n_all * LANES * 2) + n_all * 6 * LANES * 4 + 2 * nc * LANES * LANES * 4 + 8 * 1024 * 1024
    return pl.pallas_call(
        functools.partial(_ssd_kernel, n_lat=n_lat),
        grid=(b, SSM_GROUPS),
        in_specs=[
            pl.BlockSpec((1, n_all, LANES), lambda bb, g: (bb, 0, g)),
            pl.BlockSpec((1, n_all, 2 * LANES), lambda bb, g: (bb, 0, 1 + g)),
            pl.BlockSpec((1, n_all, 2 * LANES), lambda bb, g: (bb, 0, 3 + g)),
            pl.BlockSpec((SSM_CONV, 2 * LANES), lambda bb, g: (0, g)),
            pl.BlockSpec((1, 2 * LANES), lambda bb, g: (0, g)),
            vec2, vec2, vec, vec,
        ],
        out_specs=pl.BlockSpec((1, n_all, LANES), lambda bb, g: (bb, 0, g)),
        out_shape=jax.ShapeDtypeStruct((b, n_all, SSM_INNER), BF16),
        scratch_shapes=scratch,
        compiler_params=pltpu.CompilerParams(vmem_limit_bytes=_vmem_limit(est)),
        name="ssd_bidir",
    )(p_ssm, p_ssm, p_ssm, conv_w, conv_b, dtb, alog, dsk, ng)


def _ffn_kernel(xl_ref, xc_ref, a_ref, r_ref, s_ref, mod_ref, g2_ref, gf_ref, wa_ref, wr_ref, ws_ref, w1_ref, w2_ref,
                ol_ref, oc_ref=None, *, hid_cuts, final):
    d = xl_ref.shape[2]
    hidden = w2_ref.shape[0]
    mod = lambda j: mod_ref[0, 0, :, j * d:(j + 1) * d]

    def tile(rows, is_ctx):
        r = slice(0, rows)
        x_ref, o_ref = (xc_ref, oc_ref) if is_ctx else (xl_ref, ol_ref)
        mix = _dot(a_ref[0, r, :], wa_ref[...]) + _dot(r_ref[0, r, :], wr_ref[...]) + _dot(s_ref[0, r, :], ws_ref[...])
        x1 = x_ref[0] + mod(2) * mix
        h2 = (_rms(x1, g2_ref[...]) * (1.0 + mod(4)) + mod(3)).astype(BF16)
        y = None
        for c0, c1 in zip(hid_cuts[:-1], hid_cuts[1:]):
            ha = _dot(h2, w1_ref[:, c0:c1])
            hb = _dot(h2, w1_ref[:, hidden + c0:hidden + c1])
            act = (_silu(ha) * hb).astype(BF16)
            yc = _dot(act, w2_ref[c0:c1, :])
            y = yc if y is None else y + yc
        x2 = x1 + mod(5) * y
        o_ref[0] = _rms(x2, gf_ref[...]) if final else x2

    _lat_or_ctx(tile, xl_ref.shape[1], None if final else xc_ref.shape[1], pl.program_id(1), pl.num_programs(1))


def _ffn_call(x_lat, x_ctx, a, r, s, modsel, g2, gf, wa, wr, ws, w1, w2, final):
    b, n_lat, d = x_lat.shape
    n_ctx = x_ctx.shape[1]
    tm = TOKEN_TILE
    hidden = w2.shape[0]
    n_tiles = hidden // MXU_TILE
    assert hidden % MXU_TILE == 0
    hid_cuts = tuple(MXU_TILE * ((n_tiles * j + FFN_SPLIT - 1) // FFN_SPLIT) for j in range(FFN_SPLIT + 1))
    hid_chunk = max(b1 - b0 for b0, b1 in zip(hid_cuts[:-1], hid_cuts[1:]))
    nlt = n_lat // tm
    full = lambda arr: pl.BlockSpec(arr.shape, lambda bb, i: (0,) * arr.ndim, pipeline_mode=pl.Buffered(1))
    tok = lambda w: pl.BlockSpec((1, tm, w), lambda bb, i: (bb, i, 0))
    lat = pl.BlockSpec((1, tm, d), lambda bb, i: (bb, jnp.minimum(i, nlt - 1), 0))
    ctx = pl.BlockSpec((1, n_ctx, d), lambda bb, i: (bb, 0, 0))
    est = (wa.size + wr.size + ws.size + w1.size + w2.size) * 2 + 2 * (2 * tm * d * 4 + 2 * n_ctx * d * 4 + tm * 1024 * 2) \
        + 5 * tm * hid_chunk * 4 + 5 * tm * d * 4
    return pl.pallas_call(
        functools.partial(_ffn_kernel, hid_cuts=hid_cuts, final=final),
        grid=(b, nlt if final else nlt + 1),
        in_specs=[
            lat, ctx, tok(a.shape[2]), tok(r.shape[2]), tok(s.shape[2]),
            pl.BlockSpec((1, 1, 1, modsel.shape[3]), lambda bb, i: (bb, i // nlt, 0, 0)),
            pl.BlockSpec((1, d), lambda bb, i: (0, 0)),
            pl.BlockSpec((1, d), lambda bb, i: (0, 0)),
            full(wa), full(wr), full(ws), full(w1), full(w2),
        ],
        out_specs=[lat] if final else [lat, ctx],
        out_shape=[jax.ShapeDtypeStruct((b, n_lat, d), F32)] + ([] if final else [jax.ShapeDtypeStruct((b, n_ctx, d), F32)]),
        compiler_params=pltpu.CompilerParams(vmem_limit_bytes=_vmem_limit(est)),
        name="outproj_ffn",
    )(x_lat, x_ctx, a, r, s, modsel, g2, gf, wa, wr, ws, w1, w2)


def _win_perm():
    perm = np.full((P_TOTAL,), -1, np.int64)
    perm[P_QLAT:P_QLAT + MLA_Q_RANK + MLA_KV_RANK] = np.arange(MLA_Q_RANK + MLA_KV_RANK)
    perm[P_KR + MLA_NOPE:P_KR + MLA_NOPE + ROPE_LANES] = MLA_Q_RANK + MLA_KV_RANK + ROPE_PERM
    perm[P_HG:P_HG + HG_COLS] = MLA_COLS + np.arange(HG_COLS)
    o2 = MLA_COLS + HG_COLS
    perm[P_SSM:P_SSM + SSM_INNER] = o2 + np.arange(SSM_INNER)
    xo = o2 + SSM_INNER
    bo = xo + SSM_INNER
    co = bo + SSM_GROUPS * SSM_STATE
    dto = o2 + SSM_INNER + SSM_XBC
    for g in range(SSM_GROUPS):
        base = P_SSM + SSM_INNER + g * 2 * LANES
        perm[base:base + LANES] = xo + g * LANES + np.arange(LANES)
        perm[base + LANES:base + LANES + SSM_STATE] = bo + g * SSM_STATE + np.arange(SSM_STATE)
        perm[base + LANES + SSM_STATE:base + 2 * LANES] = co + g * SSM_STATE + np.arange(SSM_STATE)
        dbase = P_SSM + SSM_INNER + 2 * 2 * LANES + g * 2 * LANES
        for d in range(2):
            for j in range(2):
                lo = dbase + d * LANES + j * SSM_HEADDIM
                perm[lo:lo + SSM_HEADDIM] = dto + d * SSM_HEADS + 2 * g + j
    return perm


def _conv_perm():
    perm = np.zeros((SSM_XBC,), np.int64)
    for g in range(SSM_GROUPS):
        base = g * 2 * LANES
        perm[base:base + LANES] = g * LANES + np.arange(LANES)
        perm[base + LANES:base + LANES + SSM_STATE] = SSM_INNER + g * SSM_STATE + np.arange(SSM_STATE)
        perm[base + LANES + SSM_STATE:base + 2 * LANES] = SSM_INNER + SSM_GROUPS * SSM_STATE + g * SSM_STATE + np.arange(SSM_STATE)
    return perm


def _gather_cols(w, perm):
    idx = jnp.asarray(np.maximum(perm, 0), jnp.int32)
    out = jnp.take(w, idx, axis=-1)
    return jnp.where(jnp.asarray(perm >= 0), out, 0.0)


def _head_vec(v):
    depth = v.shape[0]
    v5 = v.reshape(depth, 2, SSM_GROUPS, 2, 1)
    v5 = jnp.broadcast_to(v5, (depth, 2, SSM_GROUPS, 2, SSM_HEADDIM))
    return jnp.transpose(v5, (0, 2, 1, 3, 4)).reshape(depth, 1, SSM_GROUPS * 2 * LANES)


def _rope_tables(n_ctx, n_lat, scale):
    rows = n_lat // GRID_W
    row = jnp.repeat(jnp.arange(rows, dtype=F32), GRID_W)
    col = jnp.tile(jnp.arange(GRID_W, dtype=F32), rows)
    n_freq = MLA_ROPE // 4
    inv = ROPE_BASE ** (-jnp.arange(n_freq, dtype=F32) / n_freq)
    ang = jnp.stack([row[:, None] * inv, col[:, None] * inv], axis=1)
    cos, sin = jnp.cos(ang), jnp.sin(ang)
    zf = jnp.zeros((n_lat, ROPE_F), F32)
    c_r = jnp.concatenate([cos[:, 0], cos[:, 0], zf, cos[:, 1], cos[:, 1], zf], axis=-1)
    s_r = jnp.concatenate([-sin[:, 0], sin[:, 0], zf, -sin[:, 1], sin[:, 1], zf], axis=-1)
    pad = jnp.zeros((n_lat, LANES - MLA_NOPE - ROPE_LANES), F32)
    c_lat = jnp.concatenate([jnp.ones((n_lat, MLA_NOPE), F32), c_r, pad], axis=-1)
    s_lat = jnp.concatenate([jnp.zeros((n_lat, MLA_NOPE), F32), s_r, pad], axis=-1)
    keep = jnp.asarray(np.concatenate([np.ones(MLA_NOPE), np.tile(np.repeat([1.0, 1.0, 0.0], ROPE_F), 2),
                                       np.zeros(LANES - MLA_NOPE - ROPE_LANES)]), F32)[None, :]
    c_all = jnp.concatenate([c_lat, jnp.broadcast_to(keep, (n_ctx, LANES))], axis=0)
    s_all = jnp.concatenate([s_lat, jnp.zeros((n_ctx, LANES), F32)], axis=0)
    return c_all * scale, s_all * scale, c_all, s_all


def kernel(x, c, ctx, c_ctx, w_ada, b_ada, norm1_g, norm2_g, w_in, mla_qa_g, mla_wqb, mla_kva_g, mla_wkvb, hg_lb_logits, hg_norm_g, ssm_conv_w, ssm_conv_b, ssm_dt_bias, ssm_a_log, ssm_d, ssm_norm_g, w_out, w_ffn_in, w_ffn_out, final_g):
    bsz, n_lat, d = x.shape
    n_ctx = ctx.shape[1]
    depth = w_ada.shape[0]
    assert n_lat % TOKEN_TILE == 0 and n_lat % GRID_W == 0 and n_ctx <= TOKEN_TILE
    assert n_ctx % (HG_BATCH * HG_CHUNK) == 0 and n_ctx % SSM_CHUNK == 0 and n_ctx % ROW_TILE == 0

    rows = -(-(bsz + 1) // SUBLANES) * SUBLANES
    cc = jnp.zeros((rows, d), F32).at[:bsz].set(c).at[bsz].set(c_ctx)
    mods = _mod_call(cc, w_ada.astype(BF16), b_ada.reshape(depth, 1, 6 * d))
    modsel = jnp.stack([mods[:, :bsz], jnp.broadcast_to(mods[:, bsz:bsz + 1], (depth, bsz, 6 * d))], axis=2)[:, :, :, None, :]

    w_in_p = _gather_cols(w_in, _win_perm()).astype(BF16)
    head_cols = np.full((LANES,), -1, np.int64)
    head_cols[:MLA_NOPE] = np.arange(MLA_NOPE)
    head_cols[MLA_NOPE:MLA_NOPE + ROPE_LANES] = MLA_NOPE + ROPE_PERM
    wqb_perm = np.concatenate([np.where(head_cols >= 0, hh * MLA_QK + head_cols, -1) for hh in range(MLA_HEADS)])
    wqb_p = _gather_cols(mla_wqb, wqb_perm).astype(BF16)
    wkvb_b = mla_wkvb.astype(BF16)
    cperm = _conv_perm()
    conv_w_p = jnp.take(ssm_conv_w, jnp.asarray(cperm, jnp.int32), axis=-1)
    conv_b_p = jnp.take(ssm_conv_b, jnp.asarray(cperm, jnp.int32), axis=-1).reshape(depth, 1, SSM_XBC)
    dtb_p = _head_vec(ssm_dt_bias)
    alog_p = _head_vec(ssm_a_log)
    dsk_p = jnp.repeat(ssm_d, SSM_HEADDIM, axis=-1).reshape(depth, 1, SSM_INNER)
    lb_soft = jax.nn.softmax(hg_lb_logits.astype(F32), axis=0)
    lb_all = jnp.cumsum(lb_soft, axis=0) - lb_soft[0]
    w_out_b = w_out.astype(BF16)
    a_w = MLA_HEADS * MLA_V
    r_w = HG_HEADS * HG_DV
    w1_b = w_ffn_in.astype(BF16)
    w2_b = w_ffn_out.astype(BF16)
    tabs = _rope_tables(n_ctx, n_lat, MLA_QK ** -0.5 * float(np.log2(np.e)))

    streams = [x, ctx]
    for l in range(depth):
        q, k, v, p_hg, p_ssm = _inproj_call(
            *streams, modsel[l], norm1_g[l].reshape(1, d), w_in_p[l], mla_qa_g[l].reshape(1, -1),
            mla_kva_g[l].reshape(1, -1), wqb_p[l], wkvb_b[l], tabs)
        a = _attn_call(q, k, v, n_lat)
        r = _hgrn_call(p_hg, lb_all[l], hg_norm_g[l].reshape(1, -1), n_lat)
        s = _ssd_call(p_ssm, conv_w_p[l], conv_b_p[l], dtb_p[l], alog_p[l], dsk_p[l],
                      ssm_norm_g[l].reshape(1, -1), n_lat)
        new = _ffn_call(*streams, a, r, s, modsel[l], norm2_g[l].reshape(1, d), final_g.reshape(1, d),
                        w_out_b[l, :a_w], w_out_b[l, a_w:a_w + r_w], w_out_b[l, a_w + r_w:], w1_b[l], w2_b[l],
                        final=(l == depth - 1))
        streams = [new[0], new[1] if len(new) > 1 else streams[1]]
    return streams[0]
```

```python
import functools

import numpy as np
import jax
import jax.numpy as jnp
from jax import lax
from jax.experimental import pallas as pl
from jax.experimental.pallas import tpu as pltpu

F32 = jnp.float32
BF16 = jnp.bfloat16
EPS = 1e-6

LANES = 128
SUBLANES = 8
MXU_TILE = 256
VMEM_BYTES = 64 * 1024 * 1024

GRID_W = 64
MLA_HEADS = 8
MLA_Q_RANK = 384
MLA_KV_RANK = 256
MLA_NOPE = 64
MLA_ROPE = 32
MLA_V = 64
MLA_QK = MLA_NOPE + MLA_ROPE
ROPE_BASE = 10000.0
HG_HEADS = 4
HG_DK = 64
HG_DV = 64
HG_W = HG_HEADS * HG_DK
SSM_HEADS = 4
SSM_HEADDIM = 64
SSM_GROUPS = 2
SSM_STATE = 64
SSM_CONV = 5
SSM_INNER = SSM_HEADS * SSM_HEADDIM
SSM_XBC = SSM_INNER + 2 * SSM_GROUPS * SSM_STATE

ROPE_F = MLA_ROPE // 4
ROPE_PERM = np.concatenate([np.concatenate([a * 2 * ROPE_F + np.arange(2 * ROPE_F), a * 2 * ROPE_F + np.arange(ROPE_F)])
                            for a in range(2)])
ROPE_LANES = ROPE_PERM.size

MLA_COLS = MLA_Q_RANK + MLA_KV_RANK + MLA_ROPE
HG_COLS = 3 * HG_W + 2 * HG_HEADS * HG_DV
SSM_COLS = SSM_INNER + SSM_XBC + 2 * SSM_HEADS

P_QLAT = 0
P_KVLAT = MLA_Q_RANK
P_KR = P_KVLAT + MLA_KV_RANK
P_HG = P_KR + LANES
P_SSM = P_HG + HG_COLS
SSM_BLOCK = SSM_INNER + 2 * 2 * LANES + 2 * 2 * LANES
P_TOTAL = P_SSM + SSM_BLOCK

TOKEN_TILE = 512
ROW_TILE = 256
ATTN_TILES = 2
HG_CHUNK = 128
HG_BATCH = 2
SSM_CHUNK = 128
SSM_BATCH = 8
FFN_SPLIT = 2

NT_DIMS = (((1,), (1,)), ((), ()))
TN_DIMS = (((0,), (0,)), ((), ()))


def _vmem_limit(nbytes):
    return int(min(VMEM_BYTES - 8 * 1024 * 1024, max(nbytes, 16 * 1024 * 1024)))


def _sigmoid_pair(x):
    e = jnp.exp(-jnp.abs(x))
    d = 1.0 / (1.0 + e)
    ed = e * d
    pos = x >= 0
    return jnp.where(pos, d, ed), jnp.where(pos, ed, d)


def _silu(x):
    return x / (1.0 + jnp.exp(-x))


def _rms(x, g):
    ms = jnp.mean(x * x, axis=-1, keepdims=True)
    return x * lax.rsqrt(ms + EPS) * g


def _dot(a, b):
    return jnp.dot(a, b, preferred_element_type=F32)


def _dot_nt(a, b):
    return lax.dot_general(a, b, NT_DIMS, preferred_element_type=F32)


def _dot_tn(a, b):
    return lax.dot_general(a, b, TN_DIMS, preferred_element_type=F32)


def _lat_or_ctx(tile, tm, ctx_rows, i, n):
    if ctx_rows is None:
        tile(tm, False)
        return
    last = i == n - 1
    pl.when(last)(lambda: tile(ctx_rows, True))
    pl.when(jnp.logical_not(last))(lambda: tile(tm, False))


def _split3(x):
    x1 = x.astype(BF16)
    r1 = x - x1.astype(F32)
    x2 = r1.astype(BF16)
    x3 = (r1 - x2.astype(F32)).astype(BF16)
    return x1, x2, x3


def _cumsum_mm(tri, x):
    x1, x2, x3 = _split3(x)
    return _dot(tri, x1) + _dot(tri, x2) + _dot(tri, x3)


def _mod_kernel(c_ref, w_ref, b_ref, o_ref):
    s = _silu(c_ref[...]).astype(BF16)
    o_ref[0] = _dot(s, w_ref[0]) + b_ref[0]


def _mod_call(cc, w_ada, b_ada):
    depth, d, six_d = w_ada.shape
    rows = cc.shape[0]
    tn = 1536
    return pl.pallas_call(
        _mod_kernel,
        grid=(depth, six_d // tn),
        in_specs=[
            pl.BlockSpec((rows, d), lambda l, j: (0, 0)),
            pl.BlockSpec((1, d, tn), lambda l, j: (l, 0, j)),
            pl.BlockSpec((1, 1, tn), lambda l, j: (l, 0, j)),
        ],
        out_specs=pl.BlockSpec((1, rows, tn), lambda l, j: (l, 0, j)),
        out_shape=jax.ShapeDtypeStruct((depth, rows, six_d), F32),
        name="adaln_mod",
    )(cc, w_ada, b_ada)


def _rope(xh, c, s):
    return xh * c + pltpu.roll(xh, LANES - ROPE_F, 1) * s


def _inproj_kernel(xl_ref, xc_ref, mod_ref, g1_ref, win_ref, qag_ref, kvag_ref, wqb_ref, wkvb_ref,
                   cq_ref, sq_ref, ck_ref, sk_ref,
                   q_ref, k_ref, v_ref, hg_ref, ssm_ref):
    d = xl_ref.shape[2]

    def tile(rows, is_ctx):
        r = slice(0, rows)
        x = xc_ref[0] if is_ctx else xl_ref[0]
        sh = mod_ref[0, 0, :, 0:d]
        sc = mod_ref[0, 0, :, d:2 * d]
        h = _rms(x, g1_ref[...]) * (1.0 + sc) + sh
        p = _dot(h.astype(BF16), win_ref[...])
        hg_ref[0, r, :] = p[:, P_HG:P_SSM]
        ssm_ref[0, r, :] = p[:, P_SSM:P_TOTAL]
        q_lat = _rms(p[:, P_QLAT:P_KVLAT], qag_ref[...])
        kv_lat = _rms(p[:, P_KVLAT:P_KR], kvag_ref[...])
        q = _dot(q_lat.astype(BF16), wqb_ref[...])
        kv = _dot(kv_lat.astype(BF16), wkvb_ref[...])
        lane = lax.broadcasted_iota(jnp.int32, (rows, LANES), 1)
        cq, sq, ck, sk = cq_ref[r, :], sq_ref[r, :], ck_ref[r, :], sk_ref[r, :]
        kr = _rope(p[:, P_KR:P_HG], ck, sk)
        nope = lane < MLA_NOPE
        for hh in range(MLA_HEADS):
            sl = slice(hh * LANES, (hh + 1) * LANES)
            q_ref[0, hh, r, :] = _rope(q[:, sl], cq, sq).astype(q_ref.dtype)
            kvh = kv[:, sl]
            k_ref[0, hh, r, :] = jnp.where(nope, kvh, kr).astype(k_ref.dtype)
            v_ref[0, hh, r, :] = jnp.where(nope, 1.0, kvh).astype(v_ref.dtype)

    _lat_or_ctx(tile, xl_ref.shape[1], xc_ref.shape[1], pl.program_id(1), pl.num_programs(1))


def _inproj_call(x_lat, x_ctx, modsel, g1, w_in, qa_g, kva_g, wqb, wkvb, tabs):
    b, n_lat, d = x_lat.shape
    n_ctx = x_ctx.shape[1]
    n_all = n_lat + n_ctx
    tm = TOKEN_TILE
    nlt = n_lat // tm
    nt = nlt + 1
    full = lambda shape: pl.BlockSpec(shape, lambda bb, i: (0,) * len(shape), pipeline_mode=pl.Buffered(1))
    tab = pl.BlockSpec((tm, LANES), lambda bb, i: (i, 0))
    head_out = pl.BlockSpec((1, MLA_HEADS, tm, LANES), lambda bb, i: (bb, 0, i, 0))
    est = (w_in.size + wqb.size + wkvb.size) * 2 + 2 * (tm * d * 4 + 3 * MLA_HEADS * tm * LANES * 2
                                                        + tm * (HG_COLS + SSM_BLOCK) * 4) + 4 * tm * P_TOTAL * 4
    return pl.pallas_call(
        _inproj_kernel,
        grid=(b, nt),
        in_specs=[
            pl.BlockSpec((1, tm, d), lambda bb, i: (bb, jnp.minimum(i, nlt - 1), 0)),
            pl.BlockSpec((1, n_ctx, d), lambda bb, i: (bb, 0, 0)),
            pl.BlockSpec((1, 1, 1, modsel.shape[3]), lambda bb, i: (bb, i // nlt, 0, 0)),
            full((1, d)),
            full(w_in.shape),
            full((1, MLA_Q_RANK)),
            full((1, MLA_KV_RANK)),
            full(wqb.shape),
            full(wkvb.shape),
            tab, tab, tab, tab,
        ],
        out_specs=[
            head_out, head_out, head_out,
            pl.BlockSpec((1, tm, HG_COLS), lambda bb, i: (bb, i, 0)),
            pl.BlockSpec((1, tm, SSM_BLOCK), lambda bb, i: (bb, i, 0)),
        ],
        out_shape=[
            jax.ShapeDtypeStruct((b, MLA_HEADS, n_all, LANES), BF16),
            jax.ShapeDtypeStruct((b, MLA_HEADS, n_all, LANES), BF16),
            jax.ShapeDtypeStruct((b, MLA_HEADS, n_all, LANES), BF16),
            jax.ShapeDtypeStruct((b, n_all, HG_COLS), F32),
            jax.ShapeDtypeStruct((b, n_all, SSM_BLOCK), F32),
        ],
        compiler_params=pltpu.CompilerParams(vmem_limit_bytes=_vmem_limit(est)),
        name="inproj_mla",
    )(x_lat, x_ctx, modsel, g1, w_in, qa_g, kva_g, wqb, wkvb, *tabs)


def _attn_kernel(q_ref, k_ref, v_ref, o_ref, *, n_lat, tq):
    n_all = k_ref.shape[2]
    n_sub = q_ref.shape[2] // tq

    def run(r0, rows, k0, k1):
        r = slice(r0, r0 + rows)
        ss = [_dot_nt(q_ref[0, j, r, :], k_ref[0, j, k0:k1, :]) for j in range(2)]
        ps = [jnp.exp2(s - jnp.max(s, axis=-1, keepdims=True)).astype(BF16) for s in ss]
        outs = []
        for j in range(2):
            o = _dot(ps[j], v_ref[0, j, k0:k1, :])
            outs.append(o / pltpu.roll(o, MLA_V, 1))
        lane = lax.broadcasted_iota(jnp.int32, (rows, LANES), 1)
        o_ref[0, r, :] = jnp.where(lane < MLA_V, pltpu.roll(outs[0], MLA_V, 1), outs[1]).astype(o_ref.dtype)

    is_ctx = pl.program_id(2) >= n_lat // (n_sub * tq)

    @pl.when(is_ctx)
    def _():
        run(0, n_all - n_lat, n_lat, n_all)

    @pl.when(jnp.logical_not(is_ctx))
    def _():
        for sub in range(n_sub):
            run(sub * tq, tq, 0, n_all)


def _attn_call(q, k, v, n_lat):
    b, h, n_all, _ = q.shape
    tq = TOKEN_TILE
    tb = ATTN_TILES * tq
    assert n_lat % tb == 0 and n_all - n_lat <= tb
    kv_spec = pl.BlockSpec((1, 2, n_all, LANES), lambda bb, hp, i: (bb, hp, 0, 0))
    est = 2 * (2 * tb * LANES * 2 + 2 * 2 * n_all * LANES * 2 + tb * LANES * 2) + 7 * tq * n_all * 4
    return pl.pallas_call(
        functools.partial(_attn_kernel, n_lat=n_lat, tq=tq),
        grid=(b, h // 2, pl.cdiv(n_all, tb)),
        in_specs=[pl.BlockSpec((1, 2, tb, LANES), lambda bb, hp, i: (bb, hp, i, 0)), kv_spec, kv_spec],
        out_specs=pl.BlockSpec((1, tb, LANES), lambda bb, hp, i: (bb, i, hp)),
        out_shape=jax.ShapeDtypeStruct((b, n_all, h * MLA_V), BF16),
        compiler_params=pltpu.CompilerParams(vmem_limit_bytes=_vmem_limit(est)),
        name="mla_attention",
    )(q, k, v)


def _hg_consts(c):
    ri = lax.broadcasted_iota(jnp.int32, (c, c), 0)
    ci = lax.broadcasted_iota(jnp.int32, (c, c), 1)
    cst = {
        "tril": jnp.where(ci <= ri, 1.0, 0.0).astype(BF16),
        "triu": jnp.where(ci >= ri, 1.0, 0.0).astype(BF16),
        "lvl": {},
    }
    h = SUBLANES
    while 2 * h <= c:
        same =jnp.right_shift(ri, int(np.log2(2 * h))) == jnp.right_shift(ci, int(np.log2(2 * h)))
        r_lo = jnp.bitwise_and(ri, 2 * h - 1) < h
        c_lo = jnp.bitwise_and(ci, 2 * h - 1) < h
        fwd = jnp.where(same, jnp.where(r_lo, 0.0, jnp.where(c_lo, 1.0, 0.0)), 0.0)
        bwd = jnp.where(same, jnp.where(r_lo, jnp.where(c_lo, 0.0, 1.0), 0.0), 0.0)
        cst["lvl"][(h, False)] = fwd
        cst["lvl"][(h, True)] = bwd
        h *= 2
    r2 = lax.broadcasted_iota(jnp.int32, (LANES, LANES), 0)
    c2 = lax.broadcasted_iota(jnp.int32, (LANES, LANES), 1)
    bd = (r2 < HG_DK) == (c2 < HG_DK)
    cst["bd"] = bd
    cst["bo"] = jnp.where(bd, 1.0, 0.0).astype(BF16)
    lane = lax.broadcasted_iota(jnp.int32, (c, LANES), 1)
    cst["hm"] = [lane < HG_DK, lane >= HG_DK]
    cst["rit"] = jnp.bitwise_and(lax.broadcasted_iota(jnp.int32, (c, LANES), 0), SUBLANES - 1)
    return cst


def _hg_level_ref(b, h, rev):
    c = b.shape[0]
    off = h if rev else h - 1
    pieces = [jnp.broadcast_to(b[blk * 2 * h + off:blk * 2 * h + off + 1, :], (2 * h, LANES))
              for blk in range(c // (2 * h))]
    return pieces[0] if len(pieces) == 1 else jnp.concatenate(pieces, axis=0)


def _hg_intra(streams, *, cst):
    c = streams[0][0].shape[0]
    ns = len(streams)
    bs = [_cumsum_mm(cst["triu"] if s[5] else cst["tril"], s[3]) for s in streams]
    tots = [b[0:1, :] if s[5] else b[c - 1:c, :] for b, s in zip(bs, streams)]
    qes = [(s[0] * jnp.exp2(b)).astype(BF16) for b, s in zip(bs, streams)]
    atts = [[None, None] for _ in range(ns)]
    sign = jnp.uint32(0x80000000)
    h = SUBLANES
    while 2 * h <= c:
        ops = []
        for si, (b, (q, k, f, g, v, rev)) in enumerate(zip(bs, streams)):
            ref = _hg_level_ref(b, h, rev)
            nabs = pltpu.bitcast(pltpu.bitcast(b - ref, jnp.uint32) | sign, F32)
            e = jnp.exp2(nabs)
            qh = q * e
            ops.append(([jnp.where(cst["hm"][j], qh, 0.0).astype(BF16) for j in range(2)], (k * e).astype(BF16)))
        prods = [[_dot_nt(qj, kh) for qj in qjs] for qjs, kh in ops]
        for si in range(ns):
            msk = cst["lvl"][(h, streams[si][5])]
            for j in range(2):
                t = prods[si][j] * msk
                atts[si][j] = t if atts[si][j] is None else atts[si][j] + t
        h *= 2
    outs = []
    for si, (q, k, f, g, v, rev) in enumerate(streams):
        o = None
        for j in range(2):
            vj = jnp.where(cst["hm"][j], v, 0.0).astype(BF16)
            oj = _dot(atts[si][j].astype(BF16), vj)
            o = oj if o is None else o + oj
        outs.append(o)

    def shift(x, jj, rev):
        if jj == 0:
            return x
        x3 = x.reshape(c // SUBLANES, SUBLANES, LANES)
        return pltpu.roll(x3, (SUBLANES - jj) if rev else jj, 1).reshape(c, LANES)

    es = [None] * ns
    for dd in range(SUBLANES):
        ws = []
        for si, (q, k, f, g, v, rev) in enumerate(streams):
            if dd == 0:
                w = q * k
            else:
                fd = shift(f, dd - 1, rev)
                es[si] = fd if es[si] is None else es[si] * fd
                valid = (cst["rit"] <= SUBLANES - 1 - dd) if rev else (cst["rit"] >= dd)
                w = jnp.where(valid, q * shift(k, dd, rev) * es[si], 0.0)
            ws.append(w.astype(BF16))
        sums = [_dot(w, cst["bo"]) for w in ws]
        for si, (q, k, f, g, v, rev) in enumerate(streams):
            outs[si] = outs[si] + sums[si] * shift(v, dd, rev)
    res = []
    for si, (q, k, f, g, v, rev) in enumerate(streams):
        ke = (k * jnp.exp2(tots[si] - bs[si])).astype(BF16)
        upd = jnp.where(cst["bd"], _dot_tn(v.astype(BF16), ke), 0.0)
        res.append((outs[si], qes[si], upd, jnp.exp2(tots[si])))
    return res


def _hgrn_kernel(q_ref, fff_ref, ffb_ref, iv_ref, og_ref, lb_ref, ng_ref, r_ref,
                 o_ref, qef_ref, qeb_ref, updf_ref, updb_ref, decf_ref, decb_ref, *, n_lat):
    c = HG_CHUNK
    n_all = q_ref.shape[1]
    nc = n_all // c
    ncc = (n_all - n_lat) // c
    cst = _hg_consts(c)
    bo = cst["bo"]
    ng = ng_ref[...]
    tr = ROW_TILE
    dirs = ((False, fff_ref, lb_ref[0:1, :], qef_ref, updf_ref, decf_ref),
            (True, ffb_ref, lb_ref[1:2, :], qeb_ref, updb_ref, decb_ref))
    nb = HG_BATCH
    assert nc % nb == 0

    def intra(i, carry):
        streams, where = [], []
        for u in range(nb):
            ci = i * nb + u
            rows = pl.ds(pl.multiple_of(ci * c, c), c)
            q = _silu(q_ref[0, rows, :])
            v = iv_ref[0, rows, :]
            for rev, ff_ref, lb, qe_ref, upd_ref, dec_ref in dirs:
                sig, sigm = _sigmoid_pair(ff_ref[0, rows, :])
                f = lb + (1.0 - lb) * sig
                streams.append((q, (1.0 - lb) * sigm, f, jnp.log2(f), v, rev))
                where.append((ci, rows, qe_ref, upd_ref, dec_ref))
        res = _hg_intra(streams, cst=cst)
        for u in range(nb):
            o_ref[where[2 * u][1], :] = res[2 * u][0] + res[2 * u + 1][0]
        for (_, qe, upd, dec), (ci, rows, qe_ref, upd_ref, dec_ref) in zip(res, where):
            qe_ref[rows, :] = qe
            upd_ref[ci] = upd
            dec_ref[ci] = jnp.broadcast_to(dec, (SUBLANES, LANES))
        return carry

    lax.fori_loop(0, nc // nb, intra, 0)

    def scan(i, carry):
        st_f, st_b = carry
        jf = jnp.where(i < ncc, nc - ncc + i, i - ncc)
        rows = pl.ds(pl.multiple_of(jf * c, c), c)
        o_ref[rows, :] = o_ref[rows, :] + _dot_nt(qef_ref[rows, :], st_f.astype(BF16))
        st_f = st_f * decf_ref[jf][0:1, :] + updf_ref[jf]
        ib = nc - 1 - i
        rows = pl.ds(pl.multiple_of(ib * c, c), c)
        o_ref[rows, :] = o_ref[rows, :] + _dot_nt(qeb_ref[rows, :], st_b.astype(BF16))
        st_b = st_b * decb_ref[ib][0:1, :] + updb_ref[ib]
        return st_f, st_b

    zero = jnp.zeros((LANES, LANES), F32)
    lax.fori_loop(0, nc, scan, (zero, zero), unroll=2)

    def readout(i, carry):
        rows = pl.ds(pl.multiple_of(i * tr, tr), tr)
        o = o_ref[rows, :]
        ms = _dot((o * o).astype(BF16), bo) * (1.0 / HG_DV)
        r = o * lax.rsqrt(ms + EPS) * ng * _silu(og_ref[0, rows, :])
        r_ref[0, rows, :] = r.astype(r_ref.dtype)
        return carry

    lax.fori_loop(0, n_all // tr, readout, 0)


def _hgrn_call(p_hg, lb, ng, n_lat):
    b, n_all, _ = p_hg.shape
    nc = n_all // HG_CHUNK
    col = lambda j: pl.BlockSpec((1, n_all, LANES), lambda bb, pr: (bb, 0, 2 * j + pr))
    scratch = [pltpu.VMEM((n_all, LANES), F32), pltpu.VMEM((n_all, LANES), BF16), pltpu.VMEM((n_all, LANES), BF16),
               pltpu.VMEM((nc, LANES, LANES), F32), pltpu.VMEM((nc, LANES, LANES), F32),
               pltpu.VMEM((nc, SUBLANES, LANES), F32), pltpu.VMEM((nc, SUBLANES, LANES), F32)]
    est = 2 * 6 * n_all * LANES * 4 + 2 * n_all * LANES * 4 + 2 * nc * LANES * LANES * 4 + 8 * 1024 * 1024
    return pl.pallas_call(
        functools.partial(_hgrn_kernel, n_lat=n_lat),
        grid=(b, 2),
        in_specs=[col(0), col(1), col(2), col(3), col(4),
                  pl.BlockSpec((2, LANES), lambda bb, pr: (0, pr)),
                  pl.BlockSpec((1, LANES), lambda bb, pr: (0, pr))],
        out_specs=pl.BlockSpec((1, n_all, LANES), lambda bb, pr: (bb, 0, pr)),
        out_shape=jax.ShapeDtypeStruct((b, n_all, HG_HEADS * HG_DV), BF16),
        scratch_shapes=scratch,
        compiler_params=pltpu.CompilerParams(vmem_limit_bytes=_vmem_limit(est)),
        name="hgrn2_bidir",
    )(p_hg, p_hg, p_hg, p_hg, p_hg, lb, ng)


def _ssd_intra(chunks, *, cst, dtb, a_neg):
    n = chunks[0][0].shape[0]
    lo = cst["lane"] < SSM_STATE
    hi = jnp.logical_not(lo)
    xss = [xa[:, 0:LANES] for xa, _ in chunks]
    bms, cms = [], []
    for xa, _ in chunks:
        bc = xa[:, LANES:2 * LANES]
        bms.append(jnp.where(lo, bc, 0.0).astype(BF16))
        cms.append(jnp.where(lo, pltpu.roll(bc, SSM_STATE, 1), 0.0).astype(BF16))
    gmats = [_dot_nt(cm, bm) for cm, bm in zip(cms, bms)]
    pairs = [(ci, d) for ci in range(len(chunks)) for d in range(2)]
    dts, css = [], []
    for ci, d in pairs:
        sl = slice(d * LANES, (d + 1) * LANES)
        xdt = chunks[ci][1][:, sl] + dtb[:, sl]
        dt = jnp.maximum(xdt, 0.0) + jnp.log(1.0 + jnp.exp(-jnp.abs(xdt)))
        dts.append(dt)
        css.append(_cumsum_mm(cst["triu"] if d else cst["tril"], dt * a_neg[:, sl]))
    tots = [cs[0:1, :] if d else cs[n - 1:n, :] for cs, (ci, d) in zip(css, pairs)]
    csrs = [pltpu.roll(cs, SSM_HEADDIM, 1) for cs in css]
    csts = [cs.T for cs in css]
    xds = [xss[ci] * dt for dt, (ci, d) in zip(dts, pairs)]
    lhs, rhs = [], []
    for pi, (ci, d) in enumerate(pairs):
        for j in range(2):
            colb = jnp.where(lo, css[pi], csrs[pi]) if j == 0 else jnp.where(lo, csrs[pi], css[pi])
            rowb = jnp.broadcast_to(csts[pi][j * SSM_HEADDIM:j * SSM_HEADDIM + 1, :], (n, n))
            lmat = jnp.where(cst["tri_mask_b" if d else "tri_mask_f"], jnp.exp2(jnp.minimum(colb - rowb, 0.0)), 0.0)
            lhs.append((gmats[ci] * lmat).astype(BF16))
            rhs.append(jnp.where(lo if j == 0 else hi, xds[pi], 0.0).astype(BF16))
    prods = [_dot(a, b) for a, b in zip(lhs, rhs)]
    xins = [(xd * jnp.exp2(tot - cs)).astype(BF16) for xd, tot, cs in zip(xds, tots, css)]
    upds = [_dot_tn(bms[ci], xin) for xin, (ci, d) in zip(xins, pairs)]
    res = []
    for ci in range(len(chunks)):
        y = prods[4 * ci] + prods[4 * ci + 1] + prods[4 * ci + 2] + prods[4 * ci + 3]
        outs = [(jnp.exp2(css[2 * ci + d]), upds[2 * ci + d], jnp.exp2(tots[2 * ci + d])) for d in range(2)]
        res.append((y, cms[ci], outs))
    return res


def _ssd_kernel(z_ref, xbc_ref, dt_ref, cw_ref, cb_ref, dtb_ref, alog_ref, dsk_ref, ng_ref,
                y_ref, xa_ref, ya_ref, cm_ref, dof_ref, dob_ref, updf_ref, updb_ref, decf_ref, decb_ref, *, n_lat):
    c = SSM_CHUNK
    n_all = z_ref.shape[1]
    nc = n_all // c
    ncc = (n_all - n_lat) // c
    nl = nc - ncc
    ri = lax.broadcasted_iota(jnp.int32, (c, c), 0)
    ci_ = lax.broadcasted_iota(jnp.int32, (c, c), 1)
    cst = {
        "tril": jnp.where(ci_ <= ri, 1.0, 0.0).astype(BF16),
        "triu": jnp.where(ci_ >= ri, 1.0, 0.0).astype(BF16),
        "tri_mask_f": ci_ <= ri,
        "tri_mask_b": ci_ >= ri,
        "lane": lax.broadcasted_iota(jnp.int32, (c, LANES), 1),
    }
    dtb = dtb_ref[...]
    a_neg = -jnp.exp(alog_ref[...]) * float(np.log2(np.e))
    cb = cb_ref[...]
    halo = SUBLANES

    def conv_body(ci, carry):
        r0 = pl.multiple_of(ci * c, c)
        cur = xbc_ref[0, pl.ds(r0, c), :]
        first = jnp.logical_or(ci == 0, ci == nl)
        last = jnp.logical_or(ci == nl - 1, ci == nc - 1)
        rp = pl.multiple_of(jnp.maximum(r0 - halo, 0), halo)
        rn = pl.multiple_of(jnp.minimum(r0 + c, n_all - halo), halo)
        prev = xbc_ref[0, pl.ds(rp, halo), :] * jnp.where(first, 0.0, 1.0)
        nxt = xbc_ref[0, pl.ds(rn, halo), :] * jnp.where(last, 0.0, 1.0)
        ext = jnp.concatenate([prev, cur, nxt], axis=0)
        acc = jnp.broadcast_to(cb, (c, 2 * LANES))
        for j in range(SSM_CONV):
            s = (SSM_CONV // 2 - j) % (c + 2 * halo)
            sh = ext if s == 0 else pltpu.roll(ext, s, 0)
            acc = acc + cw_ref[j:j + 1, :] * sh[halo:halo + c, :]
        xa_ref[pl.ds(r0, c), :] = _silu(acc)
        return carry

    lax.fori_loop(0, nc, conv_body, 0)

    def intra(i, carry, nb, first):
        cis = [first + i * nb + u for u in range(nb)]
        rws = [pl.ds(pl.multiple_of(ci * c, c), c) for ci in cis]
        res = _ssd_intra([(xa_ref[rows, :], dt_ref[0, rows, :]) for rows in rws], cst=cst, dtb=dtb, a_neg=a_neg)
        for ci, rows, (y, cm, outs) in zip(cis, rws, res):
            ya_ref[rows, :] = y
            cm_ref[rows, :] = cm
            for (dec_out, upd, dec), do_ref, upd_ref, dec_ref in zip(outs, (dof_ref, dob_ref), (updf_ref, updb_ref), (decf_ref, decb_ref)):
                do_ref[rows, :] = dec_out
                upd_ref[ci] = upd
                dec_ref[ci] = jnp.broadcast_to(dec, (SUBLANES, LANES))
        return carry

    nbl = SSM_BATCH if nl % SSM_BATCH == 0 else 1
    nbc = ncc if ncc <= SSM_BATCH else 1
    lax.fori_loop(0, nl // nbl, functools.partial(intra, nb=nbl, first=0), 0)
    lax.fori_loop(0, ncc // nbc, functools.partial(intra, nb=nbc, first=nl), 0)

    def scan(i, carry):
        st_f, st_b = carry
        jf = jnp.where(i < ncc, nl + i, i - ncc)
        rows = pl.ds(pl.multiple_of(jf * c, c), c)
        ya_ref[rows, :] = ya_ref[rows, :] + _dot(cm_ref[rows, :], st_f.astype(BF16)) * dof_ref[rows, :]
        st_f = st_f * decf_ref[jf][0:1, :] + updf_ref[jf]
        ib = nc - 1 - i
        rows = pl.ds(pl.multiple_of(ib * c, c), c)
        ya_ref[rows, :] = ya_ref[rows, :] + _dot(cm_ref[rows, :], st_b.astype(BF16)) * dob_ref[rows, :]
        st_b = st_b * decb_ref[ib][0:1, :] + updb_ref[ib]
        return st_f, st_b

    zst = jnp.zeros((LANES, LANES), F32)
    lax.fori_loop(0, nc, scan, (zst, zst), unroll=2)

    dsk = dsk_ref[...]
    ng = ng_ref[...]
    tr = ROW_TILE

    def readout(i, carry):
        rows = pl.ds(pl.multiple_of(i * tr, tr), tr)
        y = ya_ref[rows, :] + dsk * xa_ref[rows, 0:LANES]
        yz = y * _silu(z_ref[0, rows, :])
        y_ref[0, rows, :] = _rms(yz, ng).astype(y_ref.dtype)
        return carry

    lax.fori_loop(0, n_all // tr, readout, 0)


def _ssd_call(p_ssm, conv_w, conv_b, dtb, alog, dsk, ng, n_lat):
    b, n_all, _ = p_ssm.shape
    assert SSM_CHUNK == LANES
    nc = n_all // SSM_CHUNK
    vec = pl.BlockSpec((1, LANES), lambda bb, g: (0, g))
    vec2 = pl.BlockSpec((1, 2 * LANES), lambda bb, g: (0, g))
    scratch = [pltpu.VMEM((n_all, 2 * LANES), F32), pltpu.VMEM((n_all, LANES), F32), pltpu.VMEM((n_all, LANES), BF16),
               pltpu.VMEM((n_all, LANES), F32), pltpu.VMEM((n_all, LANES), F32),
               pltpu.VMEM((nc, LANES, LANES), F32), pltpu.VMEM((nc, LANES, LANES), F32),
               pltpu.VMEM((nc, SUBLANES, LANES), F32), pltpu.VMEM((nc, SUBLANES, LANES), F32)]
    est = 2 * (n_all * 5 * LANES * 4 + n_all * LANES * 2) + n_all * 6 * LANES * 4 + 2 * nc * LANES * LANES * 4 + 8 * 1024 * 1024
    return pl.pallas_call(
        functools.partial(_ssd_kernel, n_lat=n_lat),
        grid=(b, SSM_GROUPS),
        in_specs=[
            pl.BlockSpec((1, n_all, LANES), lambda bb, g: (bb, 0, g)),
            pl.BlockSpec((1, n_all, 2 * LANES), lambda bb, g: (bb, 0, 1 + g)),
            pl.BlockSpec((1, n_all, 2 * LANES), lambda bb, g: (bb, 0, 3 + g)),
            pl.BlockSpec((SSM_CONV, 2 * LANES), lambda bb, g: (0, g)),
            pl.BlockSpec((1, 2 * LANES), lambda bb, g: (0, g)),
            vec2, vec2, vec, vec,
        ],
        out_specs=pl.BlockSpec((1, n_all, LANES), lambda bb, g: (bb, 0, g)),
        out_shape=jax.ShapeDtypeStruct((b, n_all, SSM_INNER), BF16),
        scratch_shapes=scratch,
        compiler_params=pltpu.CompilerParams(vmem_limit_bytes=_vmem_limit(est)),
        name="ssd_bidir",
    )(p_ssm, p_ssm, p_ssm, conv_w, conv_b, dtb, alog, dsk, ng)


def _ffn_kernel(xl_ref, xc_ref, a_ref, r_ref, s_ref, mod_ref, g2_ref, gf_ref, wa_ref, wr_ref, ws_ref, w1_ref, w2_ref,
                ol_ref, oc_ref=None, *, hid_cuts, final):
    d = xl_ref.shape[2]
    hidden = w2_ref.shape[0]
    mod = lambda j: mod_ref[0, 0, :, j * d:(j + 1) * d]

    def tile(rows, is_ctx):
        r = slice(0, rows)
        x_ref, o_ref = (xc_ref, oc_ref) if is_ctx else (xl_ref, ol_ref)
        mix = _dot(a_ref[0, r, :], wa_ref[...]) + _dot(r_ref[0, r, :], wr_ref[...]) + _dot(s_ref[0, r, :], ws_ref[...])
        x1 = x_ref[0] + mod(2) * mix
        h2 = (_rms(x1, g2_ref[...]) * (1.0 + mod(4)) + mod(3)).astype(BF16)
        y = None
        for c0, c1 in zip(hid_cuts[:-1], hid_cuts[1:]):
            ha = _dot(h2, w1_ref[:, c0:c1])
            hb = _dot(h2, w1_ref[:, hidden + c0:hidden + c1])
            act = (_silu(ha) * hb).astype(BF16)
            yc = _dot(act, w2_ref[c0:c1, :])
            y = yc if y is None else y + yc
        x2 = x1 + mod(5) * y
        o_ref[0] = _rms(x2, gf_ref[...]) if final else x2

    _lat_or_ctx(tile, xl_ref.shape[1], None if final else xc_ref.shape[1], pl.program_id(1), pl.num_programs(1))


def _ffn_call(x_lat, x_ctx, a, r, s, modsel, g2, gf, wa, wr, ws, w1, w2, final):
    b, n_lat, d = x_lat.shape
    n_ctx = x_ctx.shape[1]
    tm = TOKEN_TILE
    hidden = w2.shape[0]
    n_tiles = hidden // MXU_TILE
    assert hidden % MXU_TILE == 0
    hid_cuts = tuple(MXU_TILE * ((n_tiles * j + FFN_SPLIT - 1) // FFN_SPLIT) for j in range(FFN_SPLIT + 1))
    hid_chunk = max(b1 - b0 for b0, b1 in zip(hid_cuts[:-1], hid_cuts[1:]))
    nlt = n_lat // tm
    full = lambda arr: pl.BlockSpec(arr.shape, lambda bb, i: (0,) * arr.ndim, pipeline_mode=pl.Buffered(1))
    tok = lambda w: pl.BlockSpec((1, tm, w), lambda bb, i: (bb, i, 0))
    lat = pl.BlockSpec((1, tm, d), lambda bb, i: (bb, jnp.minimum(i, nlt - 1), 0))
    ctx = pl.BlockSpec((1, n_ctx, d), lambda bb, i: (bb, 0, 0))
    est = (wa.size + wr.size + ws.size + w1.size + w2.size) * 2 + 2 * (2 * tm * d * 4 + 2 * n_ctx * d * 4 + tm * 1024 * 2) \
        + 5 * tm * hid_chunk * 4 + 5 * tm * d * 4
    return pl.pallas_call(
        functools.partial(_ffn_kernel, hid_cuts=hid_cuts, final=final),
        grid=(b, nlt if final else nlt + 1),
        in_specs=[
            lat, ctx, tok(a.shape[2]), tok(r.shape[2]), tok(s.shape[2]),
            pl.BlockSpec((1, 1, 1, modsel.shape[3]), lambda bb, i: (bb, i // nlt, 0, 0)),
            pl.BlockSpec((1, d), lambda bb, i: (0, 0)),
            pl.BlockSpec((1, d), lambda bb, i: (0, 0)),
            full(wa), full(wr), full(ws), full(w1), full(w2),
        ],
        out_specs=[lat] if final else [lat, ctx],
        out_shape=[jax.ShapeDtypeStruct((b, n_lat, d), F32)] + ([] if final else [jax.ShapeDtypeStruct((b, n_ctx, d), F32)]),
        compiler_params=pltpu.CompilerParams(vmem_limit_bytes=_vmem_limit(est)),
        name="outproj_ffn",
    )(x_lat, x_ctx, a, r, s, modsel, g2, gf, wa, wr, ws, w1, w2)


def _win_perm():
    perm = np.full((P_TOTAL,), -1, np.int64)
    perm[P_QLAT:P_QLAT + MLA_Q_RANK + MLA_KV_RANK] = np.arange(MLA_Q_RANK + MLA_KV_RANK)
    perm[P_KR + MLA_NOPE:P_KR + MLA_NOPE + ROPE_LANES] = MLA_Q_RANK + MLA_KV_RANK + ROPE_PERM
    perm[P_HG:P_HG + HG_COLS] = MLA_COLS + np.arange(HG_COLS)
    o2 = MLA_COLS + HG_COLS
    perm[P_SSM:P_SSM + SSM_INNER] = o2 + np.arange(SSM_INNER)
    xo = o2 + SSM_INNER
    bo = xo + SSM_INNER
    co = bo + SSM_GROUPS * SSM_STATE
    dto = o2 + SSM_INNER + SSM_XBC
    for g in range(SSM_GROUPS):
        base = P_SSM + SSM_INNER + g * 2 * LANES
        perm[base:base + LANES] = xo + g * LANES + np.arange(LANES)
        perm[base + LANES:base + LANES + SSM_STATE] = bo + g * SSM_STATE + np.arange(SSM_STATE)
        perm[base + LANES + SSM_STATE:base + 2 * LANES] = co + g * SSM_STATE + np.arange(SSM_STATE)
        dbase = P_SSM + SSM_INNER + 2 * 2 * LANES + g * 2 * LANES
        for d in range(2):
            for j in range(2):
                lo = dbase + d * LANES + j * SSM_HEADDIM
                perm[lo:lo + SSM_HEADDIM] = dto + d * SSM_HEADS + 2 * g + j
    return perm


def _conv_perm():
    perm = np.zeros((SSM_XBC,), np.int64)
    for g in range(SSM_GROUPS):
        base = g * 2 * LANES
        perm[base:base + LANES] = g * LANES + np.arange(LANES)
        perm[base + LANES:base + LANES + SSM_STATE] = SSM_INNER + g * SSM_STATE + np.arange(SSM_STATE)
        perm[base + LANES + SSM_STATE:base + 2 * LANES] = SSM_INNER + SSM_GROUPS * SSM_STATE + g * SSM_STATE + np.arange(SSM_STATE)
    return perm


def _gather_cols(w, perm):
    idx = jnp.asarray(np.maximum(perm, 0), jnp.int32)
    out = jnp.take(w, idx, axis=-1)
    return jnp.where(jnp.asarray(perm >= 0), out, 0.0)


def _head_vec(v):
    depth = v.shape[0]
    v5 = v.reshape(depth, 2, SSM_GROUPS, 2, 1)
    v5 = jnp.broadcast_to(v5, (depth, 2, SSM_GROUPS, 2, SSM_HEADDIM))
    return jnp.transpose(v5, (0, 2, 1, 3, 4)).reshape(depth, 1, SSM_GROUPS * 2 * LANES)


def _rope_tables(n_ctx, n_lat, scale):
    rows = n_lat // GRID_W
    row = jnp.repeat(jnp.arange(rows, dtype=F32), GRID_W)
    col = jnp.tile(jnp.arange(GRID_W, dtype=F32), rows)
    n_freq = MLA_ROPE // 4
    inv = ROPE_BASE ** (-jnp.arange(n_freq, dtype=F32) / n_freq)
    ang = jnp.stack([row[:, None] * inv, col[:, None] * inv], axis=1)
    cos, sin = jnp.cos(ang), jnp.sin(ang)
    zf = jnp.zeros((n_lat, ROPE_F), F32)
    c_r = jnp.concatenate([cos[:, 0], cos[:, 0], zf, cos[:, 1], cos[:, 1], zf], axis=-1)
    s_r = jnp.concatenate([-sin[:, 0], sin[:, 0], zf, -sin[:, 1], sin[:, 1], zf], axis=-1)
    pad = jnp.zeros((n_lat, LANES - MLA_NOPE - ROPE_LANES), F32)
    c_lat = jnp.concatenate([jnp.ones((n_lat, MLA_NOPE), F32), c_r, pad], axis=-1)
    s_lat = jnp.concatenate([jnp.zeros((n_lat, MLA_NOPE), F32), s_r, pad], axis=-1)
    keep = jnp.asarray(np.concatenate([np.ones(MLA_NOPE), np.tile(np.repeat([1.0, 1.0, 0.0], ROPE_F), 2),
                                       np.zeros(LANES - MLA_NOPE - ROPE_LANES)]), F32)[None, :]
    c_all = jnp.concatenate([c_lat, jnp.broadcast_to(keep, (n_ctx, LANES))], axis=0)
    s_all = jnp.concatenate([s_lat, jnp.zeros((n_ctx, LANES), F32)], axis=0)
    return c_all * scale, s_all * scale, c_all, s_all


def kernel(x, c, ctx, c_ctx, w_ada, b_ada, norm1_g, norm2_g, w_in, mla_qa_g, mla_wqb, mla_kva_g, mla_wkvb, hg_lb_logits, hg_norm_g, ssm_conv_w, ssm_conv_b, ssm_dt_bias, ssm_a_log, ssm_d, ssm_norm_g, w_out, w_ffn_in, w_ffn_out, final_g):
    bsz, n_lat, d = x.shape
    n_ctx = ctx.shape[1]
    depth = w_ada.shape[0]
    assert n_lat % TOKEN_TILE == 0 and n_lat % GRID_W == 0 and n_ctx <= TOKEN_TILE
    assert n_ctx % (HG_BATCH * HG_CHUNK) == 0 and n_ctx % SSM_CHUNK == 0 and n_ctx % ROW_TILE == 0

    rows = -(-(bsz + 1) // SUBLANES) * SUBLANES
    cc = jnp.zeros((rows, d), F32).at[:bsz].set(c).at[bsz].set(c_ctx)
    mods = _mod_call(cc, w_ada.astype(BF16), b_ada.reshape(depth, 1, 6 * d))
    modsel = jnp.stack([mods[:, :bsz], jnp.broadcast_to(mods[:, bsz:bsz + 1], (depth, bsz, 6 * d))], axis=2)[:, :, :, None, :]

    w_in_p = _gather_cols(w_in, _win_perm()).astype(BF16)
    head_cols = np.full((LANES,), -1, np.int64)
    head_cols[:MLA_NOPE] = np.arange(MLA_NOPE)
    head_cols[MLA_NOPE:MLA_NOPE + ROPE_LANES] = MLA_NOPE + ROPE_PERM
    wqb_perm = np.concatenate([np.where(head_cols >= 0, hh * MLA_QK + head_cols, -1) for hh in range(MLA_HEADS)])
    wqb_p = _gather_cols(mla_wqb, wqb_perm).astype(BF16)
    wkvb_b = mla_wkvb.astype(BF16)
    cperm = _conv_perm()
    conv_w_p = jnp.take(ssm_conv_w, jnp.asarray(cperm, jnp.int32), axis=-1)
    conv_b_p = jnp.take(ssm_conv_b, jnp.asarray(cperm, jnp.int32), axis=-1).reshape(depth, 1, SSM_XBC)
    dtb_p = _head_vec(ssm_dt_bias)
    alog_p = _head_vec(ssm_a_log)
    dsk_p = jnp.repeat(ssm_d, SSM_HEADDIM, axis=-1).reshape(depth, 1, SSM_INNER)
    lb_soft = jax.nn.softmax(hg_lb_logits.astype(F32), axis=0)
    lb_all = jnp.cumsum(lb_soft, axis=0) - lb_soft[0]
    w_out_b = w_out.astype(BF16)
    a_w = MLA_HEADS * MLA_V
    r_w = HG_HEADS * HG_DV
    w1_b = w_ffn_in.astype(BF16)
    w2_b = w_ffn_out.astype(BF16)
    tabs = _rope_tables(n_ctx, n_lat, MLA_QK ** -0.5 * float(np.log2(np.e)))

    streams = [x, ctx]
    for l in range(depth):
        q, k, v, p_hg, p_ssm = _inproj_call(
            *streams, modsel[l], norm1_g[l].reshape(1, d), w_in_p[l], mla_qa_g[l].reshape(1, -1),
            mla_kva_g[l].reshape(1, -1), wqb_p[l], wkvb_b[l], tabs)
        a = _attn_call(q, k, v, n_lat)
        r = _hgrn_call(p_hg, lb_all[l], hg_norm_g[l].reshape(1, -1), n_lat)
        s = _ssd_call(p_ssm, conv_w_p[l], conv_b_p[l], dtb_p[l], alog_p[l], dsk_p[l],
                      ssm_norm_g[l].reshape(1, -1), n_lat)
        new = _ffn_call(*streams, a, r, s, modsel[l], norm2_g[l].reshape(1, d), final_g.reshape(1, d),
                        w_out_b[l, :a_w], w_out_b[l, a_w:a_w + r_w], w_out_b[l, a_w + r_w:], w1_b[l], w2_b[l],
                        final=(l == depth - 1))
        streams = [new[0], new[1] if len(new) > 1 else streams[1]]
    return streams[0]
```

```python
import functools

import numpy as np
import jax
import jax.numpy as jnp
from jax import lax
from jax.experimental import pallas as pl
from jax.experimental.pallas import tpu as pltpu

F32 = jnp.float32
BF16 = jnp.bfloat16
EPS = 1e-6

LANES = 128
SUBLANES = 8
MXU_TILE = 256
VMEM_BYTES = 64 * 1024 * 1024

GRID_W = 64
MLA_HEADS = 8
MLA_Q_RANK = 384
MLA_KV_RANK = 256
MLA_NOPE = 64
MLA_ROPE = 32
MLA_V = 64
MLA_QK = MLA_NOPE + MLA_ROPE
ROPE_BASE = 10000.0
HG_HEADS = 4
HG_DK = 64
HG_DV = 64
HG_W = HG_HEADS * HG_DK
SSM_HEADS = 4
SSM_HEADDIM = 64
SSM_GROUPS = 2
SSM_STATE = 64
SSM_CONV = 5
SSM_INNER = SSM_HEADS * SSM_HEADDIM
SSM_XBC = SSM_INNER + 2 * SSM_GROUPS * SSM_STATE

ROPE_F = MLA_ROPE // 4
ROPE_PERM = np.concatenate([np.concatenate([a * 2 * ROPE_F + np.arange(2 * ROPE_F), a * 2 * ROPE_F + np.arange(ROPE_F)])
                            for a in range(2)])
ROPE_LANES = ROPE_PERM.size

MLA_COLS = MLA_Q_RANK + MLA_KV_RANK + MLA_ROPE
HG_COLS = 3 * HG_W + 2 * HG_HEADS * HG_DV
SSM_COLS = SSM_INNER + SSM_XBC + 2 * SSM_HEADS

P_QLAT = 0
P_KVLAT = MLA_Q_RANK
P_KR = P_KVLAT + MLA_KV_RANK
P_HG = P_KR + LANES
P_SSM = P_HG + HG_COLS
SSM_BLOCK = SSM_INNER + 2 * 2 * LANES + 2 * 2 * LANES
P_TOTAL = P_SSM + SSM_BLOCK

TOKEN_TILE = 512
ROW_TILE = 256
ATTN_TILES = 4
HG_CHUNK = 128
HG_BATCH = 2
SSM_CHUNK = 128
SSM_BATCH = 8
FFN_SPLIT = 2

NT_DIMS = (((1,), (1,)), ((), ()))
TN_DIMS = (((0,), (0,)), ((), ()))


def _vmem_limit(nbytes):
    return int(min(VMEM_BYTES - 8 * 1024 * 1024, max(nbytes, 16 * 1024 * 1024)))


def _sigmoid_pair(x):
    e = jnp.exp(-jnp.abs(x))
    d = 1.0 / (1.0 + e)
    ed = e * d
    pos = x >= 0
    return jnp.where(pos, d, ed), jnp.where(pos, ed, d)


def _silu(x):
    return x / (1.0 + jnp.exp(-x))


def _rms(x, g):
    ms = jnp.mean(x * x, axis=-1, keepdims=True)
    return x * lax.rsqrt(ms + EPS) * g


def _dot(a, b):
    return jnp.dot(a, b, preferred_element_type=F32)


def _dot_nt(a, b):
    return lax.dot_general(a, b, NT_DIMS, preferred_element_type=F32)


def _dot_tn(a, b):
    return lax.dot_general(a, b, TN_DIMS, preferred_element_type=F32)


def _lat_or_ctx(tile, tm, ctx_rows, i, n):
    if ctx_rows is None:
        tile(tm, False)
        return
    last = i == n - 1
    pl.when(last)(lambda: tile(ctx_rows, True))
    pl.when(jnp.logical_not(last))(lambda: tile(tm, False))


def _split3(x):
    x1 = x.astype(BF16)
    r1 = x - x1.astype(F32)
    x2 = r1.astype(BF16)
    x3 = (r1 - x2.astype(F32)).astype(BF16)
    return x1, x2, x3


def _cumsum_mm(tri, x):
    x1, x2, x3 = _split3(x)
    return _dot(tri, x1) + _dot(tri, x2) + _dot(tri, x3)


def _mod_kernel(c_ref, w_ref, b_ref, o_ref):
    s = _silu(c_ref[...]).astype(BF16)
    o_ref[0] = _dot(s, w_ref[0]) + b_ref[0]


def _mod_call(cc, w_ada, b_ada):
    depth, d, six_d = w_ada.shape
    rows = cc.shape[0]
    tn = 1536
    return pl.pallas_call(
        _mod_kernel,
        grid=(depth, six_d // tn),
        in_specs=[
            pl.BlockSpec((rows, d), lambda l, j: (0, 0)),
            pl.BlockSpec((1, d, tn), lambda l, j: (l, 0, j)),
            pl.BlockSpec((1, 1, tn), lambda l, j: (l, 0, j)),
        ],
        out_specs=pl.BlockSpec((1, rows, tn), lambda l, j: (l, 0, j)),
        out_shape=jax.ShapeDtypeStruct((depth, rows, six_d), F32),
        name="adaln_mod",
    )(cc, w_ada, b_ada)


def _rope(xh, c, s):
    return xh * c + pltpu.roll(xh, LANES - ROPE_F, 1) * s


def _inproj_kernel(xl_ref, xc_ref, mod_ref, g1_ref, win_ref, qag_ref, kvag_ref, wqb_ref, wkvb_ref,
                   cq_ref, sq_ref, ck_ref, sk_ref,
                   q_ref, k_ref, v_ref, hg_ref, ssm_ref):
    d = xl_ref.shape[2]

    def tile(rows, is_ctx):
        r = slice(0, rows)
        x = xc_ref[0] if is_ctx else xl_ref[0]
        sh = mod_ref[0, 0, :, 0:d]
        sc = mod_ref[0, 0, :, d:2 * d]
        h = _rms(x, g1_ref[...]) * (1.0 + sc) + sh
        p = _dot(h.astype(BF16), win_ref[...])
        hg_ref[0, r, :] = p[:, P_HG:P_SSM]
        ssm_ref[0, r, :] = p[:, P_SSM:P_TOTAL]
        q_lat = _rms(p[:, P_QLAT:P_KVLAT], qag_ref[...])
        kv_lat = _rms(p[:, P_KVLAT:P_KR], kvag_ref[...])
        q = _dot(q_lat.astype(BF16), wqb_ref[...])
        kv = _dot(kv_lat.astype(BF16), wkvb_ref[...])
        lane = lax.broadcasted_iota(jnp.int32, (rows, LANES), 1)
        cq, sq, ck, sk = cq_ref[r, :], sq_ref[r, :], ck_ref[r, :], sk_ref[r, :]
        kr = _rope(p[:, P_KR:P_HG], ck, sk)
        nope = lane < MLA_NOPE
        for hh in range(MLA_HEADS):
            sl = slice(hh * LANES, (hh + 1) * LANES)
            q_ref[0, hh, r, :] = _rope(q[:, sl], cq, sq).astype(q_ref.dtype)
            kvh = kv[:, sl]
            k_ref[0, hh, r, :] = jnp.where(nope, kvh, kr).astype(k_ref.dtype)
            v_ref[0, hh, r, :] = jnp.where(nope, 1.0, kvh).astype(v_ref.dtype)

    _lat_or_ctx(tile, xl_ref.shape[1], xc_ref.shape[1], pl.program_id(1), pl.num_programs(1))


def _inproj_call(x_lat, x_ctx, modsel, g1, w_in, qa_g, kva_g, wqb, wkvb, tabs):
    b, n_lat, d = x_lat.shape
    n_ctx = x_ctx.shape[1]
    n_all = n_lat + n_ctx
    tm = TOKEN_TILE
    nlt = n_lat // tm
    nt = nlt + 1
    full = lambda shape: pl.BlockSpec(shape, lambda bb, i: (0,) * len(shape), pipeline_mode=pl.Buffered(1))
    tab = pl.BlockSpec((tm, LANES), lambda bb, i: (i, 0))
    head_out = pl.BlockSpec((1, MLA_HEADS, tm, LANES), lambda bb, i: (bb, 0, i, 0))
    est = (w_in.size + wqb.size + wkvb.size) * 2 + 2 * (tm * d * 4 + 3 * MLA_HEADS * tm * LANES * 2
                                                        + tm * (HG_COLS + SSM_BLOCK) * 4) + 4 * tm * P_TOTAL * 4
    return pl.pallas_call(
        _inproj_kernel,
        grid=(b, nt),
        in_specs=[
            pl.BlockSpec((1, tm, d), lambda bb, i: (bb, jnp.minimum(i, nlt - 1), 0)),
            pl.BlockSpec((1, n_ctx, d), lambda bb, i: (bb, 0, 0)),
            pl.BlockSpec((1, 1, 1, modsel.shape[3]), lambda bb, i: (bb, i // nlt, 0, 0)),
            full((1, d)),
            full(w_in.shape),
            full((1, MLA_Q_RANK)),
            full((1, MLA_KV_RANK)),
            full(wqb.shape),
            full(wkvb.shape),
            tab, tab, tab, tab,
        ],
        out_specs=[
            head_out, head_out, head_out,
            pl.BlockSpec((1, tm, HG_COLS), lambda bb, i: (bb, i, 0)),
            pl.BlockSpec((1, tm, SSM_BLOCK), lambda bb, i: (bb, i, 0)),
        ],
        out_shape=[
            jax.ShapeDtypeStruct((b, MLA_HEADS, n_all, LANES), BF16),
            jax.ShapeDtypeStruct((b, MLA_HEADS, n_all, LANES), BF16),
            jax.ShapeDtypeStruct((b, MLA_HEADS, n_all, LANES), BF16),
            jax.ShapeDtypeStruct((b, n_all, HG_COLS), F32),
            jax.ShapeDtypeStruct((b, n_all, SSM_BLOCK), F32),
        ],
        compiler_params=pltpu.CompilerParams(vmem_limit_bytes=_vmem_limit(est)),
        name="inproj_mla",
    )(x_lat, x_ctx, modsel, g1, w_in, qa_g, kva_g, wqb, wkvb, *tabs)


def _attn_kernel(q_ref, k_ref, v_ref, o_ref, *, n_lat, tq):
    n_all = k_ref.shape[2]
    n_sub = q_ref.shape[2] // tq

    def run(r0, rows, k0, k1):
        r = slice(r0, r0 + rows)
        ss = [_dot_nt(q_ref[0, j, r, :], k_ref[0, j, k0:k1, :]) for j in range(2)]
        ps = [jnp.exp2(s - jnp.max(s, axis=-1, keepdims=True)).astype(BF16) for s in ss]
        outs = []
        for j in range(2):
            o = _dot(ps[j], v_ref[0, j, k0:k1, :])
            outs.append(o / pltpu.roll(o, MLA_V, 1))
        lane = lax.broadcasted_iota(jnp.int32, (rows, LANES), 1)
        o_ref[0, r, :] = jnp.where(lane < MLA_V, pltpu.roll(outs[0], MLA_V, 1), outs[1]).astype(o_ref.dtype)

    is_ctx = pl.program_id(2) >= n_lat // (n_sub * tq)

    @pl.when(is_ctx)
    def _():
        run(0, n_all - n_lat, n_lat, n_all)

    @pl.when(jnp.logical_not(is_ctx))
    def _():
        for sub in range(n_sub):
            run(sub * tq, tq, 0, n_all)


def _attn_call(q, k, v, n_lat):
    b, h, n_all, _ = q.shape
    tq = TOKEN_TILE
    tb = ATTN_TILES * tq
    assert n_lat % tb == 0 and n_all - n_lat <= tb
    kv_spec = pl.BlockSpec((1, 2, n_all, LANES), lambda bb, hp, i: (bb, hp, 0, 0))
    est = 2 * (2 * tb * LANES * 2 + 2 * 2 * n_all * LANES * 2 + tb * LANES * 2) + 7 * tq * n_all * 4
    return pl.pallas_call(
        functools.partial(_attn_kernel, n_lat=n_lat, tq=tq),
        grid=(b, h // 2, pl.cdiv(n_all, tb)),
        in_specs=[pl.BlockSpec((1, 2, tb, LANES), lambda bb, hp, i: (bb, hp, i, 0)), kv_spec, kv_spec],
        out_specs=pl.BlockSpec((1, tb, LANES), lambda bb, hp, i: (bb, i, hp)),
        out_shape=jax.ShapeDtypeStruct((b, n_all, h * MLA_V), BF16),
        compiler_params=pltpu.CompilerParams(vmem_limit_bytes=_vmem_limit(est)),
        name="mla_attention",
    )(q, k, v)


def _hg_consts(c):
    ri = lax.broadcasted_iota(jnp.int32, (c, c), 0)
    ci = lax.broadcasted_iota(jnp.int32, (c, c), 1)
    cst = {
        "tril": jnp.where(ci <= ri, 1.0, 0.0).astype(BF16),
        "triu": jnp.where(ci >= ri, 1.0, 0.0).astype(BF16),
        "lvl": {},
    }
    h = SUBLANES
    while 2 * h <= c:
        same =jnp.right_shift(ri, int(np.log2(2 * h))) == jnp.right_shift(ci, int(np.log2(2 * h)))
        r_lo = jnp.bitwise_and(ri, 2 * h - 1) < h
        c_lo = jnp.bitwise_and(ci, 2 * h - 1) < h
        fwd = jnp.where(same, jnp.where(r_lo, 0.0, jnp.where(c_lo, 1.0, 0.0)), 0.0)
        bwd = jnp.where(same, jnp.where(r_lo, jnp.where(c_lo, 0.0, 1.0), 0.0), 0.0)
        cst["lvl"][(h, False)] = fwd
        cst["lvl"][(h, True)] = bwd
        h *= 2
    r2 = lax.broadcasted_iota(jnp.int32, (LANES, LANES), 0)
    c2 = lax.broadcasted_iota(jnp.int32, (LANES, LANES), 1)
    bd = (r2 < HG_DK) == (c2 < HG_DK)
    cst["bd"] = bd
    cst["bo"] = jnp.where(bd, 1.0, 0.0).astype(BF16)
    lane = lax.broadcasted_iota(jnp.int32, (c, LANES), 1)
    cst["hm"] = [lane < HG_DK, lane >= HG_DK]
    cst["rit"] = jnp.bitwise_and(lax.broadcasted_iota(jnp.int32, (c, LANES), 0), SUBLANES - 1)
    return cst


def _hg_level_ref(b, h, rev):
    c = b.shape[0]
    off = h if rev else h - 1
    pieces = [jnp.broadcast_to(b[blk * 2 * h + off:blk * 2 * h + off + 1, :], (2 * h, LANES))
              for blk in range(c // (2 * h))]
    return pieces[0] if len(pieces) == 1 else jnp.concatenate(pieces, axis=0)


def _hg_intra(streams, *, cst):
    c = streams[0][0].shape[0]
    ns = len(streams)
    bs = [_cumsum_mm(cst["triu"] if s[5] else cst["tril"], s[3]) for s in streams]
    tots = [b[0:1, :] if s[5] else b[c - 1:c, :] for b, s in zip(bs, streams)]
    qes = [(s[0] * jnp.exp2(b)).astype(BF16) for b, s in zip(bs, streams)]
    atts = [[None, None] for _ in range(ns)]
    sign = jnp.uint32(0x80000000)
    h = SUBLANES
    while 2 * h <= c:
        ops = []
        for si, (b, (q, k, f, g, v, rev)) in enumerate(zip(bs, streams)):
            ref = _hg_level_ref(b, h, rev)
            nabs = pltpu.bitcast(pltpu.bitcast(b - ref, jnp.uint32) | sign, F32)
            e = jnp.exp2(nabs)
            qh = q * e
            ops.append(([jnp.where(cst["hm"][j], qh, 0.0).astype(BF16) for j in range(2)], (k * e).astype(BF16)))
        prods = [[_dot_nt(qj, kh) for qj in qjs] for qjs, kh in ops]
        for si in range(ns):
            msk = cst["lvl"][(h, streams[si][5])]
            for j in range(2):
                t = prods[si][j] * msk
                atts[si][j] = t if atts[si][j] is None else atts[si][j] + t
        h *= 2
    outs = []
    for si, (q, k, f, g, v, rev) in enumerate(streams):
        o = None
        for j in range(2):
            vj = jnp.where(cst["hm"][j], v, 0.0).astype(BF16)
            oj = _dot(atts[si][j].astype(BF16), vj)
            o = oj if o is None else o + oj
        outs.append(o)

    def shift(x, jj, rev):
        if jj == 0:
            return x
        x3 = x.reshape(c // SUBLANES, SUBLANES, LANES)
        return pltpu.roll(x3, (SUBLANES - jj) if rev else jj, 1).reshape(c, LANES)

    es = [None] * ns
    for dd in range(SUBLANES):
        ws = []
        for si, (q, k, f, g, v, rev) in enumerate(streams):
            if dd == 0:
                w = q * k
            else:
                fd = shift(f, dd - 1, rev)
                es[si] = fd if es[si] is None else es[si] * fd
                valid = (cst["rit"] <= SUBLANES - 1 - dd) if rev else (cst["rit"] >= dd)
                w = jnp.where(valid, q * shift(k, dd, rev) * es[si], 0.0)
            ws.append(w.astype(BF16))
        sums = [_dot(w, cst["bo"]) for w in ws]
        for si, (q, k, f, g, v, rev) in enumerate(streams):
            outs[si] = outs[si] + sums[si] * shift(v, dd, rev)
    res = []
    for si, (q, k, f, g, v, rev) in enumerate(streams):
        ke = (k * jnp.exp2(tots[si] - bs[si])).astype(BF16)
        upd = jnp.where(cst["bd"], _dot_tn(v.astype(BF16), ke), 0.0)
        res.append((outs[si], qes[si], upd, jnp.exp2(tots[si])))
    return res


def _hgrn_kernel(q_ref, fff_ref, ffb_ref, iv_ref, og_ref, lb_ref, ng_ref, r_ref,
                 o_ref, qef_ref, qeb_ref, updf_ref, updb_ref, decf_ref, decb_ref, *, n_lat):
    c = HG_CHUNK
    n_all = q_ref.shape[1]
    nc = n_all // c
    ncc = (n_all - n_lat) // c
    cst = _hg_consts(c)
    bo = cst["bo"]
    ng = ng_ref[...]
    tr = ROW_TILE
    dirs = ((False, fff_ref, lb_ref[0:1, :], qef_ref, updf_ref, decf_ref),
            (True, ffb_ref, lb_ref[1:2, :], qeb_ref, updb_ref, decb_ref))
    nb = HG_BATCH
    assert nc % nb == 0

    def intra(i, carry):
        streams, where = [], []
        for u in range(nb):
            ci = i * nb + u
            rows = pl.ds(pl.multiple_of(ci * c, c), c)
            q = _silu(q_ref[0, rows, :])
            v = iv_ref[0, rows, :]
            for rev, ff_ref, lb, qe_ref, upd_ref, dec_ref in dirs:
                sig, sigm = _sigmoid_pair(ff_ref[0, rows, :])
                f = lb + (1.0 - lb) * sig
                streams.append((q, (1.0 - lb) * sigm, f, jnp.log2(f), v, rev))
                where.append((ci, rows, qe_ref, upd_ref, dec_ref))
        res = _hg_intra(streams, cst=cst)
        for u in range(nb):
            o_ref[where[2 * u][1], :] = res[2 * u][0] + res[2 * u + 1][0]
        for (_, qe, upd, dec), (ci, rows, qe_ref, upd_ref, dec_ref) in zip(res, where):
            qe_ref[rows, :] = qe
            upd_ref[ci] = upd
            dec_ref[ci] = jnp.broadcast_to(dec, (SUBLANES, LANES))
        return carry

    lax.fori_loop(0, nc // nb, intra, 0)

    def scan(i, carry):
        st_f, st_b = carry
        jf = jnp.where(i < ncc, nc - ncc + i, i - ncc)
        rows = pl.ds(pl.multiple_of(jf * c, c), c)
        o_ref[rows, :] = o_ref[rows, :] + _dot_nt(qef_ref[rows, :], st_f.astype(BF16))
        st_f = st_f * decf_ref[jf][0:1, :] + updf_ref[jf]
        ib = nc - 1 - i
        rows = pl.ds(pl.multiple_of(ib * c, c), c)
        o_ref[rows, :] = o_ref[rows, :] + _dot_nt(qeb_ref[rows, :], st_b.astype(BF16))
        st_b = st_b * decb_ref[ib][0:1, :] + updb_ref[ib]
        return st_f, st_b

    zero = jnp.zeros((LANES, LANES), F32)
    lax.fori_loop(0, nc, scan, (zero, zero), unroll=2)

    def readout(i, carry):
        rows = pl.ds(pl.multiple_of(i * tr, tr), tr)
        o = o_ref[rows, :]
        ms = _dot((o * o).astype(BF16), bo) * (1.0 / HG_DV)
        r = o * lax.rsqrt(ms + EPS) * ng * _silu(og_ref[0, rows, :])
        r_ref[0, rows, :] = r.astype(r_ref.dtype)
        return carry

    lax.fori_loop(0, n_all // tr, readout, 0)


def _hgrn_call(p_hg, lb, ng, n_lat):
    b, n_all, _ = p_hg.shape
    nc = n_all // HG_CHUNK
    col = lambda j: pl.BlockSpec((1, n_all, LANES), lambda bb, pr: (bb, 0, 2 * j + pr))
    scratch = [pltpu.VMEM((n_all, LANES), F32), pltpu.VMEM((n_all, LANES), BF16), pltpu.VMEM((n_all, LANES), BF16),
               pltpu.VMEM((nc, LANES, LANES), F32), pltpu.VMEM((nc, LANES, LANES), F32),
               pltpu.VMEM((nc, SUBLANES, LANES), F32), pltpu.VMEM((nc, SUBLANES, LANES), F32)]
    est = 2 * 6 * n_all * LANES * 4 + 2 * n_all * LANES * 4 + 2 * nc * LANES * LANES * 4 + 8 * 1024 * 1024
    return pl.pallas_call(
        functools.partial(_hgrn_kernel, n_lat=n_lat),
        grid=(b, 2),
        in_specs=[col(0), col(1), col(2), col(3), col(4),
                  pl.BlockSpec((2, LANES), lambda bb, pr: (0, pr)),
                  pl.BlockSpec((1, LANES), lambda bb, pr: (0, pr))],
        out_specs=pl.BlockSpec((1, n_all, LANES), lambda bb, pr: (bb, 0, pr)),
        out_shape=jax.ShapeDtypeStruct((b, n_all, HG_HEADS * HG_DV), BF16),
        scratch_shapes=scratch,
        compiler_params=pltpu.CompilerParams(vmem_limit_bytes=_vmem_limit(est)),
        name="hgrn2_bidir",
    )(p_hg, p_hg, p_hg, p_hg, p_hg, lb, ng)


def _ssd_intra(chunks, *, cst, dtb, a_neg):
    n = chunks[0][0].shape[0]
    lo = cst["lane"] < SSM_STATE
    hi = jnp.logical_not(lo)
    xss = [xa[:, 0:LANES] for xa, _ in chunks]
    bms, cms = [], []
    for xa, _ in chunks:
        bc = xa[:, LANES:2 * LANES]
        bms.append(jnp.where(lo, bc, 0.0).astype(BF16))
        cms.append(jnp.where(lo, pltpu.roll(bc, SSM_STATE, 1), 0.0).astype(BF16))
    gmats = [_dot_nt(cm, bm) for cm, bm in zip(cms, bms)]
    pairs = [(ci, d) for ci in range(len(chunks)) for d in range(2)]
    dts, css = [], []
    for ci, d in pairs:
        sl = slice(d * LANES, (d + 1) * LANES)
        xdt = chunks[ci][1][:, sl] + dtb[:, sl]
        dt = jnp.maximum(xdt, 0.0) + jnp.log(1.0 + jnp.exp(-jnp.abs(xdt)))
        dts.append(dt)
        css.append(_cumsum_mm(cst["triu"] if d else cst["tril"], dt * a_neg[:, sl]))
    tots = [cs[0:1, :] if d else cs[n - 1:n, :] for cs, (ci, d) in zip(css, pairs)]
    csrs = [pltpu.roll(cs, SSM_HEADDIM, 1) for cs in css]
    csts = [cs.T for cs in css]
    xds = [xss[ci] * dt for dt, (ci, d) in zip(dts, pairs)]
    lhs, rhs = [], []
    for pi, (ci, d) in enumerate(pairs):
        for j in range(2):
            colb = jnp.where(lo, css[pi], csrs[pi]) if j == 0 else jnp.where(lo, csrs[pi], css[pi])
            rowb = jnp.broadcast_to(csts[pi][j * SSM_HEADDIM:j * SSM_HEADDIM + 1, :], (n, n))
            lmat = jnp.where(cst["tri_mask_b" if d else "tri_mask_f"], jnp.exp2(jnp.minimum(colb - rowb, 0.0)), 0.0)
            lhs.append((gmats[ci] * lmat).astype(BF16))
            rhs.append(jnp.where(lo if j == 0 else hi, xds[pi], 0.0).astype(BF16))
    prods = [_dot(a, b) for a, b in zip(lhs, rhs)]
    xins = [(xd * jnp.exp2(tot - cs)).astype(BF16) for xd, tot, cs in zip(xds, tots, css)]
    upds = [_dot_tn(bms[ci], xin) for xin, (ci, d) in zip(xins, pairs)]
    res = []
    for ci in range(len(chunks)):
        y = prods[4 * ci] + prods[4 * ci + 1] + prods[4 * ci + 2] + prods[4 * ci + 3]
        outs = [(jnp.exp2(css[2 * ci + d]), upds[2 * ci + d], jnp.exp2(tots[2 * ci + d])) for d in range(2)]
        res.append((y, cms[ci], outs))
    return res


def _ssd_kernel(z_ref, xbc_ref, dt_ref, cw_ref, cb_ref, dtb_ref, alog_ref, dsk_ref, ng_ref,
                y_ref, xa_ref, ya_ref, cm_ref, dof_ref, dob_ref, updf_ref, updb_ref, decf_ref, decb_ref, *, n_lat):
    c = SSM_CHUNK
    n_all = z_ref.shape[1]
    nc = n_all // c
    ncc = (n_all - n_lat) // c
    nl = nc - ncc
    ri = lax.broadcasted_iota(jnp.int32, (c, c), 0)
    ci_ = lax.broadcasted_iota(jnp.int32, (c, c), 1)
    cst = {
        "tril": jnp.where(ci_ <= ri, 1.0, 0.0).astype(BF16),
        "triu": jnp.where(ci_ >= ri, 1.0, 0.0).astype(BF16),
        "tri_mask_f": ci_ <= ri,
        "tri_mask_b": ci_ >= ri,
        "lane": lax.broadcasted_iota(jnp.int32, (c, LANES), 1),
    }
    dtb = dtb_ref[...]
    a_neg = -jnp.exp(alog_ref[...]) * float(np.log2(np.e))
    cb = cb_ref[...]
    halo = SUBLANES

    def conv_body(ci, carry):
        r0 = pl.multiple_of(ci * c, c)
        cur = xbc_ref[0, pl.ds(r0, c), :]
        first = jnp.logical_or(ci == 0, ci == nl)
        last = jnp.logical_or(ci == nl - 1, ci == nc - 1)
        rp = pl.multiple_of(jnp.maximum(r0 - halo, 0), halo)
        rn = pl.multiple_of(jnp.minimum(r0 + c, n_all - halo), halo)
        prev = xbc_ref[0, pl.ds(rp, halo), :] * jnp.where(first, 0.0, 1.0)
        nxt = xbc_ref[0, pl.ds(rn, halo), :] * jnp.where(last, 0.0, 1.0)
        ext = jnp.concatenate([prev, cur, nxt], axis=0)
        acc = jnp.broadcast_to(cb, (c, 2 * LANES))
        for j in range(SSM_CONV):
            s = (SSM_CONV // 2 - j) % (c + 2 * halo)
            sh = ext if s == 0 else pltpu.roll(ext, s, 0)
            acc = acc + cw_ref[j:j + 1, :] * sh[halo:halo + c, :]
        xa_ref[pl.ds(r0, c), :] = _silu(acc)
        return carry

    lax.fori_loop(0, nc, conv_body, 0)

    def intra(i, carry, nb, first):
        cis = [first + i * nb + u for u in range(nb)]
        rws = [pl.ds(pl.multiple_of(ci * c, c), c) for ci in cis]
        res = _ssd_intra([(xa_ref[rows, :], dt_ref[0, rows, :]) for rows in rws], cst=cst, dtb=dtb, a_neg=a_neg)
        for ci, rows, (y, cm, outs) in zip(cis, rws, res):
            ya_ref[rows, :] = y
            cm_ref[rows, :] = cm
            for (dec_out, upd, dec), do_ref, upd_ref, dec_ref in zip(outs, (dof_ref, dob_ref), (updf_ref, updb_ref), (decf_ref, decb_ref)):
                do_ref[rows, :] = dec_out
                upd_ref[ci] = upd
                dec_ref[ci] = jnp.broadcast_to(dec, (SUBLANES, LANES))
        return carry

    nbl = SSM_BATCH if nl % SSM_BATCH == 0 else 1
    nbc = ncc if ncc <= SSM_BATCH else 1
    lax.fori_loop(0, nl // nbl, functools.partial(intra, nb=nbl, first=0), 0)
    lax.fori_loop(0, ncc // nbc, functools.partial(intra, nb=nbc, first=nl), 0)

    def scan(i, carry):
        st_f, st_b = carry
        jf = jnp.where(i < ncc, nl + i, i - ncc)
        rows = pl.ds(pl.multiple_of(jf * c, c), c)
        ya_ref[rows, :] = ya_ref[rows, :] + _dot(cm_ref[rows, :], st_f.astype(BF16)) * dof_ref[rows, :]
        st_f = st_f * decf_ref[jf][0:1, :] + updf_ref[jf]
        ib = nc - 1 - i
        rows = pl.ds(pl.multiple_of(ib * c, c), c)
        ya_ref[rows, :] = ya_ref[rows, :] + _dot(cm_ref[rows, :], st_b.astype(BF16)) * dob_ref[rows, :]
        st_b = st_b * decb_ref[ib][0:1, :] + updb_ref[ib]
        return st_f, st_b

    zst = jnp.zeros((LANES, LANES), F32)
    lax.fori_loop(0, nc, scan, (zst, zst), unroll=2)

    dsk = dsk_ref[...]
    ng = ng_ref[...]
    tr = ROW_TILE

    def readout(i, carry):
        rows = pl.ds(pl.multiple_of(i * tr, tr), tr)
        y = ya_ref[rows, :] + dsk * xa_ref[rows, 0:LANES]
        yz = y * _silu(z_ref[0, rows, :])
        y_ref[0, rows, :] = _rms(yz, ng).astype(y_ref.dtype)
        return carry

    lax.fori_loop(0, n_all // tr, readout, 0)


def _ssd_call(p_ssm, conv_w, conv_b, dtb, alog, dsk, ng, n_lat):
    b, n_all, _ = p_ssm.shape
    assert SSM_CHUNK == LANES
    nc = n_all // SSM_CHUNK
    vec = pl.BlockSpec((1, LANES), lambda bb, g: (0, g))
    vec2 = pl.BlockSpec((1, 2 * LANES), lambda bb, g: (0, g))
    scratch = [pltpu.VMEM((n_all, 2 * LANES), F32), pltpu.VMEM((n_all, LANES), F32), pltpu.VMEM((n_all, LANES), BF16),
               pltpu.VMEM((n_all, LANES), F32), pltpu.VMEM((n_all, LANES), F32),
               pltpu.VMEM((nc, LANES, LANES), F32), pltpu.VMEM((nc, LANES, LANES), F32),
               pltpu.VMEM((nc, SUBLANES, LANES), F32), pltpu.VMEM((nc, SUBLANES, LANES), F32)]
    est = 2 * (n_all * 5 * LANES * 4 + n_all * LANES * 2) + n_all * 6 * LANES * 4 + 2 * nc * LANES * LANES * 4 + 8 * 1024 * 1024
    return pl.pallas_call(
        functools.partial(_ssd_kernel, n_lat=n_lat),
        grid=(b, SSM_GROUPS),
        in_specs=[
            pl.BlockSpec((1, n_all, LANES), lambda bb, g: (bb, 0, g)),
            pl.BlockSpec((1, n_all, 2 * LANES), lambda bb, g: (bb, 0, 1 + g)),
            pl.BlockSpec((1, n_all, 2 * LANES), lambda bb, g: (bb, 0, 3 + g)),
            pl.BlockSpec((SSM_CONV, 2 * LANES), lambda bb, g: (0, g)),
            pl.BlockSpec((1, 2 * LANES), lambda bb, g: (0, g)),
            vec2, vec2, vec, vec,
        ],
        out_specs=pl.BlockSpec((1, n_all, LANES), lambda bb, g: (bb, 0, g)),
        out_shape=jax.ShapeDtypeStruct((b, n_all, SSM_INNER), BF16),
        scratch_shapes=scratch,
        compiler_params=pltpu.CompilerParams(vmem_limit_bytes=_vmem_limit(est)),
        name="ssd_bidir",
    )(p_ssm, p_ssm, p_ssm, conv_w, conv_b, dtb, alog, dsk, ng)


def _ffn_kernel(xl_ref, xc_ref, a_ref, r_ref, s_ref, mod_ref, g2_ref, gf_ref, wa_ref, wr_ref, ws_ref, w1_ref, w2_ref,
                ol_ref, oc_ref=None, *, hid_cuts, final):
    d = xl_ref.shape[2]
    hidden = w2_ref.shape[0]
    mod = lambda j: mod_ref[0, 0, :, j * d:(j + 1) * d]

    def tile(rows, is_ctx):
        r = slice(0, rows)
        x_ref, o_ref = (xc_ref, oc_ref) if is_ctx else (xl_ref, ol_ref)
        mix = _dot(a_ref[0, r, :], wa_ref[...]) + _dot(r_ref[0, r, :], wr_ref[...]) + _dot(s_ref[0, r, :], ws_ref[...])
        x1 = x_ref[0] + mod(2) * mix
        h2 = (_rms(x1, g2_ref[...]) * (1.0 + mod(4)) + mod(3)).astype(BF16)
        y = None
        for c0, c1 in zip(hid_cuts[:-1], hid_cuts[1:]):
            ha = _dot(h2, w1_ref[:, c0:c1])
            hb = _dot(h2, w1_ref[:, hidden + c0:hidden + c1])
            act = (_silu(ha) * hb).astype(BF16)
            yc = _dot(act, w2_ref[c0:c1, :])
            y = yc if y is None else y + yc
        x2 = x1 + mod(5) * y
        o_ref[0] = _rms(x2, gf_ref[...]) if final else x2

    _lat_or_ctx(tile, xl_ref.shape[1], None if final else xc_ref.shape[1], pl.program_id(1), pl.num_programs(1))


def _ffn_call(x_lat, x_ctx, a, r, s, modsel, g2, gf, wa, wr, ws, w1, w2, final):
    b, n_lat, d = x_lat.shape
    n_ctx = x_ctx.shape[1]
    tm = TOKEN_TILE
    hidden = w2.shape[0]
    n_tiles = hidden // MXU_TILE
    assert hidden % MXU_TILE == 0
    hid_cuts = tuple(MXU_TILE * ((n_tiles * j + FFN_SPLIT - 1) // FFN_SPLIT) for j in range(FFN_SPLIT + 1))
    hid_chunk = max(b1 - b0 for b0, b1 in zip(hid_cuts[:-1], hid_cuts[1:]))
    nlt = n_lat // tm
    full = lambda arr: pl.BlockSpec(arr.shape, lambda bb, i: (0,) * arr.ndim, pipeline_mode=pl.Buffered(1))
    tok = lambda w: pl.BlockSpec((1, tm, w), lambda bb, i: (bb, i, 0))
    lat = pl.BlockSpec((1, tm, d), lambda bb, i: (bb, jnp.minimum(i, nlt - 1), 0))
    ctx = pl.BlockSpec((1, n_ctx, d), lambda bb, i: (bb, 0, 0))
    est = (wa.size + wr.size + ws.size + w1.size + w2.size) * 2 + 2 * (2 * tm * d * 4 + 2 * n_ctx * d * 4 + tm * 1024 * 2) \
        + 5 * tm * hid_chunk * 4 + 5 * tm * d * 4
    return pl.pallas_call(
        functools.partial(_ffn_kernel, hid_cuts=hid_cuts, final=final),
        grid=(b, nlt if final else nlt + 1),
        in_specs=[
            lat, ctx, tok(a.shape[2]), tok(r.shape[2]), tok(s.shape[2]),
            pl.BlockSpec((1, 1, 1, modsel.shape[3]), lambda bb, i: (bb, i // nlt, 0, 0)),
            pl.BlockSpec((1, d), lambda bb, i: (0, 0)),
            pl.BlockSpec((1, d), lambda bb, i: (0, 0)),
            full(wa), full(wr), full(ws), full(w1), full(w2),
        ],
        out_specs=[lat] if final else [lat, ctx],
        out_shape=[jax.ShapeDtypeStruct((b, n_lat, d), F32)] + ([] if final else [jax.ShapeDtypeStruct((b, n_ctx, d), F32)]),
        compiler_params=pltpu.CompilerParams(vmem_limit_bytes=_vmem_limit(est)),
        name="outproj_ffn",
    )(x_lat, x_ctx, a, r, s, modsel, g2, gf, wa, wr, ws, w1, w2)


def _win_perm():
    perm = np.full((P_TOTAL,), -1, np.int64)
    perm[P_QLAT:P_QLAT + MLA_Q_RANK + MLA_KV_RANK] = np.arange(MLA_Q_RANK + MLA_KV_RANK)
    perm[P_KR + MLA_NOPE:P_KR + MLA_NOPE + ROPE_LANES] = MLA_Q_RANK + MLA_KV_RANK + ROPE_PERM
    perm[P_HG:P_HG + HG_COLS] = MLA_COLS + np.arange(HG_COLS)
    o2 = MLA_COLS + HG_COLS
    perm[P_SSM:P_SSM + SSM_INNER] = o2 + np.arange(SSM_INNER)
    xo = o2 + SSM_INNER
    bo = xo + SSM_INNER
    co = bo + SSM_GROUPS * SSM_STATE
    dto = o2 + SSM_INNER + SSM_XBC
    for g in range(SSM_GROUPS):
        base = P_SSM + SSM_INNER + g * 2 * LANES
        perm[base:base + LANES] = xo + g * LANES + np.arange(LANES)
        perm[base + LANES:base + LANES + SSM_STATE] = bo + g * SSM_STATE + np.arange(SSM_STATE)
        perm[base + LANES + SSM_STATE:base + 2 * LANES] = co + g * SSM_STATE + np.arange(SSM_STATE)
        dbase = P_SSM + SSM_INNER + 2 * 2 * LANES + g * 2 * LANES
        for d in range(2):
            for j in range(2):
                lo = dbase + d * LANES + j * SSM_HEADDIM
                perm[lo:lo + SSM_HEADDIM] = dto + d * SSM_HEADS + 2 * g + j
    return perm


def _conv_perm():
    perm = np.zeros((SSM_XBC,), np.int64)
    for g in range(SSM_GROUPS):
        base = g * 2 * LANES
        perm[base:base + LANES] = g * LANES + np.arange(LANES)
        perm[base + LANES:base + LANES + SSM_STATE] = SSM_INNER + g * SSM_STATE + np.arange(SSM_STATE)
        perm[base + LANES + SSM_STATE:base + 2 * LANES] = SSM_INNER + SSM_GROUPS * SSM_STATE + g * SSM_STATE + np.arange(SSM_STATE)
    return perm


def _gather_cols(w, perm):
    idx = jnp.asarray(np.maximum(perm, 0), jnp.int32)
    out = jnp.take(w, idx, axis=-1)
    return jnp.where(jnp.asarray(perm >= 0), out, 0.0)


def _head_vec(v):
    depth = v.shape[0]
    v5 = v.reshape(depth, 2, SSM_GROUPS, 2, 1)
    v5 = jnp.broadcast_to(v5, (depth, 2, SSM_GROUPS, 2, SSM_HEADDIM))
    return jnp.transpose(v5, (0, 2, 1, 3, 4)).reshape(depth, 1, SSM_GROUPS * 2 * LANES)


def _rope_tables(n_ctx, n_lat, scale):
    rows = n_lat // GRID_W
    row = jnp.repeat(jnp.arange(rows, dtype=F32), GRID_W)
    col = jnp.tile(jnp.arange(GRID_W, dtype=F32), rows)
    n_freq = MLA_ROPE // 4
    inv = ROPE_BASE ** (-jnp.arange(n_freq, dtype=F32) / n_freq)
    ang = jnp.stack([row[:, None] * inv, col[:, None] * inv], axis=1)
    cos, sin = jnp.cos(ang), jnp.sin(ang)
    zf = jnp.zeros((n_lat, ROPE_F), F32)
    c_r = jnp.concatenate([cos[:, 0], cos[:, 0], zf, cos[:, 1], cos[:, 1], zf], axis=-1)
    s_r = jnp.concatenate([-sin[:, 0], sin[:, 0], zf, -sin[:, 1], sin[:, 1], zf], axis=-1)
    pad = jnp.zeros((n_lat, LANES - MLA_NOPE - ROPE_LANES), F32)
    c_lat = jnp.concatenate([jnp.ones((n_lat, MLA_NOPE), F32), c_r, pad], axis=-1)
    s_lat = jnp.concatenate([jnp.zeros((n_lat, MLA_NOPE), F32), s_r, pad], axis=-1)
    keep = jnp.asarray(np.concatenate([np.ones(MLA_NOPE), np.tile(np.repeat([1.0, 1.0, 0.0], ROPE_F), 2),
                                       np.zeros(LANES - MLA_NOPE - ROPE_LANES)]), F32)[None, :]
    c_all = jnp.concatenate([c_lat, jnp.broadcast_to(keep, (n_ctx, LANES))], axis=0)
    s_all = jnp.concatenate([s_lat, jnp.zeros((n_ctx, LANES), F32)], axis=0)
    return c_all * scale, s_all * scale, c_all, s_all


def kernel(x, c, ctx, c_ctx, w_ada, b_ada, norm1_g, norm2_g, w_in, mla_qa_g, mla_wqb, mla_kva_g, mla_wkvb, hg_lb_logits, hg_norm_g, ssm_conv_w, ssm_conv_b, ssm_dt_bias, ssm_a_log, ssm_d, ssm_norm_g, w_out, w_ffn_in, w_ffn_out, final_g):
    bsz, n_lat, d = x.shape
    n_ctx = ctx.shape[1]
    depth = w_ada.shape[0]
    assert n_lat % TOKEN_TILE == 0 and n_lat % GRID_W == 0 and n_ctx <= TOKEN_TILE
    assert n_ctx % (HG_BATCH * HG_CHUNK) == 0 and n_ctx % SSM_CHUNK == 0 and n_ctx % ROW_TILE == 0

    rows = -(-(bsz + 1) // SUBLANES) * SUBLANES
    cc = jnp.zeros((rows, d), F32).at[:bsz].set(c).at[bsz].set(c_ctx)
    mods = _mod_call(cc, w_ada.astype(BF16), b_ada.reshape(depth, 1, 6 * d))
    modsel = jnp.stack([mods[:, :bsz], jnp.broadcast_to(mods[:, bsz:bsz + 1], (depth, bsz, 6 * d))], axis=2)[:, :, :, None, :]

    w_in_p = _gather_cols(w_in, _win_perm()).astype(BF16)
    head_cols = np.full((LANES,), -1, np.int64)
    head_cols[:MLA_NOPE] = np.arange(MLA_NOPE)
    head_cols[MLA_NOPE:MLA_NOPE + ROPE_LANES] = MLA_NOPE + ROPE_PERM
    wqb_perm = np.concatenate([np.where(head_cols >= 0, hh * MLA_QK + head_cols, -1) for hh in range(MLA_HEADS)])
    wqb_p = _gather_cols(mla_wqb, wqb_perm).astype(BF16)
    wkvb_b = mla_wkvb.astype(BF16)
    cperm = _conv_perm()
    conv_w_p = jnp.take(ssm_conv_w, jnp.asarray(cperm, jnp.int32), axis=-1)
    conv_b_p = jnp.take(ssm_conv_b, jnp.asarray(cperm, jnp.int32), axis=-1).reshape(depth, 1, SSM_XBC)
    dtb_p = _head_vec(ssm_dt_bias)
    alog_p = _head_vec(ssm_a_log)
    dsk_p = jnp.repeat(ssm_d, SSM_HEADDIM, axis=-1).reshape(depth, 1, SSM_INNER)
    lb_soft = jax.nn.softmax(hg_lb_logits.astype(F32), axis=0)
    lb_all = jnp.cumsum(lb_soft, axis=0) - lb_soft[0]
    w_out_b = w_out.astype(BF16)
    a_w = MLA_HEADS * MLA_V
    r_w = HG_HEADS * HG_DV
    w1_b = w_ffn_in.astype(BF16)
    w2_b = w_ffn_out.astype(BF16)
    tabs = _rope_tables(n_ctx, n_lat, MLA_QK ** -0.5 * float(np.log2(np.e)))

    streams = [x, ctx]
    for l in range(depth):
        q, k, v, p_hg, p_ssm = _inproj_call(
            *streams, modsel[l], norm1_g[l].reshape(1, d), w_in_p[l], mla_qa_g[l].reshape(1, -1),
            mla_kva_g[l].reshape(1, -1), wqb_p[l], wkvb_b[l], tabs)
        a = _attn_call(q, k, v, n_lat)
        r = _hgrn_call(p_hg, lb_all[l], hg_norm_g[l].reshape(1, -1), n_lat)
        s = _ssd_call(p_ssm, conv_w_p[l], conv_b_p[l], dtb_p[l], alog_p[l], dsk_p[l],
                      ssm_norm_g[l].reshape(1, -1), n_lat)
        new = _ffn_call(*streams, a, r, s, modsel[l], norm2_g[l].reshape(1, d), final_g.reshape(1, d),
                        w_out_b[l, :a_w], w_out_b[l, a_w:a_w + r_w], w_out_b[l, a_w + r_w:], w1_b[l], w2_b[l],
                        final=(l == depth - 1))
        streams = [new[0], new[1] if len(new) > 1 else streams[1]]
    return streams[0]
```

```python
import functools

import numpy as np
import jax
import jax.numpy as jnp
from jax import lax
from jax.experimental import pallas as pl
from jax.experimental.pallas import tpu as pltpu

F32 = jnp.float32
BF16 = jnp.bfloat16
EPS = 1e-6

LANES = 128
SUBLANES = 8
MXU_TILE = 256
VMEM_BYTES = 64 * 1024 * 1024

GRID_W = 64
MLA_HEADS = 8
MLA_Q_RANK = 384
MLA_KV_RANK = 256
MLA_NOPE = 64
MLA_ROPE = 32
MLA_V = 64
MLA_QK = MLA_NOPE + MLA_ROPE
ROPE_BASE = 10000.0
HG_HEADS = 4
HG_DK = 64
HG_DV = 64
HG_W = HG_HEADS * HG_DK
SSM_HEADS = 4
SSM_HEADDIM = 64
SSM_GROUPS = 2
SSM_STATE = 64
SSM_CONV = 5
SSM_INNER = SSM_HEADS * SSM_HEADDIM
SSM_XBC = SSM_INNER + 2 * SSM_GROUPS * SSM_STATE

ROPE_F = MLA_ROPE // 4
ROPE_PERM = np.concatenate([np.concatenate([a * 2 * ROPE_F + np.arange(2 * ROPE_F), a * 2 * ROPE_F + np.arange(ROPE_F)])
                            for a in range(2)])
ROPE_LANES = ROPE_PERM.size

MLA_COLS = MLA_Q_RANK + MLA_KV_RANK + MLA_ROPE
HG_COLS = 3 * HG_W + 2 * HG_HEADS * HG_DV
SSM_COLS = SSM_INNER + SSM_XBC + 2 * SSM_HEADS

P_QLAT = 0
P_KVLAT = MLA_Q_RANK
P_KR = P_KVLAT + MLA_KV_RANK
P_HG = P_KR + LANES
P_SSM = P_HG + HG_COLS
SSM_BLOCK = SSM_INNER + 2 * 2 * LANES + 2 * 2 * LANES
P_TOTAL = P_SSM + SSM_BLOCK

TOKEN_TILE = 512
ROW_TILE = 256
ATTN_TILES = 8
HG_CHUNK = 128
HG_BATCH = 2
SSM_CHUNK = 128
SSM_BATCH = 8
FFN_SPLIT = 2

NT_DIMS = (((1,), (1,)), ((), ()))
TN_DIMS = (((0,), (0,)), ((), ()))


def _vmem_limit(nbytes):
    return int(min(VMEM_BYTES - 8 * 1024 * 1024, max(nbytes, 16 * 1024 * 1024)))


def _sigmoid_pair(x):
    e = jnp.exp(-jnp.abs(x))
    d = 1.0 / (1.0 + e)
    ed = e * d
    pos = x >= 0
    return jnp.where(pos, d, ed), jnp.where(pos, ed, d)


def _silu(x):
    return x / (1.0 + jnp.exp(-x))


def _rms(x, g):
    ms = jnp.mean(x * x, axis=-1, keepdims=True)
    return x * lax.rsqrt(ms + EPS) * g


def _dot(a, b):
    return jnp.dot(a, b, preferred_element_type=F32)


def _dot_nt(a, b):
    return lax.dot_general(a, b, NT_DIMS, preferred_element_type=F32)


def _dot_tn(a, b):
    return lax.dot_general(a, b, TN_DIMS, preferred_element_type=F32)


def _lat_or_ctx(tile, tm, ctx_rows, i, n):
    if ctx_rows is None:
        tile(tm, False)
        return
    last = i == n - 1
    pl.when(last)(lambda: tile(ctx_rows, True))
    pl.when(jnp.logical_not(last))(lambda: tile(tm, False))


def _split3(x):
    x1 = x.astype(BF16)
    r1 = x - x1.astype(F32)
    x2 = r1.astype(BF16)
    x3 = (r1 - x2.astype(F32)).astype(BF16)
    return x1, x2, x3


def _cumsum_mm(tri, x):
    x1, x2, x3 = _split3(x)
    return _dot(tri, x1) + _dot(tri, x2) + _dot(tri, x3)


def _mod_kernel(c_ref, w_ref, b_ref, o_ref):
    s = _silu(c_ref[...]).astype(BF16)
    o_ref[0] = _dot(s, w_ref[0]) + b_ref[0]


def _mod_call(cc, w_ada, b_ada):
    depth, d, six_d = w_ada.shape
    rows = cc.shape[0]
    tn = 1536
    return pl.pallas_call(
        _mod_kernel,
        grid=(depth, six_d // tn),
        in_specs=[
            pl.BlockSpec((rows, d), lambda l, j: (0, 0)),
            pl.BlockSpec((1, d, tn), lambda l, j: (l, 0, j)),
            pl.BlockSpec((1, 1, tn), lambda l, j: (l, 0, j)),
        ],
        out_specs=pl.BlockSpec((1, rows, tn), lambda l, j: (l, 0, j)),
        out_shape=jax.ShapeDtypeStruct((depth, rows, six_d), F32),
        name="adaln_mod",
    )(cc, w_ada, b_ada)


def _rope(xh, c, s):
    return xh * c + pltpu.roll(xh, LANES - ROPE_F, 1) * s


def _inproj_kernel(xl_ref, xc_ref, mod_ref, g1_ref, win_ref, qag_ref, kvag_ref, wqb_ref, wkvb_ref,
                   cq_ref, sq_ref, ck_ref, sk_ref,
                   q_ref, k_ref, v_ref, hg_ref, ssm_ref):
    d = xl_ref.shape[2]

    def tile(rows, is_ctx):
        r = slice(0, rows)
        x = xc_ref[0] if is_ctx else xl_ref[0]
        sh = mod_ref[0, 0, :, 0:d]
        sc = mod_ref[0, 0, :, d:2 * d]
        h = _rms(x, g1_ref[...]) * (1.0 + sc) + sh
        p = _dot(h.astype(BF16), win_ref[...])
        hg_ref[0, r, :] = p[:, P_HG:P_SSM]
        ssm_ref[0, r, :] = p[:, P_SSM:P_TOTAL]
        q_lat = _rms(p[:, P_QLAT:P_KVLAT], qag_ref[...])
        kv_lat = _rms(p[:, P_KVLAT:P_KR], kvag_ref[...])
        q = _dot(q_lat.astype(BF16), wqb_ref[...])
        kv = _dot(kv_lat.astype(BF16), wkvb_ref[...])
        lane = lax.broadcasted_iota(jnp.int32, (rows, LANES), 1)
        cq, sq, ck, sk = cq_ref[r, :], sq_ref[r, :], ck_ref[r, :], sk_ref[r, :]
        kr = _rope(p[:, P_KR:P_HG], ck, sk)
        nope = lane < MLA_NOPE
        for hh in range(MLA_HEADS):
            sl = slice(hh * LANES, (hh + 1) * LANES)
            q_ref[0, hh, r, :] = _rope(q[:, sl], cq, sq).astype(q_ref.dtype)
            kvh = kv[:, sl]
            k_ref[0, hh, r, :] = jnp.where(nope, kvh, kr).astype(k_ref.dtype)
            v_ref[0, hh, r, :] = jnp.where(nope, 1.0, kvh).astype(v_ref.dtype)

    _lat_or_ctx(tile, xl_ref.shape[1], xc_ref.shape[1], pl.program_id(1), pl.num_programs(1))


def _inproj_call(x_lat, x_ctx, modsel, g1, w_in, qa_g, kva_g, wqb, wkvb, tabs):
    b, n_lat, d = x_lat.shape
    n_ctx = x_ctx.shape[1]
    n_all = n_lat + n_ctx
    tm = TOKEN_TILE
    nlt = n_lat // tm
    nt = nlt + 1
    full = lambda shape: pl.BlockSpec(shape, lambda bb, i: (0,) * len(shape), pipeline_mode=pl.Buffered(1))
    tab = pl.BlockSpec((tm, LANES), lambda bb, i: (i, 0))
    head_out = pl.BlockSpec((1, MLA_HEADS, tm, LANES), lambda bb, i: (bb, 0, i, 0))
    est = (w_in.size + wqb.size + wkvb.size) * 2 + 2 * (tm * d * 4 + 3 * MLA_HEADS * tm * LANES * 2
                                                        + tm * (HG_COLS + SSM_BLOCK) * 4) + 4 * tm * P_TOTAL * 4
    return pl.pallas_call(
        _inproj_kernel,
        grid=(b, nt),
        in_specs=[
            pl.BlockSpec((1, tm, d), lambda bb, i: (bb, jnp.minimum(i, nlt - 1), 0)),
            pl.BlockSpec((1, n_ctx, d), lambda bb, i: (bb, 0, 0)),
            pl.BlockSpec((1, 1, 1, modsel.shape[3]), lambda bb, i: (bb, i // nlt, 0, 0)),
            full((1, d)),
            full(w_in.shape),
            full((1, MLA_Q_RANK)),
            full((1, MLA_KV_RANK)),
            full(wqb.shape),
            full(wkvb.shape),
            tab, tab, tab, tab,
        ],
        out_specs=[
            head_out, head_out, head_out,
            pl.BlockSpec((1, tm, HG_COLS), lambda bb, i: (bb, i, 0)),
            pl.BlockSpec((1, tm, SSM_BLOCK), lambda bb, i: (bb, i, 0)),
        ],
        out_shape=[
            jax.ShapeDtypeStruct((b, MLA_HEADS, n_all, LANES), BF16),
            jax.ShapeDtypeStruct((b, MLA_HEADS, n_all, LANES), BF16),
            jax.ShapeDtypeStruct((b, MLA_HEADS, n_all, LANES), BF16),
            jax.ShapeDtypeStruct((b, n_all, HG_COLS), F32),
            jax.ShapeDtypeStruct((b, n_all, SSM_BLOCK), F32),
        ],
        compiler_params=pltpu.CompilerParams(vmem_limit_bytes=_vmem_limit(est)),
        name="inproj_mla",
    )(x_lat, x_ctx, modsel, g1, w_in, qa_g, kva_g, wqb, wkvb, *tabs)


def _attn_kernel(q_ref, k_ref, v_ref, o_ref, *, n_lat, tq):
    n_all = k_ref.shape[2]
    n_sub = q_ref.shape[2] // tq

    def run(r0, rows, k0, k1):
        r = slice(r0, r0 + rows)
        ss = [_dot_nt(q_ref[0, j, r, :], k_ref[0, j, k0:k1, :]) for j in range(2)]
        ps = [jnp.exp2(s - jnp.max(s, axis=-1, keepdims=True)).astype(BF16) for s in ss]
        outs = []
        for j in range(2):
            o = _dot(ps[j], v_ref[0, j, k0:k1, :])
            outs.append(o / pltpu.roll(o, MLA_V, 1))
        lane = lax.broadcasted_iota(jnp.int32, (rows, LANES), 1)
        o_ref[0, r, :] = jnp.where(lane < MLA_V, pltpu.roll(outs[0], MLA_V, 1), outs[1]).astype(o_ref.dtype)

    is_ctx = pl.program_id(2) >= n_lat // (n_sub * tq)

    @pl.when(is_ctx)
    def _():
        run(0, n_all - n_lat, n_lat, n_all)

    @pl.when(jnp.logical_not(is_ctx))
    def _():
        for sub in range(n_sub):
            run(sub * tq, tq, 0, n_all)


def _attn_call(q, k, v, n_lat):
    b, h, n_all, _ = q.shape
    tq = TOKEN_TILE
    tb = ATTN_TILES * tq
    assert n_lat % tb == 0 and n_all - n_lat <= tb
    kv_spec = pl.BlockSpec((1, 2, n_all, LANES), lambda bb, hp, i: (bb, hp, 0, 0))
    est = 2 * (2 * tb * LANES * 2 + 2 * 2 * n_all * LANES * 2 + tb * LANES * 2) + 7 * tq * n_all * 4
    return pl.pallas_call(
        functools.partial(_attn_kernel, n_lat=n_lat, tq=tq),
        grid=(b, h // 2, pl.cdiv(n_all, tb)),
        in_specs=[pl.BlockSpec((1, 2, tb, LANES), lambda bb, hp, i: (bb, hp, i, 0)), kv_spec, kv_spec],
        out_specs=pl.BlockSpec((1, tb, LANES), lambda bb, hp, i: (bb, i, hp)),
        out_shape=jax.ShapeDtypeStruct((b, n_all, h * MLA_V), BF16),
        compiler_params=pltpu.CompilerParams(vmem_limit_bytes=_vmem_limit(est)),
        name="mla_attention",
    )(q, k, v)


def _hg_consts(c):
    ri = lax.broadcasted_iota(jnp.int32, (c, c), 0)
    ci = lax.broadcasted_iota(jnp.int32, (c, c), 1)
    cst = {
        "tril": jnp.where(ci <= ri, 1.0, 0.0).astype(BF16),
        "triu": jnp.where(ci >= ri, 1.0, 0.0).astype(BF16),
        "lvl": {},
    }
    h = SUBLANES
    while 2 * h <= c:
        same =jnp.right_shift(ri, int(np.log2(2 * h))) == jnp.right_shift(ci, int(np.log2(2 * h)))
        r_lo = jnp.bitwise_and(ri, 2 * h - 1) < h
        c_lo = jnp.bitwise_and(ci, 2 * h - 1) < h
        fwd = jnp.where(same, jnp.where(r_lo, 0.0, jnp.where(c_lo, 1.0, 0.0)), 0.0)
        bwd = jnp.where(same, jnp.where(r_lo, jnp.where(c_lo, 0.0, 1.0), 0.0), 0.0)
        cst["lvl"][(h, False)] = fwd
        cst["lvl"][(h, True)] = bwd
        h *= 2
    r2 = lax.broadcasted_iota(jnp.int32, (LANES, LANES), 0)
    c2 = lax.broadcasted_iota(jnp.int32, (LANES, LANES), 1)
    bd = (r2 < HG_DK) == (c2 < HG_DK)
    cst["bd"] = bd
    cst["bo"] = jnp.where(bd, 1.0, 0.0).astype(BF16)
    lane = lax.broadcasted_iota(jnp.int32, (c, LANES), 1)
    cst["hm"] = [lane < HG_DK, lane >= HG_DK]
    cst["rit"] = jnp.bitwise_and(lax.broadcasted_iota(jnp.int32, (c, LANES), 0), SUBLANES - 1)
    return cst


def _hg_level_ref(b, h, rev):
    c = b.shape[0]
    off = h if rev else h - 1
    pieces = [jnp.broadcast_to(b[blk * 2 * h + off:blk * 2 * h + off + 1, :], (2 * h, LANES))
              for blk in range(c // (2 * h))]
    return pieces[0] if len(pieces) == 1 else jnp.concatenate(pieces, axis=0)


def _hg_intra(streams, *, cst):
    c = streams[0][0].shape[0]
    ns = len(streams)
    bs = [_cumsum_mm(cst["triu"] if s[5] else cst["tril"], s[3]) for s in streams]
    tots = [b[0:1, :] if s[5] else b[c - 1:c, :] for b, s in zip(bs, streams)]
    qes = [(s[0] * jnp.exp2(b)).astype(BF16) for b, s in zip(bs, streams)]
    atts = [[None, None] for _ in range(ns)]
    sign = jnp.uint32(0x80000000)
    h = SUBLANES
    while 2 * h <= c:
        ops = []
        for si, (b, (q, k, f, g, v, rev)) in enumerate(zip(bs, streams)):
            ref = _hg_level_ref(b, h, rev)
            nabs = pltpu.bitcast(pltpu.bitcast(b - ref, jnp.uint32) | sign, F32)
            e = jnp.exp2(nabs)
            qh = q * e
            ops.append(([jnp.where(cst["hm"][j], qh, 0.0).astype(BF16) for j in range(2)], (k * e).astype(BF16)))
        prods = [[_dot_nt(qj, kh) for qj in qjs] for qjs, kh in ops]
        for si in range(ns):
            msk = cst["lvl"][(h, streams[si][5])]
            for j in range(2):
                t = prods[si][j] * msk
                atts[si][j] = t if atts[si][j] is None else atts[si][j] + t
        h *= 2
    outs = []
    for si, (q, k, f, g, v, rev) in enumerate(streams):
        o = None
        for j in range(2):
            vj = jnp.where(cst["hm"][j], v, 0.0).astype(BF16)
            oj = _dot(atts[si][j].astype(BF16), vj)
            o = oj if o is None else o + oj
        outs.append(o)

    def shift(x, jj, rev):
        if jj == 0:
            return x
        x3 = x.reshape(c // SUBLANES, SUBLANES, LANES)
        return pltpu.roll(x3, (SUBLANES - jj) if rev else jj, 1).reshape(c, LANES)

    es = [None] * ns
    for dd in range(SUBLANES):
        ws = []
        for si, (q, k, f, g, v, rev) in enumerate(streams):
            if dd == 0:
                w = q * k
            else:
                fd = shift(f, dd - 1, rev)
                es[si] = fd if es[si] is None else es[si] * fd
                valid = (cst["rit"] <= SUBLANES - 1 - dd) if rev else (cst["rit"] >= dd)
                w = jnp.where(valid, q * shift(k, dd, rev) * es[si], 0.0)
            ws.append(w.astype(BF16))
        sums = [_dot(w, cst["bo"]) for w in ws]
        for si, (q, k, f, g, v, rev) in enumerate(streams):
            outs[si] = outs[si] + sums[si] * shift(v, dd, rev)
    res = []
    for si, (q, k, f, g, v, rev) in enumerate(streams):
        ke = (k * jnp.exp2(tots[si] - bs[si])).astype(BF16)
        upd = jnp.where(cst["bd"], _dot_tn(v.astype(BF16), ke), 0.0)
        res.append((outs[si], qes[si], upd, jnp.exp2(tots[si])))
    return res


def _hgrn_kernel(q_ref, fff_ref, ffb_ref, iv_ref, og_ref, lb_ref, ng_ref, r_ref,
                 o_ref, qef_ref, qeb_ref, updf_ref, updb_ref, decf_ref, decb_ref, *, n_lat):
    c = HG_CHUNK
    n_all = q_ref.shape[1]
    nc = n_all // c
    ncc = (n_all - n_lat) // c
    cst = _hg_consts(c)
    bo = cst["bo"]
    ng = ng_ref[...]
    tr = ROW_TILE
    dirs = ((False, fff_ref, lb_ref[0:1, :], qef_ref, updf_ref, decf_ref),
            (True, ffb_ref, lb_ref[1:2, :], qeb_ref, updb_ref, decb_ref))
    nb = HG_BATCH
    assert nc % nb == 0

    def intra(i, carry):
        streams, where = [], []
        for u in range(nb):
            ci = i * nb + u
            rows = pl.ds(pl.multiple_of(ci * c, c), c)
            q = _silu(q_ref[0, rows, :])
            v = iv_ref[0, rows, :]
            for rev, ff_ref, lb, qe_ref, upd_ref, dec_ref in dirs:
                sig, sigm = _sigmoid_pair(ff_ref[0, rows, :])
                f = lb + (1.0 - lb) * sig
                streams.append((q, (1.0 - lb) * sigm, f, jnp.log2(f), v, rev))
                where.append((ci, rows, qe_ref, upd_ref, dec_ref))
        res = _hg_intra(streams, cst=cst)
        for u in range(nb):
            o_ref[where[2 * u][1], :] = res[2 * u][0] + res[2 * u + 1][0]
        for (_, qe, upd, dec), (ci, rows, qe_ref, upd_ref, dec_ref) in zip(res, where):
            qe_ref[rows, :] = qe
            upd_ref[ci] = upd
            dec_ref[ci] = jnp.broadcast_to(dec, (SUBLANES, LANES))
        return carry

    lax.fori_loop(0, nc // nb, intra, 0)

    def scan(i, carry):
        st_f, st_b = carry
        jf = jnp.where(i < ncc, nc - ncc + i, i - ncc)
        rows = pl.ds(pl.multiple_of(jf * c, c), c)
        o_ref[rows, :] = o_ref[rows, :] + _dot_nt(qef_ref[rows, :], st_f.astype(BF16))
        st_f = st_f * decf_ref[jf][0:1, :] + updf_ref[jf]
        ib = nc - 1 - i
        rows = pl.ds(pl.multiple_of(ib * c, c), c)
        o_ref[rows, :] = o_ref[rows, :] + _dot_nt(qeb_ref[rows, :], st_b.astype(BF16))
        st_b = st_b * decb_ref[ib][0:1, :] + updb_ref[ib]
        return st_f, st_b

    zero = jnp.zeros((LANES, LANES), F32)
    lax.fori_loop(0, nc, scan, (zero, zero), unroll=2)

    def readout(i, carry):
        rows = pl.ds(pl.multiple_of(i * tr, tr), tr)
        o = o_ref[rows, :]
        ms = _dot((o * o).astype(BF16), bo) * (1.0 / HG_DV)
        r = o * lax.rsqrt(ms + EPS) * ng * _silu(og_ref[0, rows, :])
        r_ref[0, rows, :] = r.astype(r_ref.dtype)
        return carry

    lax.fori_loop(0, n_all // tr, readout, 0)


def _hgrn_call(p_hg, lb, ng, n_lat):
    b, n_all, _ = p_hg.shape
    nc = n_all // HG_CHUNK
    col = lambda j: pl.BlockSpec((1, n_all, LANES), lambda bb, pr: (bb, 0, 2 * j + pr))
    scratch = [pltpu.VMEM((n_all, LANES), F32), pltpu.VMEM((n_all, LANES), BF16), pltpu.VMEM((n_all, LANES), BF16),
               pltpu.VMEM((nc, LANES, LANES), F32), pltpu.VMEM((nc, LANES, LANES), F32),
               pltpu.VMEM((nc, SUBLANES, LANES), F32), pltpu.VMEM((nc, SUBLANES, LANES), F32)]
    est = 2 * 6 * n_all * LANES * 4 + 2 * n_all * LANES * 4 + 2 * nc * LANES * LANES * 4 + 8 * 1024 * 1024
    return pl.pallas_call(
        functools.partial(_hgrn_kernel, n_lat=n_lat),
        grid=(b, 2),
        in_specs=[col(0), col(1), col(2), col(3), col(4),
                  pl.BlockSpec((2, LANES), lambda bb, pr: (0, pr)),
                  pl.BlockSpec((1, LANES), lambda bb, pr: (0, pr))],
        out_specs=pl.BlockSpec((1, n_all, LANES), lambda bb, pr: (bb, 0, pr)),
        out_shape=jax.ShapeDtypeStruct((b, n_all, HG_HEADS * HG_DV), BF16),
        scratch_shapes=scratch,
        compiler_params=pltpu.CompilerParams(vmem_limit_bytes=_vmem_limit(est)),
        name="hgrn2_bidir",
    )(p_hg, p_hg, p_hg, p_hg, p_hg, lb, ng)


def _ssd_intra(chunks, *, cst, dtb, a_neg):
    n = chunks[0][0].shape[0]
    lo = cst["lane"] < SSM_STATE
    hi = jnp.logical_not(lo)
    xss = [xa[:, 0:LANES] for xa, _ in chunks]
    bms, cms = [], []
    for xa, _ in chunks:
        bc = xa[:, LANES:2 * LANES]
        bms.append(jnp.where(lo, bc, 0.0).astype(BF16))
        cms.append(jnp.where(lo, pltpu.roll(bc, SSM_STATE, 1), 0.0).astype(BF16))
    gmats = [_dot_nt(cm, bm) for cm, bm in zip(cms, bms)]
    pairs = [(ci, d) for ci in range(len(chunks)) for d in range(2)]
    dts, css = [], []
    for ci, d in pairs:
        sl = slice(d * LANES, (d + 1) * LANES)
        xdt = chunks[ci][1][:, sl] + dtb[:, sl]
        dt = jnp.maximum(xdt, 0.0) + jnp.log(1.0 + jnp.exp(-jnp.abs(xdt)))
        dts.append(dt)
        css.append(_cumsum_mm(cst["triu"] if d else cst["tril"], dt * a_neg[:, sl]))
    tots = [cs[0:1, :] if d else cs[n - 1:n, :] for cs, (ci, d) in zip(css, pairs)]
    csrs = [pltpu.roll(cs, SSM_HEADDIM, 1) for cs in css]
    csts = [cs.T for cs in css]
    xds = [xss[ci] * dt for dt, (ci, d) in zip(dts, pairs)]
    lhs, rhs = [], []
    for pi, (ci, d) in enumerate(pairs):
        for j in range(2):
            colb = jnp.where(lo, css[pi], csrs[pi]) if j == 0 else jnp.where(lo, csrs[pi], css[pi])
            rowb = jnp.broadcast_to(csts[pi][j * SSM_HEADDIM:j * SSM_HEADDIM + 1, :], (n, n))
            lmat = jnp.where(cst["tri_mask_b" if d else "tri_mask_f"], jnp.exp2(jnp.minimum(colb - rowb, 0.0)), 0.0)
            lhs.append((gmats[ci] * lmat).astype(BF16))
            rhs.append(jnp.where(lo if j == 0 else hi, xds[pi], 0.0).astype(BF16))
    prods = [_dot(a, b) for a, b in zip(lhs, rhs)]
    xins = [(xd * jnp.exp2(tot - cs)).astype(BF16) for xd, tot, cs in zip(xds, tots, css)]
    upds = [_dot_tn(bms[ci], xin) for xin, (ci, d) in zip(xins, pairs)]
    res = []
    for ci in range(len(chunks)):
        y = prods[4 * ci] + prods[4 * ci + 1] + prods[4 * ci + 2] + prods[4 * ci + 3]
        outs = [(jnp.exp2(css[2 * ci + d]), upds[2 * ci + d], jnp.exp2(tots[2 * ci + d])) for d in range(2)]
        res.append((y, cms[ci], outs))
    return res


def _ssd_kernel(z_ref, xbc_ref, dt_ref, cw_ref, cb_ref, dtb_ref, alog_ref, dsk_ref, ng_ref,
                y_ref, xa_ref, ya_ref, cm_ref, dof_ref, dob_ref, updf_ref, updb_ref, decf_ref, decb_ref, *, n_lat):
    c = SSM_CHUNK
    n_all = z_ref.shape[1]
    nc = n_all // c
    ncc = (n_all - n_lat) // c
    nl = nc - ncc
    ri = lax.broadcasted_iota(jnp.int32, (c, c), 0)
    ci_ = lax.broadcasted_iota(jnp.int32, (c, c), 1)
    cst = {
        "tril": jnp.where(ci_ <= ri, 1.0, 0.0).astype(BF16),
        "triu": jnp.where(ci_ >= ri, 1.0, 0.0).astype(BF16),
        "tri_mask_f": ci_ <= ri,
        "tri_mask_b": ci_ >= ri,
        "lane": lax.broadcasted_iota(jnp.int32, (c, LANES), 1),
    }
    dtb = dtb_ref[...]
    a_neg = -jnp.exp(alog_ref[...]) * float(np.log2(np.e))
    cb = cb_ref[...]
    halo = SUBLANES

    def conv_body(ci, carry):
        r0 = pl.multiple_of(ci * c, c)
        cur = xbc_ref[0, pl.ds(r0, c), :]
        first = jnp.logical_or(ci == 0, ci == nl)
        last = jnp.logical_or(ci == nl - 1, ci == nc - 1)
        rp = pl.multiple_of(jnp.maximum(r0 - halo, 0), halo)
        rn = pl.multiple_of(jnp.minimum(r0 + c, n_all - halo), halo)
        prev = xbc_ref[0, pl.ds(rp, halo), :] * jnp.where(first, 0.0, 1.0)
        nxt = xbc_ref[0, pl.ds(rn, halo), :] * jnp.where(last, 0.0, 1.0)
        ext = jnp.concatenate([prev, cur, nxt], axis=0)
        acc = jnp.broadcast_to(cb, (c, 2 * LANES))
        for j in range(SSM_CONV):
            s = (SSM_CONV // 2 - j) % (c + 2 * halo)
            sh = ext if s == 0 else pltpu.roll(ext, s, 0)
            acc = acc + cw_ref[j:j + 1, :] * sh[halo:halo + c, :]
        xa_ref[pl.ds(r0, c), :] = _silu(acc)
        return carry

    lax.fori_loop(0, nc, conv_body, 0)

    def intra(i, carry, nb, first):
        cis = [first + i * nb + u for u in range(nb)]
        rws = [pl.ds(pl.multiple_of(ci * c, c), c) for ci in cis]
        res = _ssd_intra([(xa_ref[rows, :], dt_ref[0, rows, :]) for rows in rws], cst=cst, dtb=dtb, a_neg=a_neg)
        for ci, rows, (y, cm, outs) in zip(cis, rws, res):
            ya_ref[rows, :] = y
            cm_ref[rows, :] = cm
            for (dec_out, upd, dec), do_ref, upd_ref, dec_ref in zip(outs, (dof_ref, dob_ref), (updf_ref, updb_ref), (decf_ref, decb_ref)):
                do_ref[rows, :] = dec_out
                upd_ref[ci] = upd
                dec_ref[ci] = jnp.broadcast_to(dec, (SUBLANES, LANES))
        return carry

    nbl = SSM_BATCH if nl % SSM_BATCH == 0 else 1
    nbc = ncc if ncc <= SSM_BATCH else 1
    lax.fori_loop(0, nl // nbl, functools.partial(intra, nb=nbl, first=0), 0)
    lax.fori_loop(0, ncc // nbc, functools.partial(intra, nb=nbc, first=nl), 0)

    def scan(i, carry):
        st_f, st_b = carry
        jf = jnp.where(i < ncc, nl + i, i - ncc)
        rows = pl.ds(pl.multiple_of(jf * c, c), c)
        ya_ref[rows, :] = ya_ref[rows, :] + _dot(cm_ref[rows, :], st_f.astype(BF16)) * dof_ref[rows, :]
        st_f = st_f * decf_ref[jf][0:1, :] + updf_ref[jf]
        ib = nc - 1 - i
        rows = pl.ds(pl.multiple_of(ib * c, c), c)
        ya_ref[rows, :] = ya_ref[rows, :] + _dot(cm_ref[rows, :], st_b.astype(BF16)) * dob_ref[rows, :]
        st_b = st_b * decb_ref[ib][0:1, :] + updb_ref[ib]
        return st_f, st_b

    zst = jnp.zeros((LANES, LANES), F32)
    lax.fori_loop(0, nc, scan, (zst, zst), unroll=2)

    dsk = dsk_ref[...]
    ng = ng_ref[...]
    tr = ROW_TILE

    def readout(i, carry):
        rows = pl.ds(pl.multiple_of(i * tr, tr), tr)
        y = ya_ref[rows, :] + dsk * xa_ref[rows, 0:LANES]
        yz = y * _silu(z_ref[0, rows, :])
        y_ref[0, rows, :] = _rms(yz, ng).astype(y_ref.dtype)
        return carry

    lax.fori_loop(0, n_all // tr, readout, 0)


def _ssd_call(p_ssm, conv_w, conv_b, dtb, alog, dsk, ng, n_lat):
    b, n_all, _ = p_ssm.shape
    assert SSM_CHUNK == LANES
    nc = n_all // SSM_CHUNK
    vec = pl.BlockSpec((1, LANES), lambda bb, g: (0, g))
    vec2 = pl.BlockSpec((1, 2 * LANES), lambda bb, g: (0, g))
    scratch = [pltpu.VMEM((n_all, 2 * LANES), F32), pltpu.VMEM((n_all, LANES), F32), pltpu.VMEM((n_all, LANES), BF16),
               pltpu.VMEM((n_all, LANES), F32), pltpu.VMEM((n_all, LANES), F32),
               pltpu.VMEM((nc, LANES, LANES), F32), pltpu.VMEM((nc, LANES, LANES), F32),
               pltpu.VMEM((nc, SUBLANES, LANES), F32), pltpu.VMEM((nc, SUBLANES, LANES), F32)]
    est = 2 * (n_all * 5 * LANES * 4 + n_all * LANES * 2) + n_all * 6 * LANES * 4 + 2 * nc * LANES * LANES * 4 + 8 * 1024 * 1024
    return pl.pallas_call(
        functools.partial(_ssd_kernel, n_lat=n_lat),
        grid=(b, SSM_GROUPS),
        in_specs=[
            pl.BlockSpec((1, n_all, LANES), lambda bb, g: (bb, 0, g)),
            pl.BlockSpec((1, n_all, 2 * LANES), lambda bb, g: (bb, 0, 1 + g)),
            pl.BlockSpec((1, n_all, 2 * LANES), lambda bb, g: (bb, 0, 3 + g)),
            pl.BlockSpec((SSM_CONV, 2 * LANES), lambda bb, g: (0, g)),
            pl.BlockSpec((1, 2 * LANES), lambda bb, g: (0, g)),
            vec2, vec2, vec, vec,
        ],
        out_specs=pl.BlockSpec((1, n_all, LANES), lambda bb, g: (bb, 0, g)),
        out_shape=jax.ShapeDtypeStruct((b, n_all, SSM_INNER), BF16),
        scratch_shapes=scratch,
        compiler_params=pltpu.CompilerParams(vmem_limit_bytes=_vmem_limit(est)),
        name="ssd_bidir",
    )(p_ssm, p_ssm, p_ssm, conv_w, conv_b, dtb, alog, dsk, ng)


def _ffn_kernel(xl_ref, xc_ref, a_ref, r_ref, s_ref, mod_ref, g2_ref, gf_ref, wa_ref, wr_ref, ws_ref, w1_ref, w2_ref,
                ol_ref, oc_ref=None, *, hid_cuts, final):
    d = xl_ref.shape[2]
    hidden = w2_ref.shape[0]
    mod = lambda j: mod_ref[0, 0, :, j * d:(j + 1) * d]

    def tile(rows, is_ctx):
        r = slice(0, rows)
        x_ref, o_ref = (xc_ref, oc_ref) if is_ctx else (xl_ref, ol_ref)
        mix = _dot(a_ref[0, r, :], wa_ref[...]) + _dot(r_ref[0, r, :], wr_ref[...]) + _dot(s_ref[0, r, :], ws_ref[...])
        x1 = x_ref[0] + mod(2) * mix
        h2 = (_rms(x1, g2_ref[...]) * (1.0 + mod(4)) + mod(3)).astype(BF16)
        y = None
        for c0, c1 in zip(hid_cuts[:-1], hid_cuts[1:]):
            ha = _dot(h2, w1_ref[:, c0:c1])
            hb = _dot(h2, w1_ref[:, hidden + c0:hidden + c1])
            act = (_silu(ha) * hb).astype(BF16)
            yc = _dot(act, w2_ref[c0:c1, :])
            y = yc if y is None else y + yc
        x2 = x1 + mod(5) * y
        o_ref[0] = _rms(x2, gf_ref[...]) if final else x2

    _lat_or_ctx(tile, xl_ref.shape[1], None if final else xc_ref.shape[1], pl.program_id(1), pl.num_programs(1))


def _ffn_call(x_lat, x_ctx, a, r, s, modsel, g2, gf, wa, wr, ws, w1, w2, final):
    b, n_lat, d = x_lat.shape
    n_ctx = x_ctx.shape[1]
    tm = TOKEN_TILE
    hidden = w2.shape[0]
    n_tiles = hidden // MXU_TILE
    assert hidden % MXU_TILE == 0
    hid_cuts = tuple(MXU_TILE * ((n_tiles * j + FFN_SPLIT - 1) // FFN_SPLIT) for j in range(FFN_SPLIT + 1))
    hid_chunk = max(b1 - b0 for b0, b1 in zip(hid_cuts[:-1], hid_cuts[1:]))
    nlt = n_lat // tm
    full = lambda arr: pl.BlockSpec(arr.shape, lambda bb, i: (0,) * arr.ndim, pipeline_mode=pl.Buffered(1))
    tok = lambda w: pl.BlockSpec((1, tm, w), lambda bb, i: (bb, i, 0))
    lat = pl.BlockSpec((1, tm, d), lambda bb, i: (bb, jnp.minimum(i, nlt - 1), 0))
    ctx = pl.BlockSpec((1, n_ctx, d), lambda bb, i: (bb, 0, 0))
    est = (wa.size + wr.size + ws.size + w1.size + w2.size) * 2 + 2 * (2 * tm * d * 4 + 2 * n_ctx * d * 4 + tm * 1024 * 2) \
        + 5 * tm * hid_chunk * 4 + 5 * tm * d * 4
    return pl.pallas_call(
        functools.partial(_ffn_kernel, hid_cuts=hid_cuts, final=final),
        grid=(b, nlt if final else nlt + 1),
        in_specs=[
            lat, ctx, tok(a.shape[2]), tok(r.shape[2]), tok(s.shape[2]),
            pl.BlockSpec((1, 1, 1, modsel.shape[3]), lambda bb, i: (bb, i // nlt, 0, 0)),
            pl.BlockSpec((1, d), lambda bb, i: (0, 0)),
            pl.BlockSpec((1, d), lambda bb, i: (0, 0)),
            full(wa), full(wr), full(ws), full(w1), full(w2),
        ],
        out_specs=[lat] if final else [lat, ctx],
        out_shape=[jax.ShapeDtypeStruct((b, n_lat, d), F32)] + ([] if final else [jax.ShapeDtypeStruct((b, n_ctx, d), F32)]),
        compiler_params=pltpu.CompilerParams(vmem_limit_bytes=_vmem_limit(est)),
        name="outproj_ffn",
    )(x_lat, x_ctx, a, r, s, modsel, g2, gf, wa, wr, ws, w1, w2)


def _win_perm():
    perm = np.full((P_TOTAL,), -1, np.int64)
    perm[P_QLAT:P_QLAT + MLA_Q_RANK + MLA_KV_RANK] = np.arange(MLA_Q_RANK + MLA_KV_RANK)
    perm[P_KR + MLA_NOPE:P_KR + MLA_NOPE + ROPE_LANES] = MLA_Q_RANK + MLA_KV_RANK + ROPE_PERM
    perm[P_HG:P_HG + HG_COLS] = MLA_COLS + np.arange(HG_COLS)
    o2 = MLA_COLS + HG_COLS
    perm[P_SSM:P_SSM + SSM_INNER] = o2 + np.arange(SSM_INNER)
    xo = o2 + SSM_INNER
    bo = xo + SSM_INNER
    co = bo + SSM_GROUPS * SSM_STATE
    dto = o2 + SSM_INNER + SSM_XBC
    for g in range(SSM_GROUPS):
        base = P_SSM + SSM_INNER + g * 2 * LANES
        perm[base:base + LANES] = xo + g * LANES + np.arange(LANES)
        perm[base + LANES:base + LANES + SSM_STATE] = bo + g * SSM_STATE + np.arange(SSM_STATE)
        perm[base + LANES + SSM_STATE:base + 2 * LANES] = co + g * SSM_STATE + np.arange(SSM_STATE)
        dbase = P_SSM + SSM_INNER + 2 * 2 * LANES + g * 2 * LANES
        for d in range(2):
            for j in range(2):
                lo = dbase + d * LANES + j * SSM_HEADDIM
                perm[lo:lo + SSM_HEADDIM] = dto + d * SSM_HEADS + 2 * g + j
    return perm


def _conv_perm():
    perm = np.zeros((SSM_XBC,), np.int64)
    for g in range(SSM_GROUPS):
        base = g * 2 * LANES
        perm[base:base + LANES] = g * LANES + np.arange(LANES)
        perm[base + LANES:base + LANES + SSM_STATE] = SSM_INNER + g * SSM_STATE + np.arange(SSM_STATE)
        perm[base + LANES + SSM_STATE:base + 2 * LANES] = SSM_INNER + SSM_GROUPS * SSM_STATE + g * SSM_STATE + np.arange(SSM_STATE)
    return perm


def _gather_cols(w, perm):
    idx = jnp.asarray(np.maximum(perm, 0), jnp.int32)
    out = jnp.take(w, idx, axis=-1)
    return jnp.where(jnp.asarray(perm >= 0), out, 0.0)


def _head_vec(v):
    depth = v.shape[0]
    v5 = v.reshape(depth, 2, SSM_GROUPS, 2, 1)
    v5 = jnp.broadcast_to(v5, (depth, 2, SSM_GROUPS, 2, SSM_HEADDIM))
    return jnp.transpose(v5, (0, 2, 1, 3, 4)).reshape(depth, 1, SSM_GROUPS * 2 * LANES)


def _rope_tables(n_ctx, n_lat, scale):
    rows = n_lat // GRID_W
    row = jnp.repeat(jnp.arange(rows, dtype=F32), GRID_W)
    col = jnp.tile(jnp.arange(GRID_W, dtype=F32), rows)
    n_freq = MLA_ROPE // 4
    inv = ROPE_BASE ** (-jnp.arange(n_freq, dtype=F32) / n_freq)
    ang = jnp.stack([row[:, None] * inv, col[:, None] * inv], axis=1)
    cos, sin = jnp.cos(ang), jnp.sin(ang)
    zf = jnp.zeros((n_lat, ROPE_F), F32)
    c_r = jnp.concatenate([cos[:, 0], cos[:, 0], zf, cos[:, 1], cos[:, 1], zf], axis=-1)
    s_r = jnp.concatenate([-sin[:, 0], sin[:, 0], zf, -sin[:, 1], sin[:, 1], zf], axis=-1)
    pad = jnp.zeros((n_lat, LANES - MLA_NOPE - ROPE_LANES), F32)
    c_lat = jnp.concatenate([jnp.ones((n_lat, MLA_NOPE), F32), c_r, pad], axis=-1)
    s_lat = jnp.concatenate([jnp.zeros((n_lat, MLA_NOPE), F32), s_r, pad], axis=-1)
    keep = jnp.asarray(np.concatenate([np.ones(MLA_NOPE), np.tile(np.repeat([1.0, 1.0, 0.0], ROPE_F), 2),
                                       np.zeros(LANES - MLA_NOPE - ROPE_LANES)]), F32)[None, :]
    c_all = jnp.concatenate([c_lat, jnp.broadcast_to(keep, (n_ctx, LANES))], axis=0)
    s_all = jnp.concatenate([s_lat, jnp.zeros((n_ctx, LANES), F32)], axis=0)
    return c_all * scale, s_all * scale, c_all, s_all


def kernel(x, c, ctx, c_ctx, w_ada, b_ada, norm1_g, norm2_g, w_in, mla_qa_g, mla_wqb, mla_kva_g, mla_wkvb, hg_lb_logits, hg_norm_g, ssm_conv_w, ssm_conv_b, ssm_dt_bias, ssm_a_log, ssm_d, ssm_norm_g, w_out, w_ffn_in, w_ffn_out, final_g):
    bsz, n_lat, d = x.shape
    n_ctx = ctx.shape[1]
    depth = w_ada.shape[0]
    assert n_lat % TOKEN_TILE == 0 and n_lat % GRID_W == 0 and n_ctx <= TOKEN_TILE
    assert n_ctx % (HG_BATCH * HG_CHUNK) == 0 and n_ctx % SSM_CHUNK == 0 and n_ctx % ROW_TILE == 0

    rows = -(-(bsz + 1) // SUBLANES) * SUBLANES
    cc = jnp.zeros((rows, d), F32).at[:bsz].set(c).at[bsz].set(c_ctx)
    mods = _mod_call(cc, w_ada.astype(BF16), b_ada.reshape(depth, 1, 6 * d))
    modsel = jnp.stack([mods[:, :bsz], jnp.broadcast_to(mods[:, bsz:bsz + 1], (depth, bsz, 6 * d))], axis=2)[:, :, :, None, :]

    w_in_p = _gather_cols(w_in, _win_perm()).astype(BF16)
    head_cols = np.full((LANES,), -1, np.int64)
    head_cols[:MLA_NOPE] = np.arange(MLA_NOPE)
    head_cols[MLA_NOPE:MLA_NOPE + ROPE_LANES] = MLA_NOPE + ROPE_PERM
    wqb_perm = np.concatenate([np.where(head_cols >= 0, hh * MLA_QK + head_cols, -1) for hh in range(MLA_HEADS)])
    wqb_p = _gather_cols(mla_wqb, wqb_perm).astype(BF16)
    wkvb_b = mla_wkvb.astype(BF16)
    cperm = _conv_perm()
    conv_w_p = jnp.take(ssm_conv_w, jnp.asarray(cperm, jnp.int32), axis=-1)
    conv_b_p = jnp.take(ssm_conv_b, jnp.asarray(cperm, jnp.int32), axis=-1).reshape(depth, 1, SSM_XBC)
    dtb_p = _head_vec(ssm_dt_bias)
    alog_p = _head_vec(ssm_a_log)
    dsk_p = jnp.repeat(ssm_d, SSM_HEADDIM, axis=-1).reshape(depth, 1, SSM_INNER)
    lb_soft = jax.nn.softmax(hg_lb_logits.astype(F32), axis=0)
    lb_all = jnp.cumsum(lb_soft, axis=0) - lb_soft[0]
    w_out_b = w_out.astype(BF16)
    a_w = MLA_HEADS * MLA_V
    r_w = HG_HEADS * HG_DV
    w1_b = w_ffn_in.astype(BF16)
    w2_b = w_ffn_out.astype(BF16)
    tabs = _rope_tables(n_ctx, n_lat, MLA_QK ** -0.5 * float(np.log2(np.e)))

    streams = [x, ctx]
    for l in range(depth):
        q, k, v, p_hg, p_ssm = _inproj_call(
            *streams, modsel[l], norm1_g[l].reshape(1, d), w_in_p[l], mla_qa_g[l].reshape(1, -1),
            mla_kva_g[l].reshape(1, -1), wqb_p[l], wkvb_b[l], tabs)
        a = _attn_call(q, k, v, n_lat)
        r = _hgrn_call(p_hg, lb_all[l], hg_norm_g[l].reshape(1, -1), n_lat)
        s = _ssd_call(p_ssm, conv_w_p[l], conv_b_p[l], dtb_p[l], alog_p[l], dsk_p[l],
                      ssm_norm_g[l].reshape(1, -1), n_lat)
        new = _ffn_call(*streams, a, r, s, modsel[l], norm2_g[l].reshape(1, d), final_g.reshape(1, d),
                        w_out_b[l, :a_w], w_out_b[l, a_w:a_w + r_w], w_out_b[l, a_w + r_w:], w1_b[l], w2_b[l],
                        final=(l == depth - 1))
        streams = [new[0], new[1] if len(new) > 1 else streams[1]]
    return streams[0]
```
